```python
import jax, jax.numpy as jnp
from jax import lax
import numpy as np

D_MODEL = 1024
BATCH = 4
SEQ = 8192
DEPTH = 4

HEAD_DIM = 64
N_SB_HEADS = 4
N_MOBA_HEADS = 4
N_MLA_HEADS = 4
N_NSA_HEADS = 4
N_MEM_HEADS = 4
MEM_LEN = 256
BRANCH_W = 4 * HEAD_DIM
N_BRANCHES = 5
Q_BLOCK = 128
MOBA_BLOCK = 256
MOBA_TOPK = 3
MLA_Q_RANK = 256
MLA_KV_RANK = 128
MLA_NOPE = 64
MLA_ROPE = 32
MLA_V = 64
ROPE_THETA = 10000.0
NSA_CMP_LEN = 32
NSA_CMP_STRIDE = 16
NSA_SEL_LEN = 64
NSA_TOPN = 16
NSA_WINDOW = 512
NSA_PHI_HIDDEN = 128
N_GROUPS = 4
EXPERTS_PER_GROUP = 4
N_EXPERTS = N_GROUPS * EXPERTS_PER_GROUP
D_EXPERT = 256
TOPK_IN_GROUP = 2
DEEPNORM_ALPHA = (2.0 * DEPTH) ** 0.25
DEEPNORM_BETA = (8.0 * DEPTH) ** -0.25
LN_EPS = 1e-5
RMS_EPS = 1e-6
NEG = -1e30
BIG = 1e30

IN_SIZES = (3 * N_SB_HEADS * HEAD_DIM,
            3 * N_MOBA_HEADS * HEAD_DIM,
            MLA_Q_RANK,
            MLA_KV_RANK,
            MLA_ROPE,
            N_NSA_HEADS * HEAD_DIM,
            6 * HEAD_DIM,
            3 * N_NSA_HEADS,
            N_MEM_HEADS * HEAD_DIM)
IN_TOTAL = sum(IN_SIZES)
IN_SPLITS = tuple(int(c) for c in np.cumsum(IN_SIZES)[:-1])

kernel_name = 'hybrid_gated_sparse_mixers_hmoe'


def layer_norm(x, g, b):
    xf = x.astype(jnp.float32)
    mu = jnp.mean(xf, axis=-1, keepdims=True)
    var = jnp.mean(jnp.square(xf - mu), axis=-1, keepdims=True)
    return ((xf - mu) * lax.rsqrt(var + LN_EPS) * g + b).astype(x.dtype)


def rms_norm(x, g):
    xf = x.astype(jnp.float32)
    return (xf * lax.rsqrt(jnp.mean(jnp.square(xf), axis=-1, keepdims=True) + RMS_EPS) * g).astype(x.dtype)


def heads(x, n):
    b, s, _ = x.shape
    return x.reshape(b, s, n, -1).transpose(0, 2, 1, 3)


def merge_heads(o):
    b, n, s, d = o.shape
    return o.transpose(0, 2, 1, 3).reshape(b, s, n * d)


def alibi_slopes(n):
    return jnp.power(2.0, -8.0 * jnp.arange(1, n + 1, dtype=jnp.float32) / n)


def rope(x, pos):
    half = x.shape[-1] // 2
    freqs = jnp.power(ROPE_THETA, -jnp.arange(half, dtype=jnp.float32) / half)
    ang = pos.astype(jnp.float32)[:, None] * freqs
    cos, sin = jnp.cos(ang), jnp.sin(ang)
    xf = x.astype(jnp.float32)
    x1, x2 = xf[..., :half], xf[..., half:]
    return jnp.concatenate([x1 * cos - x2 * sin, x1 * sin + x2 * cos], axis=-1).astype(x.dtype)


def sweep_query_blocks(block_fn, seq_len):
    out = lax.map(block_fn, jnp.arange(seq_len // Q_BLOCK))
    nb, b, h, q, dv = out.shape
    return out.transpose(1, 2, 0, 3, 4).reshape(b, h, nb * q, dv)


def stick_breaking_attention(q, k, v):
    s_len, d = q.shape[2], q.shape[3]
    scale = d ** -0.5
    kpos = jnp.arange(s_len)

    def block(i):
        q0 = i * Q_BLOCK
        qb = lax.dynamic_slice_in_dim(q, q0, Q_BLOCK, axis=2)
        qpos = q0 + jnp.arange(Q_BLOCK)
        z = jnp.einsum('bhqd,bhkd->bhqk', qb, k).astype(jnp.float32) * scale
        past = kpos[None, :] < qpos[:, None]
        log_keep = jnp.where(past, jax.nn.log_sigmoid(-z), 0.0)
        later = lax.cumsum(log_keep, axis=3, reverse=True) - log_keep
        a = jnp.where(past, jnp.exp(jax.nn.log_sigmoid(z) + later), 0.0)
        return jnp.einsum('bhqk,bhkd->bhqd', a.astype(v.dtype), v)

    return sweep_query_blocks(block, s_len)


def moba_attention(q, k, v, slopes):
    b, h, s_len, d = q.shape
    scale = d ** -0.5
    nblk = s_len // MOBA_BLOCK
    kb = k.reshape(b, h, nblk, MOBA_BLOCK, d)
    vb = v.reshape(b, h, nblk, MOBA_BLOCK, d)
    k_mean = jnp.mean(kb.astype(jnp.float32), axis=3).astype(k.dtype)
    topk = min(MOBA_TOPK, nblk - 1)
    blk_ids = jnp.arange(nblk)
    in_blk = jnp.arange(MOBA_BLOCK)
    bi = jnp.arange(b)[:, None, None, None]
    hi = jnp.arange(h)[None, :, None, None]

    def block(i):
        q0 = i * Q_BLOCK
        qb = lax.dynamic_slice_in_dim(q, q0, Q_BLOCK, axis=2)
        qpos = q0 + jnp.arange(Q_BLOCK)
        own = q0 // MOBA_BLOCK
        k_own = lax.dynamic_index_in_dim(kb, own, axis=2, keepdims=False)
        v_own = lax.dynamic_index_in_dim(vb, own, axis=2, keepdims=False)
        own_pos = own * MOBA_BLOCK + in_blk
        dist_own = (qpos[:, None] - own_pos[None, :]).astype(jnp.float32)
        s_own = jnp.einsum('bhqd,bhkd->bhqk', qb, k_own).astype(jnp.float32) * scale - slopes[:, None, None] * dist_own
        s_own = jnp.where(own_pos[None, :] <= qpos[:, None], s_own, NEG)
        if topk == 0:
            p = jax.nn.softmax(s_own, axis=-1).astype(v.dtype)
            return jnp.einsum('bhqk,bhkd->bhqd', p, v_own)
        gscore = jnp.einsum('bhqd,bhnd->bhqn', qb, k_mean).astype(jnp.float32)
        gscore = jnp.where(blk_ids[None, None, None, :] < own, gscore, NEG)
        gval, gidx = lax.top_k(gscore, topk)
        sel_ok = gval > 0.5 * NEG
        k_sel = kb[bi, hi, gidx]
        v_sel = vb[bi, hi, gidx]
        sel_pos = gidx[..., None] * MOBA_BLOCK + in_blk
        dist_sel = (qpos[:, None, None] - sel_pos).astype(jnp.float32)
        s_sel = jnp.einsum('bhqd,bhqnkd->bhqnk', qb, k_sel).astype(jnp.float32) * scale - slopes[:, None, None, None] * dist_sel
        s_sel = jnp.where(sel_ok[..., None], s_sel, NEG).reshape(b, h, Q_BLOCK, topk * MOBA_BLOCK)
        p = jax.nn.softmax(jnp.concatenate([s_sel, s_own], axis=-1), axis=-1).astype(v.dtype)
        p_sel = p[..., :topk * MOBA_BLOCK].reshape(b, h, Q_BLOCK, topk, MOBA_BLOCK)
        p_own = p[..., topk * MOBA_BLOCK:]
        return (jnp.einsum('bhqnk,bhqnkd->bhqd', p_sel, v_sel)
                + jnp.einsum('bhqk,bhkd->bhqd', p_own, v_own))

    return sweep_query_blocks(block, s_len)


def dense_causal_attention(q, k, v):
    s_len, dq = q.shape[2], q.shape[3]
    scale = dq ** -0.5
    kpos = jnp.arange(s_len)

    def block(i):
        q0 = i * Q_BLOCK
        qb = lax.dynamic_slice_in_dim(q, q0, Q_BLOCK, axis=2)
        qpos = q0 + jnp.arange(Q_BLOCK)
        s = jnp.einsum('bhqd,bhkd->bhqk', qb, k).astype(jnp.float32) * scale
        s = jnp.where(kpos[None, :] <= qpos[:, None], s, NEG)
        p = jax.nn.softmax(s, axis=-1).astype(v.dtype)
        return jnp.einsum('bhqk,bhkd->bhqd', p, v)

    return sweep_query_blocks(block, s_len)


def mla_attention(c_q, c_kv, k_rope_in, g_cq, g_ckv, w_uq, w_ukv):
    b, s_len, _ = c_q.shape
    pos = jnp.arange(s_len)
    q = heads(rms_norm(c_q, g_cq) @ w_uq, N_MLA_HEADS)
    q = jnp.concatenate([q[..., :MLA_NOPE], rope(q[..., MLA_NOPE:], pos)], axis=-1)
    kv = heads(rms_norm(c_kv, g_ckv) @ w_ukv, N_MLA_HEADS)
    k_r = jnp.broadcast_to(rope(k_rope_in, pos)[:, None], (b, N_MLA_HEADS, s_len, MLA_ROPE))
    k = jnp.concatenate([kv[..., :MLA_NOPE], k_r], axis=-1)
    v = kv[..., MLA_NOPE:]
    return merge_heads(dense_causal_attention(q, k, v))


def nsa_compress(x, pe, w1, w2):
    b, s_len, d = x.shape
    n_chunks = s_len // NSA_CMP_STRIDE
    ratio = NSA_CMP_LEN // NSA_CMP_STRIDE
    chunks = x.reshape(b, n_chunks, NSA_CMP_STRIDE, d)
    blocks = jnp.concatenate([chunks[:, r:n_chunks - ratio + 1 + r] for r in range(ratio)], axis=2)
    flat = (blocks + pe).reshape(b, n_chunks - ratio + 1, NSA_CMP_LEN * d)
    return jax.nn.gelu(flat @ w1) @ w2


def nsa_attention(q, kv, gate_logits, pe, w_k1, w_k2, w_v1, w_v2, slopes):
    b, h, s_len, d = q.shape
    scale = d ** -0.5
    k_cmp, v_cmp, k_slc, v_slc, k_win, v_win = jnp.split(kv, 6, axis=-1)
    gates = jax.nn.sigmoid(gate_logits).reshape(b, s_len, h, 3).transpose(0, 2, 1, 3)
    k_c = nsa_compress(k_cmp, pe, w_k1, w_k2)
    v_c = nsa_compress(v_cmp, pe, w_v1, w_v2)
    n_cmp = k_c.shape[1]
    cmp_start = jnp.arange(n_cmp) * NSA_CMP_STRIDE
    cmp_end = cmp_start + NSA_CMP_LEN - 1
    n_sel = s_len // NSA_SEL_LEN
    sel_ids = jnp.arange(n_sel)
    sel_start = sel_ids * NSA_SEL_LEN
    overlap = ((cmp_start[:, None] < sel_start[None, :] + NSA_SEL_LEN)
               & (cmp_start[:, None] + NSA_CMP_LEN > sel_start[None, :])).astype(jnp.float32)
    topn = min(NSA_TOPN, n_sel)
    ks_b = k_slc.reshape(b, n_sel, NSA_SEL_LEN, d)
    vs_b = v_slc.reshape(b, n_sel, NSA_SEL_LEN, d)
    pad = jnp.zeros((b, NSA_WINDOW, d), k_win.dtype)
    kw_pad = jnp.concatenate([pad, k_win], axis=1)
    vw_pad = jnp.concatenate([pad, v_win], axis=1)
    bi = jnp.arange(b)[:, None, None]
    in_sel = jnp.arange(NSA_SEL_LEN)

    def block(i):
        q0 = i * Q_BLOCK
        qb = lax.dynamic_slice_in_dim(q, q0, Q_BLOCK, axis=2)
        gb = lax.dynamic_slice_in_dim(gates, q0, Q_BLOCK, axis=2)
        qpos = q0 + jnp.arange(Q_BLOCK)
        dist_c = (qpos[:, None] - cmp_end[None, :]).astype(jnp.float32)
        m_c = cmp_end[None, :] <= qpos[:, None]
        s_c = jnp.einsum('bhqd,bnd->bhqn', qb, k_c).astype(jnp.float32) * scale - slopes[:, None, None] * dist_c
        p_c = jnp.where(m_c, jax.nn.softmax(jnp.where(m_c, s_c, NEG), axis=-1), 0.0)
        o_c = jnp.einsum('bhqn,bnd->bhqd', p_c.astype(v_c.dtype), v_c)
        imp = jnp.einsum('bhqn,ns->bqs', p_c, overlap)
        cur = qpos // NSA_SEL_LEN
        forced = (sel_ids[None, :] == 0) | (sel_ids[None, :] == cur[:, None])
        past = sel_ids[None, :] < cur[:, None]
        score = jnp.where(forced, BIG, jnp.where(past, imp, NEG))
        val, idx = lax.top_k(score, topn)
        ok = val > 0.5 * NEG
        k_s = ks_b[bi, idx]
        v_s = vs_b[bi, idx]
        pos_s = idx[..., None] * NSA_SEL_LEN + in_sel
        dist_s = (qpos[:, None, None] - pos_s).astype(jnp.float32)[:, None]
        m_s = (ok[..., None] & (pos_s <= qpos[:, None, None]))[:, None]
        s_s = jnp.einsum('bhqd,bqnkd->bhqnk', qb, k_s).astype(jnp.float32) * scale - slopes[:, None, None, None] * dist_s
        s_s = jnp.where(m_s, s_s, NEG).reshape(b, h, Q_BLOCK, topn * NSA_SEL_LEN)
        p_s = jax.nn.softmax(s_s, axis=-1).astype(v_s.dtype).reshape(b, h, Q_BLOCK, topn, NSA_SEL_LEN)
        o_s = jnp.einsum('bhqnk,bqnkd->bhqd', p_s, v_s)
        k_w = lax.dynamic_slice_in_dim(kw_pad, q0, Q_BLOCK + NSA_WINDOW, axis=1)
        v_w = lax.dynamic_slice_in_dim(vw_pad, q0, Q_BLOCK + NSA_WINDOW, axis=1)
        pos_w = q0 - NSA_WINDOW + jnp.arange(Q_BLOCK + NSA_WINDOW)
        dist_w = qpos[:, None] - pos_w[None, :]
        m_w = (dist_w >= 0) & (dist_w < NSA_WINDOW) & (pos_w[None, :] >= 0)
        s_w = jnp.einsum('bhqd,bkd->bhqk', qb, k_w).astype(jnp.float32) * scale - slopes[:, None, None] * dist_w.astype(jnp.float32)
        p_w = jax.nn.softmax(jnp.where(m_w, s_w, NEG), axis=-1).astype(v_w.dtype)
        o_w = jnp.einsum('bhqk,bkd->bhqd', p_w, v_w)
        return gb[..., 0:1] * o_c + gb[..., 1:2] * o_s + gb[..., 2:3] * o_w

    return merge_heads(sweep_query_blocks(block, s_len))


def memory_cross_attention(q_mem, mem, w_mem_kv):
    q = heads(q_mem, N_MEM_HEADS)
    k, v = jnp.split(mem @ w_mem_kv, 2, axis=-1)
    k, v = heads(k, N_MEM_HEADS), heads(v, N_MEM_HEADS)
    s = jnp.einsum('bhqd,bhkd->bhqk', q, k).astype(jnp.float32) * (HEAD_DIM ** -0.5)
    p = jax.nn.softmax(s, axis=-1).astype(v.dtype)
    return merge_heads(jnp.einsum('bhqk,bhkd->bhqd', p, v))


def hybrid_mixer(x, mem, w_in, g_cq, g_ckv, w_uq, w_ukv, nsa_pe, w_phi_k1, w_phi_k2, w_phi_v1, w_phi_v2,
                 w_mem_kv, w_br, w_gate, b_gate, w_out):
    h = x @ w_in
    h_sb, h_moba, c_q, c_kv, k_rope_in, q_nsa, kv_nsa, g_nsa, q_mem = jnp.split(h, IN_SPLITS, axis=-1)
    q, k, v = jnp.split(h_sb, 3, axis=-1)
    o_sb = merge_heads(stick_breaking_attention(heads(q, N_SB_HEADS), heads(k, N_SB_HEADS), heads(v, N_SB_HEADS)))
    q, k, v = jnp.split(h_moba, 3, axis=-1)
    o_moba = merge_heads(moba_attention(heads(q, N_MOBA_HEADS), heads(k, N_MOBA_HEADS), heads(v, N_MOBA_HEADS),
                                        alibi_slopes(N_MOBA_HEADS)))
    o_mla = mla_attention(c_q, c_kv, k_rope_in, g_cq, g_ckv, w_uq, w_ukv)
    o_nsa = nsa_attention(heads(q_nsa, N_NSA_HEADS), kv_nsa, g_nsa, nsa_pe, w_phi_k1, w_phi_k2, w_phi_v1, w_phi_v2,
                          alibi_slopes(N_NSA_HEADS))
    o_mem = memory_cross_attention(q_mem, mem, w_mem_kv)
    merged = jnp.zeros_like(x)
    for i, o in enumerate((o_sb, o_moba, o_mla, o_nsa, o_mem)):
        gate = jax.nn.sigmoid(x @ w_gate[i] + b_gate[i])
        merged = merged + gate * (o @ w_br[i])
    return merged @ w_out


def hierarchical_moe(x, w_rg, b_rg, w_re, b_re, w_up, w_down):
    b, s_len, d = x.shape
    xt = x.reshape(b * s_len, d)
    glog = (xt @ w_rg + b_rg).astype(jnp.float32)
    pg = jax.nn.softmax(glog, axis=-1)
    g_sel = jnp.argmax(glog, axis=-1)
    g_oh = jax.nn.one_hot(g_sel, N_GROUPS, dtype=jnp.float32)
    pg_sel = jnp.sum(pg * g_oh, axis=-1)
    elog_all = (jnp.einsum('td,gde->tge', xt, w_re) + b_re).astype(jnp.float32)
    elog = jnp.einsum('tge,tg->te', elog_all, g_oh)
    pe = jax.nn.softmax(elog, axis=-1)
    val, idx = lax.top_k(pe, TOPK_IN_GROUP)
    weight = pg_sel[:, None] * val / jnp.sum(val, axis=-1, keepdims=True)
    expert_id = g_sel[:, None] * EXPERTS_PER_GROUP + idx
    gate_dense = jnp.sum(jax.nn.one_hot(expert_id, N_EXPERTS, dtype=jnp.float32) * weight[..., None], axis=1).astype(x.dtype)
    y = jnp.zeros_like(xt)
    for e in range(N_EXPERTS):
        a, u = jnp.split(xt @ w_up[e], 2, axis=-1)
        y = y + gate_dense[:, e:e + 1] * ((jax.nn.silu(a) * u) @ w_down[e])
    return y.reshape(b, s_len, d)


def setup_inputs(seed: int = 0) -> dict:
    key = jax.random.key(seed)
    ks = jax.random.split(key, 27)
    L = DEPTH

    def nrm(k, shape, scale):
        return jax.random.normal(k, shape, jnp.float32) * scale

    def gain(k, shape):
        return 1.0 + 0.02 * jax.random.normal(k, shape, jnp.float32)

    return {
        'x': nrm(ks[0], (BATCH, SEQ, D_MODEL), 1.0),
        'mem': nrm(ks[1], (BATCH, MEM_LEN, D_MODEL), 1.0),
        'w_in': nrm(ks[2], (L, D_MODEL, IN_TOTAL), D_MODEL ** -0.5),
        'g_cq': gain(ks[3], (L, MLA_Q_RANK)),
        'g_ckv': gain(ks[4], (L, MLA_KV_RANK)),
        'w_uq': nrm(ks[5], (L, MLA_Q_RANK, N_MLA_HEADS * (MLA_NOPE + MLA_ROPE)), MLA_Q_RANK ** -0.5),
        'w_ukv': nrm(ks[6], (L, MLA_KV_RANK, N_MLA_HEADS * (MLA_NOPE + MLA_V)), MLA_KV_RANK ** -0.5),
        'nsa_pe': nrm(ks[7], (L, NSA_CMP_LEN, HEAD_DIM), 0.1),
        'w_phi_k1': nrm(ks[8], (L, NSA_CMP_LEN * HEAD_DIM, NSA_PHI_HIDDEN), (NSA_CMP_LEN * HEAD_DIM) ** -0.5),
        'w_phi_k2': nrm(ks[9], (L, NSA_PHI_HIDDEN, HEAD_DIM), NSA_PHI_HIDDEN ** -0.5),
        'w_phi_v1': nrm(ks[10], (L, NSA_CMP_LEN * HEAD_DIM, NSA_PHI_HIDDEN), (NSA_CMP_LEN * HEAD_DIM) ** -0.5),
        'w_phi_v2': nrm(ks[11], (L, NSA_PHI_HIDDEN, HEAD_DIM), NSA_PHI_HIDDEN ** -0.5),
        'w_mem_kv': nrm(ks[12], (L, D_MODEL, 2 * N_MEM_HEADS * HEAD_DIM), D_MODEL ** -0.5),
        'w_br': nrm(ks[13], (L, N_BRANCHES, BRANCH_W, D_MODEL), DEEPNORM_BETA * BRANCH_W ** -0.5),
        'w_gate': nrm(ks[14], (L, N_BRANCHES, D_MODEL, D_MODEL), D_MODEL ** -0.5),
        'b_gate': nrm(ks[15], (L, N_BRANCHES, D_MODEL), 0.02),
        'w_out': nrm(ks[16], (L, D_MODEL, D_MODEL), DEEPNORM_BETA * D_MODEL ** -0.5),
        'ln1_g': gain(ks[17], (L, D_MODEL)),
        'ln1_b': nrm(ks[18], (L, D_MODEL), 0.02),
        'w_rg': nrm(ks[19], (L, D_MODEL, N_GROUPS), D_MODEL ** -0.5),
        'b_rg': nrm(ks[20], (L, N_GROUPS), 0.01),
        'w_re': nrm(ks[21], (L, N_GROUPS, D_MODEL, EXPERTS_PER_GROUP), D_MODEL ** -0.5),
        'b_re': nrm(ks[22], (L, N_GROUPS, EXPERTS_PER_GROUP), 0.01),
        'w_up': nrm(ks[23], (L, N_EXPERTS, D_MODEL, 2 * D_EXPERT), D_MODEL ** -0.5),
        'w_down': nrm(ks[24], (L, N_EXPERTS, D_EXPERT, D_MODEL), DEEPNORM_BETA * D_EXPERT ** -0.5),
        'ln2_g': gain(ks[25], (L, D_MODEL)),
        'ln2_b': nrm(ks[26], (L, D_MODEL), 0.02),
    }


def reference(x, mem, w_in, g_cq, g_ckv, w_uq, w_ukv, nsa_pe, w_phi_k1, w_phi_k2, w_phi_v1, w_phi_v2,
              w_mem_kv, w_br, w_gate, b_gate, w_out, ln1_g, ln1_b, w_rg, b_rg, w_re, b_re, w_up, w_down,
              ln2_g, ln2_b):
    s_len = x.shape[1]
    s_pad = -(-s_len // MOBA_BLOCK) * MOBA_BLOCK
    h = jnp.pad(x, ((0, 0), (0, s_pad - s_len), (0, 0)))
    for l in range(DEPTH):
        y = hybrid_mixer(h, mem, w_in[l], g_cq[l], g_ckv[l], w_uq[l], w_ukv[l], nsa_pe[l],
                         w_phi_k1[l], w_phi_k2[l], w_phi_v1[l], w_phi_v2[l], w_mem_kv[l],
                         w_br[l], w_gate[l], b_gate[l], w_out[l])
        h = layer_norm(DEEPNORM_ALPHA * h + y, ln1_g[l], ln1_b[l])
        y = hierarchical_moe(h, w_rg[l], b_rg[l], w_re[l], b_re[l], w_up[l], w_down[l])
        h = layer_norm(DEEPNORM_ALPHA * h + y, ln2_g[l], ln2_b[l])
    return h[:, :s_len]
```

```python
import functools

import jax
import jax.numpy as jnp
import numpy as np
from jax import lax
from jax.experimental import pallas as pl
from jax.experimental.pallas import tpu as pltpu

DEPTH = 4
HEAD_DIM = 64
N_HEADS = 4
BRANCH_W = N_HEADS * HEAD_DIM
N_BRANCHES = 5
MOBA_BLOCK = 256
MOBA_TOPK = 3
MLA_Q_RANK = 256
MLA_KV_RANK = 128
MLA_NOPE = 64
MLA_ROPE = 32
MLA_V = 64
ROPE_THETA = 10000.0
NSA_CMP_LEN = 32
NSA_CMP_STRIDE = 16
NSA_SEL_LEN = 64
NSA_TOPN = 16
NSA_WINDOW = 512
N_GROUPS = 4
EXPERTS_PER_GROUP = 4
N_EXPERTS = N_GROUPS * EXPERTS_PER_GROUP
D_EXPERT = 256
DEEPNORM_ALPHA = (2.0 * DEPTH) ** 0.25
LN_EPS = 1e-5
RMS_EPS = 1e-6
NEG = -1e30
BIG = 1e30

KEY_TILE = 256
NSA_Q = 128
VMEM_LIMIT_BYTES = 56 * 1024 * 1024

F32 = jnp.float32
BF16 = jnp.bfloat16
HIGHEST = lax.Precision.HIGHEST

COL_SB = 0
COL_MOBA = 768
COL_MLA = 1536
COL_NSA_Q = 2048
COL_NSA_KV = 2304
COL_NSA_G = 2688
COL_MEM_Q = 2816
IN_PAD = 3072


def _params(semantics):
    return pltpu.CompilerParams(dimension_semantics=semantics, vmem_limit_bytes=VMEM_LIMIT_BYTES)


def _dot(a, b):
    return jnp.dot(a, b, preferred_element_type=F32)


def _dot_exact(a, b):
    return jnp.dot(a, b, preferred_element_type=F32, precision=HIGHEST)


def _sigmoid(x):
    return 1.0 / (1.0 + jnp.exp(-x))


def _mm_kernel(x_ref, w_ref, o_ref, *, gelu):
    y = _dot(x_ref[...].astype(BF16), w_ref[...])
    if gelu:
        y = jax.nn.gelu(y)
    o_ref[...] = y.astype(o_ref.dtype)


def matmul(x, w, *, bm, out_dtype=F32, gelu=False):
    m, k = x.shape
    n = w.shape[1]
    return pl.pallas_call(
        functools.partial(_mm_kernel, gelu=gelu),
        grid=(m // bm,),
        in_specs=[pl.BlockSpec((bm, k), lambda i: (i, 0)),
                  pl.BlockSpec((k, n), lambda i: (0, 0))],
        out_specs=pl.BlockSpec((bm, n), lambda i: (i, 0)),
        out_shape=jax.ShapeDtypeStruct((m, n), out_dtype),
        compiler_params=_params(("parallel",)),
        name="matmul",
    )(x, w)


def _softmax_tile(s, v_t, m, l, acc):
    m_new = jnp.maximum(m, jnp.max(s, axis=0, keepdims=True))
    alpha = jnp.exp(m - m_new)
    p = jnp.exp(s - m_new)
    l = alpha * l + jnp.sum(p, axis=0, keepdims=True)
    acc = alpha * acc + _dot(v_t, p.astype(BF16))
    return m_new, l, acc


def _softmax_init(dv, q):
    return (jnp.full((1, q), NEG, F32), jnp.zeros((1, q), F32), jnp.zeros((dv, q), F32))


def _sb_kernel(qt_ref, k_ref, vt_ref, tri_ref, ot_ref, *, scale):
    qi = pl.program_id(2)
    qt = qt_ref[0, 0]
    tri = tri_ref[...]
    kb = tri.shape[0]
    qb = qt.shape[1]
    dv = vt_ref.shape[2]
    rk = lax.broadcasted_iota(jnp.int32, (kb, qb), 0)
    rq = lax.broadcasted_iota(jnp.int32, (kb, qb), 1)
    past = rk < rq

    def tile(j, carry, acc, diag):
        k0 = pl.multiple_of(j * kb, kb)
        z = _dot(k_ref[0, 0, pl.ds(k0, kb), :], qt) * scale
        nz = -z
        soft = jnp.log(1.0 + jnp.exp(jnp.minimum(z, nz)))
        log_keep = jnp.minimum(nz, 0.0) - soft
        log_beta = log_keep + z
        if diag:
            log_keep = jnp.where(past, log_keep, 0.0)
        hi = log_keep.astype(BF16)
        lo = (log_keep - hi.astype(F32)).astype(BF16)
        later = _dot(tri, jnp.concatenate([hi, lo], axis=0))
        a = jnp.exp(log_beta + later + carry)
        if diag:
            a = jnp.where(past, a, 0.0)
        acc = acc + _dot(vt_ref[0, 0, :, pl.ds(k0, kb)], a.astype(BF16))
        carry = carry + jnp.sum(log_keep, axis=0, keepdims=True)
        return carry, acc

    carry, acc = tile(qi, jnp.zeros((1, qb), F32), jnp.zeros((dv, qb), F32), True)
    carry, acc = lax.fori_loop(0, qi, lambda t, c: tile(qi - 1 - t, c[0], c[1], False), (carry, acc))
    ot_ref[0, 0] = acc


def sb_attention(qt, k, vt):
    b, h, d, s = qt.shape
    dv = vt.shape[2]
    kb = KEY_TILE
    upper = np.triu(np.ones((kb, kb), np.float32), 1)
    tri = jnp.asarray(np.concatenate([upper, upper], axis=1), BF16)
    return pl.pallas_call(
        functools.partial(_sb_kernel, scale=d ** -0.5),
        grid=(b, h, s // kb),
        in_specs=[pl.BlockSpec((1, 1, d, kb), lambda bi, hi, qi: (bi, hi, 0, qi)),
                  pl.BlockSpec((1, 1, s, d), lambda bi, hi, qi: (bi, hi, 0, 0)),
                  pl.BlockSpec((1, 1, dv, s), lambda bi, hi, qi: (bi, hi, 0, 0)),
                  pl.BlockSpec((kb, 2 * kb), lambda bi, hi, qi: (0, 0))],
        out_specs=pl.BlockSpec((1, 1, dv, kb), lambda bi, hi, qi: (bi, hi, 0, qi)),
        out_shape=jax.ShapeDtypeStruct((b, h, dv, s), F32),
        compiler_params=_params(("parallel", "parallel", "arbitrary")),
        name="sb_attention",
    )(qt, k, vt, tri)


def _alibi_slopes(n):
    return np.power(2.0, -8.0 * np.arange(1, n + 1, dtype=np.float64) / n).astype(np.float32)


def _block_mean_kernel(x_ref, o_ref):
    o_ref[0, 0] = jnp.mean(x_ref[0, 0], axis=0, keepdims=True)


def block_mean(x, block):
    b, s, c = x.shape
    n = s // block
    out = pl.pallas_call(
        _block_mean_kernel,
        grid=(b, n),
        in_specs=[pl.BlockSpec((1, 1, block, c), lambda bi, ni: (bi, ni, 0, 0))],
        out_specs=pl.BlockSpec((1, 1, 1, c), lambda bi, ni: (bi, ni, 0, 0)),
        out_shape=jax.ShapeDtypeStruct((b, n, 1, c), F32),
        compiler_params=_params(("parallel", "parallel")),
        name="block_mean",
    )(x.reshape(b, n, block, c))
    return out.reshape(b, n, c)


def _select_top(score, ids, count, floor):
    def step(_, c):
        score, sel = c
        mx = jnp.max(score, axis=0, keepdims=True)
        idx = jnp.min(jnp.where(score == mx, ids, 1e9), axis=0, keepdims=True)
        pick = ids == idx
        sel = jnp.where(jnp.logical_and(pick, mx > floor), 1.0, sel)
        score = jnp.where(pick, -jnp.inf, score)
        return score, sel
    return lax.fori_loop(0, count, step, (score, jnp.zeros_like(score)))[1]


def _moba_kernel(slope_ref, qt_ref, k_ref, vt_ref, km_ref, ot_ref, sel_ref, *, scale, topk):
    hi = pl.program_id(1)
    qi = pl.program_id(2)
    slope = slope_ref[hi]
    qt = qt_ref[0, 0]
    kb = qt.shape[1]
    dv = vt_ref.shape[2]
    nblk = km_ref.shape[2]
    gscore = _dot_exact(km_ref[0, 0], qt.astype(F32))
    ids = lax.broadcasted_iota(jnp.int32, (nblk, kb), 0)
    gscore = jnp.where(ids < qi, gscore, NEG)
    sel_ref[...] = _select_top(gscore, ids.astype(F32), topk, 0.5 * NEG)

    rk = lax.broadcasted_iota(jnp.int32, (kb, kb), 0)
    rq = lax.broadcasted_iota(jnp.int32, (kb, kb), 1)
    bias0 = slope * (rk - rq).astype(F32)

    def tile(j, c, diag):
        k0 = pl.multiple_of(j * kb, kb)
        s = _dot(k_ref[0, 0, pl.ds(k0, kb), :], qt) * scale
        if diag:
            s = jnp.where(rk <= rq, s + bias0, NEG)
        else:
            s = s + (bias0 - slope * ((qi - j) * kb).astype(F32))
            s = jnp.where(sel_ref[pl.ds(j, 1), :] > 0.5, s, NEG)
        return _softmax_tile(s, vt_ref[0, 0, :, pl.ds(k0, kb)], *c)

    c = tile(qi, _softmax_init(dv, kb), True)
    m, l, acc = lax.fori_loop(0, qi, lambda j, c: tile(j, c, False), c)
    ot_ref[0, 0] = acc / l


def moba_attention(qt, k, vt, k_mean):
    b, h, d, s = qt.shape
    dv = vt.shape[2]
    kb = MOBA_BLOCK
    nblk = s // kb
    topk = min(MOBA_TOPK, nblk - 1)
    slopes = jnp.asarray(_alibi_slopes(h))
    return pl.pallas_call(
        functools.partial(_moba_kernel, scale=d ** -0.5, topk=topk),
        grid_spec=pltpu.PrefetchScalarGridSpec(
            num_scalar_prefetch=1,
            grid=(b, h, nblk),
            in_specs=[pl.BlockSpec((1, 1, d, kb), lambda bi, hi, qi, sl: (bi, hi, 0, qi)),
                      pl.BlockSpec((1, 1, s, d), lambda bi, hi, qi, sl: (bi, hi, 0, 0)),
                      pl.BlockSpec((1, 1, dv, s), lambda bi, hi, qi, sl: (bi, hi, 0, 0)),
                      pl.BlockSpec((1, 1, nblk, d), lambda bi, hi, qi, sl: (bi, hi, 0, 0))],
            out_specs=pl.BlockSpec((1, 1, dv, kb), lambda bi, hi, qi, sl: (bi, hi, 0, qi)),
            scratch_shapes=[pltpu.VMEM((nblk, kb), F32)]),
        out_shape=jax.ShapeDtypeStruct((b, h, dv, s), F32),
        compiler_params=_params(("parallel", "parallel", "arbitrary")),
        name="moba_attention",
    )(slopes, qt, k, vt, k_mean)


def _rms(x, g):
    return x * lax.rsqrt(jnp.mean(x * x, axis=-1, keepdims=True) + RMS_EPS) * g


def _mla_proj_kernel(x_ref, gq_ref, gkv_ref, wq_ref, wqr_ref, wkv_ref, cq_ref, sq_ref, ck_ref, sk_ref,
                     q_ref, kv_ref, kr_ref):
    x = x_ref[...]
    c_q = _rms(x[:, :MLA_Q_RANK], gq_ref[...]).astype(BF16)
    c_kv = _rms(x[:, MLA_Q_RANK:MLA_Q_RANK + MLA_KV_RANK], gkv_ref[...]).astype(BF16)
    q_ref[...] = (_dot(c_q, wq_ref[...]) * cq_ref[...] + _dot(c_q, wqr_ref[...]) * sq_ref[...]).astype(q_ref.dtype)
    kv_ref[...] = _dot(c_kv, wkv_ref[...]).astype(kv_ref.dtype)
    r0 = MLA_Q_RANK + MLA_KV_RANK
    kr_ref[...] = (x[:, r0:r0 + MLA_ROPE] * ck_ref[...]
                   + x[:, r0 + MLA_ROPE:r0 + 2 * MLA_ROPE] * sk_ref[...]).astype(kr_ref.dtype)


def _rope_tables(s_len):
    half = MLA_ROPE // 2
    freqs = jnp.power(ROPE_THETA, -jnp.arange(half, dtype=F32) / half)
    ang = jnp.arange(s_len).astype(F32)[:, None] * freqs
    cos, sin = jnp.cos(ang), jnp.sin(ang)
    ck = jnp.concatenate([cos, cos], axis=-1)
    sk = jnp.concatenate([sin, sin], axis=-1)
    cq = jnp.tile(jnp.concatenate([jnp.ones((s_len, MLA_NOPE), F32), ck], axis=-1), (1, N_HEADS))
    sq = jnp.tile(jnp.concatenate([jnp.zeros((s_len, MLA_NOPE), F32), sk], axis=-1), (1, N_HEADS))
    return cq, sq, ck, sk


def _rotate_half_cols(w):
    half = w.shape[-1] // 2
    return jnp.concatenate([-w[..., half:], w[..., :half]], axis=-1)


def mla_project(hin, g_cq, g_ckv, w_uq, w_ukv, s_len, bm):
    t = hin.shape[0]
    dq = MLA_NOPE + MLA_ROPE
    wq = w_uq.reshape(MLA_Q_RANK, N_HEADS, dq)
    wqr = jnp.concatenate([jnp.zeros_like(wq[..., :MLA_NOPE]), _rotate_half_cols(wq[..., MLA_NOPE:])], axis=-1)
    cq, sq, ck, sk = _rope_tables(s_len)
    nrow = s_len // bm
    row = lambda i: (i, 0)
    pos = lambda i: (i % nrow, 0)
    const = lambda i: (0, 0)
    return pl.pallas_call(
        _mla_proj_kernel,
        grid=(t // bm,),
        in_specs=[pl.BlockSpec((bm, 512), lambda i: (i, COL_MLA // 512)),
                  pl.BlockSpec((1, MLA_Q_RANK), const),
                  pl.BlockSpec((1, MLA_KV_RANK), const),
                  pl.BlockSpec((MLA_Q_RANK, N_HEADS * dq), const),
                  pl.BlockSpec((MLA_Q_RANK, N_HEADS * dq), const),
                  pl.BlockSpec((MLA_KV_RANK, N_HEADS * (MLA_NOPE + MLA_V)), const),
                  pl.BlockSpec((bm, N_HEADS * dq), pos),
                  pl.BlockSpec((bm, N_HEADS * dq), pos),
                  pl.BlockSpec((bm, MLA_ROPE), pos),
                  pl.BlockSpec((bm, MLA_ROPE), pos)],
        out_specs=[pl.BlockSpec((bm, N_HEADS * dq), row),
                   pl.BlockSpec((bm, N_HEADS * (MLA_NOPE + MLA_V)), row),
                   pl.BlockSpec((bm, MLA_ROPE), row)],
        out_shape=[jax.ShapeDtypeStruct((t, N_HEADS * dq), BF16),
                   jax.ShapeDtypeStruct((t, N_HEADS * (MLA_NOPE + MLA_V)), BF16),
                   jax.ShapeDtypeStruct((t, MLA_ROPE), BF16)],
        compiler_params=_params(("parallel",)),
        name="mla_project",
    )(hin, g_cq.reshape(1, -1), g_ckv.reshape(1, -1), wq.reshape(MLA_Q_RANK, -1).astype(BF16),
      wqr.reshape(MLA_Q_RANK, -1).astype(BF16), w_ukv.astype(BF16), cq, sq, ck, sk)


def _causal_kernel(qt_ref, k_ref, vt_ref, ot_ref, *, scale):
    qi = pl.program_id(2)
    qt = qt_ref[0, 0]
    kb = qt.shape[1]
    dv = vt_ref.shape[2]
    rk = lax.broadcasted_iota(jnp.int32, (kb, kb), 0)
    rq = lax.broadcasted_iota(jnp.int32, (kb, kb), 1)

    def tile(j, c, diag):
        k0 = pl.multiple_of(j * kb, kb)
        s = _dot(k_ref[0, 0, pl.ds(k0, kb), :], qt) * scale
        if diag:
            s = jnp.where(rk <= rq, s, NEG)
        return _softmax_tile(s, vt_ref[0, 0, :, pl.ds(k0, kb)], *c)

    c = tile(qi, _softmax_init(dv, kb), True)
    m, l, acc = lax.fori_loop(0, qi, lambda j, c: tile(j, c, False), c)
    ot_ref[0, 0] = acc / l


def causal_attention(qt, k, vt):
    b, h, d, s = qt.shape
    dv = vt.shape[2]
    kb = KEY_TILE
    return pl.pallas_call(
        functools.partial(_causal_kernel, scale=d ** -0.5),
        grid=(b, h, s // kb),
        in_specs=[pl.BlockSpec((1, 1, d, kb), lambda bi, hi, qi: (bi, hi, 0, qi)),
                  pl.BlockSpec((1, 1, s, d), lambda bi, hi, qi: (bi, hi, 0, 0)),
                  pl.BlockSpec((1, 1, dv, s), lambda bi, hi, qi: (bi, hi, 0, 0))],
        out_specs=pl.BlockSpec((1, 1, dv, kb), lambda bi, hi, qi: (bi, hi, 0, qi)),
        out_shape=jax.ShapeDtypeStruct((b, h, dv, s), F32),
        compiler_params=_params(("parallel", "parallel", "arbitrary")),
        name="causal_attention",
    )(qt, k, vt)


def _cross_kernel(qt_ref, k_ref, vt_ref, ot_ref, *, scale):
    s = _dot(k_ref[0, 0], qt_ref[0, 0]) * scale
    p = jnp.exp(s - jnp.max(s, axis=0, keepdims=True))
    l = jnp.sum(p, axis=0, keepdims=True)
    ot_ref[0, 0] = _dot(vt_ref[0, 0], p.astype(BF16)) / l


def cross_attention(qt, k, vt, qb):
    b, h, d, s = qt.shape
    dv, n = vt.shape[2], vt.shape[3]
    return pl.pallas_call(
        functools.partial(_cross_kernel, scale=d ** -0.5),
        grid=(b, h, s // qb),
        in_specs=[pl.BlockSpec((1, 1, d, qb), lambda bi, hi, qi: (bi, hi, 0, qi)),
                  pl.BlockSpec((1, 1, n, d), lambda bi, hi, qi: (bi, hi, 0, 0)),
                  pl.BlockSpec((1, 1, dv, n), lambda bi, hi, qi: (bi, hi, 0, 0))],
        out_specs=pl.BlockSpec((1, 1, dv, qb), lambda bi, hi, qi: (bi, hi, 0, qi)),
        out_shape=jax.ShapeDtypeStruct((b, h, dv, s), F32),
        compiler_params=_params(("parallel", "parallel", "parallel")),
        name="cross_attention",
    )(qt, k, vt)


def _nsa_kernel(qt_ref, g_ref, slope_ref, kc_ref, vct_ref, ovt_ref, ks_ref, vst_ref, kw_ref, vwt_ref,
                ot_ref, sel_ref, *, scale, topn, n_cmp):
    qi = pl.program_id(1)
    qt = qt_ref[0, 0]
    slope = slope_ref[...]
    lanes = qt.shape[1]
    qn = NSA_Q
    kb = KEY_TILE
    dv = vst_ref.shape[1]
    q0 = qi * qn
    lane = lax.broadcasted_iota(jnp.int32, (1, lanes), 1)
    qpos = q0 + jnp.bitwise_and(lane, qn - 1)

    ncp = kc_ref.shape[1]
    cid = lax.broadcasted_iota(jnp.int32, (ncp, 1), 0)
    cmp_end = cid * NSA_CMP_STRIDE + (NSA_CMP_LEN - 1)
    valid = jnp.logical_and(cmp_end <= qpos, cid < n_cmp)
    s = _dot(kc_ref[0], qt) * scale - slope * (qpos - cmp_end).astype(F32)
    s = jnp.where(valid, s, NEG)
    e = jnp.where(valid, jnp.exp(s - jnp.max(s, axis=0, keepdims=True)), 0.0)
    den = jnp.sum(e, axis=0, keepdims=True)
    p_c = e / jnp.where(den > 0.0, den, 1.0)
    o_c = _dot(vct_ref[0], p_c.astype(BF16))

    p_sum = p_c[:, 0:qn]
    for hh in range(1, lanes // qn):
        p_sum = p_sum + p_c[:, hh * qn:(hh + 1) * qn]
    imp = _dot_exact(ovt_ref[...], p_sum)
    nsel = imp.shape[0]
    sid = lax.broadcasted_iota(jnp.int32, (nsel, qn), 0)
    cur = jnp.right_shift(qpos[:, 0:qn], 6)
    forced = jnp.logical_or(sid == 0, sid == cur)
    score = jnp.where(forced, BIG, jnp.where(sid < cur, imp, NEG))
    sel = _select_top(score, sid.astype(F32), topn, 0.5 * NEG)
    sel_ref[...] = jnp.concatenate([sel] * (lanes // qn), axis=1)

    kid = lax.broadcasted_iota(jnp.int32, (kb, 1), 0)
    jd = lax.div(q0, kb)
    per_tile = kb // NSA_SEL_LEN

    def scores(k_ref, j):
        k0 = pl.multiple_of(j * kb, kb)
        kpos = k0 + kid
        dist = qpos - kpos
        return _dot(k_ref[0, pl.ds(k0, kb), :], qt) * scale - slope * dist.astype(F32), dist, k0

    def sel_tile(j, c, diag):
        s, dist, k0 = scores(ks_ref, j)
        rows = [jnp.broadcast_to(sel_ref[pl.ds(j * per_tile + r, 1), :], (NSA_SEL_LEN, lanes))
                for r in range(per_tile)]
        keep = jnp.concatenate(rows, axis=0) > 0.5
        if diag:
            keep = jnp.logical_and(keep, dist >= 0)
        return _softmax_tile(jnp.where(keep, s, NEG), vst_ref[0, :, pl.ds(k0, kb)], *c)

    c = sel_tile(jd, _softmax_init(dv, lanes), True)
    m, l, acc = lax.fori_loop(0, jd, lambda j, c: sel_tile(j, c, False), c)
    o_s = acc / l

    c = _softmax_init(dv, lanes)
    for back in range(NSA_WINDOW // kb + 1):
        j = jnp.maximum(jd - back, 0)
        s, dist, k0 = scores(kw_ref, j)
        width = jnp.where(jd - back >= 0, NSA_WINDOW, 0)
        keep = jnp.logical_and(dist >= 0, dist < width)
        c = _softmax_tile(jnp.where(keep, s, NEG), vwt_ref[0, :, pl.ds(k0, kb)], *c)
    o_w = c[2] / c[1]

    g = _sigmoid(g_ref[0, 0])
    ot_ref[0, 0] = g[0:1] * o_c + g[1:2] * o_s + g[2:3] * o_w


def nsa_attention(qt, gates, k_c, vc_t, k_s, vs_t, k_w, vw_t, n_cmp):
    b, nq, d, lanes = qt.shape
    s_len = k_s.shape[1]
    ncp = k_c.shape[1]
    nsel = s_len // NSA_SEL_LEN
    cs = np.arange(ncp) * NSA_CMP_STRIDE
    ss = np.arange(nsel) * NSA_SEL_LEN
    ov = ((cs[:, None] < ss[None, :] + NSA_SEL_LEN) & (cs[:, None] + NSA_CMP_LEN > ss[None, :])
          & (np.arange(ncp)[:, None] < n_cmp)).astype(np.float32)
    slopes = jnp.asarray(np.repeat(_alibi_slopes(lanes // NSA_Q), NSA_Q)[None, :])
    blk = lambda bi, qi: (bi, qi, 0, 0)
    per_b = lambda bi, qi: (bi, 0, 0)
    return pl.pallas_call(
        functools.partial(_nsa_kernel, scale=d ** -0.5, topn=min(NSA_TOPN, nsel), n_cmp=n_cmp),
        grid=(b, nq),
        in_specs=[pl.BlockSpec((1, 1, d, lanes), blk),
                  pl.BlockSpec((1, 1, 3, lanes), blk),
                  pl.BlockSpec((1, lanes), lambda bi, qi: (0, 0)),
                  pl.BlockSpec((1, ncp, d), per_b),
                  pl.BlockSpec((1, d, ncp), per_b),
                  pl.BlockSpec((nsel, ncp), lambda bi, qi: (0, 0)),
                  pl.BlockSpec((1, s_len, d), per_b),
                  pl.BlockSpec((1, d, s_len), per_b),
                  pl.BlockSpec((1, s_len, d), per_b),
                  pl.BlockSpec((1, d, s_len), per_b)],
        out_specs=pl.BlockSpec((1, 1, d, lanes), blk),
        out_shape=jax.ShapeDtypeStruct((b, nq, d, lanes), F32),
        scratch_shapes=[pltpu.VMEM((nsel, lanes), F32)],
        compiler_params=_params(("parallel", "arbitrary")),
        name="nsa_attention",
    )(qt, gates, slopes, k_c, vc_t, jnp.asarray(ov.T), k_s, vs_t, k_w, vw_t)


def _layer_norm(r, g, b):
    mu = jnp.mean(r, axis=-1, keepdims=True)
    c = r - mu
    var = jnp.mean(c * c, axis=-1, keepdims=True)
    return c * lax.rsqrt(var + LN_EPS) * g + b


def _merge_kernel(h_ref, o_ref, wg_ref, bg_ref, wbr_ref, wout_ref, g_ref, b_ref, out_ref):
    h = h_ref[...]
    hb = h.astype(BF16)
    merged = jnp.zeros(h.shape, F32)
    for i in range(N_BRANCHES):
        gate = _sigmoid(_dot(hb, wg_ref[i]) + bg_ref[i])
        merged = merged + gate * _dot(o_ref[i].astype(BF16), wbr_ref[i])
    y = _dot(merged.astype(BF16), wout_ref[...])
    out_ref[...] = _layer_norm(DEEPNORM_ALPHA * h + y, g_ref[...], b_ref[...])


def gated_merge(h, branches, w_gate, b_gate, w_br, w_out, ln_g, ln_b, bm):
    t, d = h.shape
    nb, _, bw = branches.shape
    row = lambda i: (i, 0)
    c2 = lambda i: (0, 0)
    c3 = lambda i: (0, 0, 0)
    return pl.pallas_call(
        _merge_kernel,
        grid=(t // bm,),
        in_specs=[pl.BlockSpec((bm, d), row),
                  pl.BlockSpec((nb, bm, bw), lambda i: (0, i, 0)),
                  pl.BlockSpec((nb, d, d), c3, pipeline_mode=pl.Buffered(1)),
                  pl.BlockSpec((nb, 1, d), c3),
                  pl.BlockSpec((nb, bw, d), c3, pipeline_mode=pl.Buffered(1)),
                  pl.BlockSpec((d, d), c2, pipeline_mode=pl.Buffered(1)),
                  pl.BlockSpec((1, d), c2),
                  pl.BlockSpec((1, d), c2)],
        out_specs=pl.BlockSpec((bm, d), row),
        out_shape=jax.ShapeDtypeStruct((t, d), F32),
        compiler_params=_params(("parallel",)),
        name="gated_merge",
    )(h, branches, w_gate.astype(BF16), b_gate.reshape(nb, 1, d), w_br.astype(BF16), w_out.astype(BF16),
      ln_g.reshape(1, d), ln_b.reshape(1, d))


ROUTER_LANES = 128


def _moe_kernel(h_ref, wr_ref, br_ref, wup_ref, wdn_ref, g_ref, b_ref, out_ref, hid_ref):
    h = h_ref[...]
    hb = h.astype(BF16)
    bm = h.shape[0]
    logits = _dot_exact(h, wr_ref[...]) + br_ref[...]
    lane = lax.broadcasted_iota(jnp.int32, (bm, ROUTER_LANES), 1)
    lane_f = lane.astype(F32)
    is_g = lane < N_GROUPS
    glog = jnp.where(is_g, logits, NEG)
    gmax = jnp.max(glog, axis=-1, keepdims=True)
    g_sel = jnp.min(jnp.where(glog == gmax, lane_f, 1e9), axis=-1, keepdims=True)
    pg_sel = 1.0 / jnp.sum(jnp.where(is_g, jnp.exp(glog - gmax), 0.0), axis=-1, keepdims=True)
    lo = N_GROUPS + g_sel * EXPERTS_PER_GROUP
    in_grp = jnp.logical_and(lane_f >= lo, lane_f < lo + EXPERTS_PER_GROUP)
    elog = jnp.where(in_grp, logits, NEG)
    emax = jnp.max(elog, axis=-1, keepdims=True)
    ee = jnp.where(in_grp, jnp.exp(elog - emax), 0.0)
    pe = ee / jnp.sum(ee, axis=-1, keepdims=True)
    pe_m = jnp.where(in_grp, pe, -1.0)
    v1 = jnp.max(pe_m, axis=-1, keepdims=True)
    i1 = jnp.min(jnp.where(pe_m == v1, lane_f, 1e9), axis=-1, keepdims=True)
    pe_m2 = jnp.where(lane_f == i1, -1.0, pe_m)
    v2 = jnp.max(pe_m2, axis=-1, keepdims=True)
    i2 = jnp.min(jnp.where(pe_m2 == v2, lane_f, 1e9), axis=-1, keepdims=True)
    norm = pg_sel / (v1 + v2)
    gate = jnp.where(lane_f == i1, v1 * norm, jnp.where(lane_f == i2, v2 * norm, 0.0))
    for e in range(N_EXPERTS):
        au = _dot(hb, wup_ref[e])
        a, u = au[:, :D_EXPERT], au[:, D_EXPERT:]
        w_e = jnp.sum(jnp.where(lane == N_GROUPS + e, gate, 0.0), axis=-1, keepdims=True)
        hid_ref[:, e * D_EXPERT:(e + 1) * D_EXPERT] = (w_e * (a * _sigmoid(a) * u)).astype(BF16)
    y = _dot(hid_ref[...], wdn_ref[...])
    out_ref[...] = _layer_norm(DEEPNORM_ALPHA * h + y, g_ref[...], b_ref[...])


def hierarchical_moe(h, w_rg, b_rg, w_re, b_re, w_up, w_down, ln_g, ln_b, bm):
    t, d = h.shape
    ne = N_EXPERTS
    w_r = jnp.concatenate([w_rg, w_re.transpose(1, 0, 2).reshape(d, ne)], axis=1)
    w_r = jnp.pad(w_r, ((0, 0), (0, ROUTER_LANES - w_r.shape[1])))
    b_r = jnp.pad(jnp.concatenate([b_rg, b_re.reshape(ne)]), (0, ROUTER_LANES - N_GROUPS - ne)).reshape(1, -1)
    row = lambda i: (i, 0)
    c2 = lambda i: (0, 0)
    return pl.pallas_call(
        _moe_kernel,
        grid=(t // bm,),
        in_specs=[pl.BlockSpec((bm, d), row),
                  pl.BlockSpec((d, ROUTER_LANES), c2),
                  pl.BlockSpec((1, ROUTER_LANES), c2),
                  pl.BlockSpec((ne, d, 2 * D_EXPERT), lambda i: (0, 0, 0), pipeline_mode=pl.Buffered(1)),
                  pl.BlockSpec((ne * D_EXPERT, d), c2, pipeline_mode=pl.Buffered(1)),
                  pl.BlockSpec((1, d), c2),
                  pl.BlockSpec((1, d), c2)],
        out_specs=pl.BlockSpec((bm, d), row),
        out_shape=jax.ShapeDtypeStruct((t, d), F32),
        scratch_shapes=[pltpu.VMEM((bm, ne * D_EXPERT), BF16)],
        compiler_params=_params(("parallel",)),
        name="hierarchical_moe",
    )(h, w_r, b_r, w_up.astype(BF16), w_down.reshape(ne * D_EXPERT, d).astype(BF16),
      ln_g.reshape(1, d), ln_b.reshape(1, d))


def _pad_in_weight(w_in):
    d = w_in.shape[0]
    sizes = (768, 768, MLA_Q_RANK, MLA_KV_RANK, MLA_ROPE, 256, 384, 12, 256)
    offs = np.concatenate([[0], np.cumsum(sizes)])
    part = [w_in[:, offs[i]:offs[i + 1]] for i in range(len(sizes))]
    z = lambda n: jnp.zeros((d, n), w_in.dtype)
    return jnp.concatenate([part[0], part[1], part[2], part[3], part[4], _rotate_half_cols(part[4]), z(64),
                            part[5], part[6], part[7], z(116), part[8]], axis=1)


def _heads_t(x, b, s, h):
    return x.reshape(b, s, h, -1).transpose(0, 2, 3, 1)


def _heads(x, b, s, h):
    return x.reshape(b, s, h, -1).transpose(0, 2, 1, 3)


def _merge_t(o):
    b, h, d, s = o.shape
    return o.transpose(0, 3, 1, 2).reshape(b * s, h * d)


def _mixer_branches(hin, mem, b, s, g_cq, g_ckv, w_uq, w_ukv, nsa_pe, w_phi_k1, w_phi_k2, w_phi_v1, w_phi_v2,
                    w_mem_kv):
    t = b * s
    hd = HEAD_DIM
    nh = N_HEADS
    sb = hin[:, COL_SB:COL_SB + 768]
    o_sb = sb_attention(_heads_t(sb[:, :256], b, s, nh).astype(BF16), _heads(sb[:, 256:512], b, s, nh).astype(BF16),
                        _heads_t(sb[:, 512:], b, s, nh).astype(BF16))
    mb = hin[:, COL_MOBA:COL_MOBA + 768]
    k_mean = block_mean(mb[:, 256:512].reshape(b, s, nh * hd), MOBA_BLOCK)
    k_mean = k_mean.reshape(b, -1, nh, hd).transpose(0, 2, 1, 3)
    o_moba = moba_attention(_heads_t(mb[:, :256], b, s, nh).astype(BF16), _heads(mb[:, 256:512], b, s, nh).astype(BF16),
                            _heads_t(mb[:, 512:], b, s, nh).astype(BF16), k_mean)
    q, kv, k_r = mla_project(hin, g_cq, g_ckv, w_uq, w_ukv, s, bm=512)
    kv = kv.reshape(b, s, nh, MLA_NOPE + MLA_V)
    k_r = jnp.broadcast_to(k_r.reshape(b, s, 1, MLA_ROPE), (b, s, nh, MLA_ROPE))
    k = jnp.concatenate([kv[..., :MLA_NOPE], k_r], axis=-1).transpose(0, 2, 1, 3)
    o_mla = causal_attention(_heads_t(q, b, s, nh), k, kv[..., MLA_NOPE:].transpose(0, 2, 3, 1))
    nq = s // NSA_Q
    qn = hin[:, COL_NSA_Q:COL_NSA_Q + 256].astype(BF16)
    qt = qn.reshape(b, nq, NSA_Q, nh, hd).transpose(0, 1, 4, 3, 2).reshape(b, nq, hd, nh * NSA_Q)
    gl = hin[:, COL_NSA_G:COL_NSA_G + 3 * nh]
    gl = gl.reshape(b, nq, NSA_Q, nh, 3).transpose(0, 1, 4, 3, 2).reshape(b, nq, 3, nh * NSA_Q)
    kvn = hin[:, COL_NSA_KV:COL_NSA_KV + 6 * hd].reshape(b, s, 6, hd)
    n_chunks = s // NSA_CMP_STRIDE
    ratio = NSA_CMP_LEN // NSA_CMP_STRIDE
    n_cmp = n_chunks - ratio + 1
    ncp = -(-n_cmp // KEY_TILE) * KEY_TILE

    def compress(x, w1, w2):
        chunks = x.reshape(b, n_chunks, NSA_CMP_STRIDE, hd)
        blocks = jnp.concatenate([chunks[:, r:n_chunks - ratio + 1 + r] for r in range(ratio)], axis=2)
        flat = (blocks + nsa_pe).reshape(b, n_cmp, NSA_CMP_LEN * hd)
        flat = jnp.pad(flat, ((0, 0), (0, ncp - n_cmp), (0, 0))).reshape(b * ncp, -1)
        hid = matmul(flat, w1.astype(BF16), bm=KEY_TILE, gelu=True)
        return matmul(hid, w2.astype(BF16), bm=KEY_TILE, out_dtype=BF16).reshape(b, ncp, hd)

    k_c = compress(kvn[:, :, 0], w_phi_k1, w_phi_k2)
    v_c = compress(kvn[:, :, 1], w_phi_v1, w_phi_v2)
    tr = lambda x: x.astype(BF16).transpose(0, 2, 1)
    o_nsa = nsa_attention(qt, gl, k_c, v_c.transpose(0, 2, 1), kvn[:, :, 2].astype(BF16), tr(kvn[:, :, 3]),
                          kvn[:, :, 4].astype(BF16), tr(kvn[:, :, 5]), n_cmp)
    o_nsa = o_nsa.reshape(b, nq, hd, nh, NSA_Q).transpose(0, 1, 4, 3, 2).reshape(t, nh * hd)
    n_mem = mem.shape[1]
    mkv = matmul(mem.reshape(b * n_mem, -1), w_mem_kv.astype(BF16), bm=n_mem, out_dtype=BF16)
    o_mem = cross_attention(_heads_t(hin[:, COL_MEM_Q:COL_MEM_Q + 256], b, s, nh).astype(BF16),
                            _heads(mkv[:, :256], b, n_mem, nh), _heads_t(mkv[:, 256:], b, n_mem, nh), qb=512)
    return jnp.stack([_merge_t(o_sb), _merge_t(o_moba), _merge_t(o_mla), o_nsa, _merge_t(o_mem)])


def kernel(x, mem, w_in, g_cq, g_ckv, w_uq, w_ukv, nsa_pe, w_phi_k1, w_phi_k2, w_phi_v1, w_phi_v2, w_mem_kv, w_br,
           w_gate, b_gate, w_out, ln1_g, ln1_b, w_rg, b_rg, w_re, b_re, w_up, w_down, ln2_g, ln2_b):
    b, s_len, d = x.shape
    s = -(-s_len // MOBA_BLOCK) * MOBA_BLOCK
    h = jnp.pad(x, ((0, 0), (0, s - s_len), (0, 0))).reshape(b * s, d)
    for l in range(w_in.shape[0]):
        hin = matmul(h, _pad_in_weight(w_in[l]).astype(BF16), bm=512)
        branches = _mixer_branches(hin, mem, b, s, g_cq[l], g_ckv[l], w_uq[l], w_ukv[l], nsa_pe[l], w_phi_k1[l],
                                   w_phi_k2[l], w_phi_v1[l], w_phi_v2[l], w_mem_kv[l])
        h = gated_merge(h, branches, w_gate[l], b_gate[l], w_br[l], w_out[l], ln1_g[l], ln1_b[l], bm=256)
        h = hierarchical_moe(h, w_rg[l], b_rg[l], w_re[l], b_re[l], w_up[l], w_down[l], ln2_g[l], ln2_b[l], bm=256)
    return h.reshape(b, s, d)[:, :s_len]
```

```python
import functools

import jax
import jax.numpy as jnp
import numpy as np
from jax import lax
from jax.experimental import pallas as pl
from jax.experimental.pallas import tpu as pltpu

DEPTH = 4
HEAD_DIM = 64
N_HEADS = 4
BRANCH_W = N_HEADS * HEAD_DIM
N_BRANCHES = 5
MOBA_BLOCK = 256
MOBA_TOPK = 3
MLA_Q_RANK = 256
MLA_KV_RANK = 128
MLA_NOPE = 64
MLA_ROPE = 32
MLA_V = 64
ROPE_THETA = 10000.0
NSA_CMP_LEN = 32
NSA_CMP_STRIDE = 16
NSA_SEL_LEN = 64
NSA_TOPN = 16
NSA_WINDOW = 512
NSA_PHI_HIDDEN = 128
N_GROUPS = 4
EXPERTS_PER_GROUP = 4
N_EXPERTS = N_GROUPS * EXPERTS_PER_GROUP
D_EXPERT = 256
DEEPNORM_ALPHA = (2.0 * DEPTH) ** 0.25
LN_EPS = 1e-5
RMS_EPS = 1e-6
NEG = -1e30
BIG = 1e30

LANES = 128
KEY_TILE = 256
NSA_Q = 128
VMEM_LIMIT_BYTES = 56 * 1024 * 1024

F32 = jnp.float32
BF16 = jnp.bfloat16
HIGHEST = lax.Precision.HIGHEST

COL_SB = 0
COL_MOBA = 768
COL_MLA = 1536
COL_NSA_CMP = 2048
COL_NSA_G = 2176
COL_NSA_SLC = 2304
COL_NSA_WIN = 2432
COL_NSA_Q = 2560
COL_MEM_Q = 2816
IN_PAD = 3072
COL_F32 = COL_MLA
F32_W = COL_NSA_SLC - COL_MLA


def _params(semantics):
    return pltpu.CompilerParams(dimension_semantics=semantics, vmem_limit_bytes=VMEM_LIMIT_BYTES)


def _dot(a, b):
    return jnp.dot(a, b, preferred_element_type=F32)


def _dot_exact(a, b):
    return jnp.dot(a, b, preferred_element_type=F32, precision=HIGHEST)


def _sigmoid(x):
    return 1.0 / (1.0 + jnp.exp(-x))


def _resident(shape):
    zeros = (0,) * len(shape)
    return pl.BlockSpec(shape, lambda *_: zeros, pipeline_mode=pl.Buffered(1))


def _in_proj_kernel(x_ref, w_ref, o_ref, f_ref):
    y = _dot(x_ref[...].astype(BF16), w_ref[...])
    o_ref[...] = y.astype(BF16)
    f_ref[...] = y[:, COL_F32:COL_F32 + F32_W]


def in_projection(h, w, bm):
    t, d = h.shape
    n = w.shape[1]
    return pl.pallas_call(
        _in_proj_kernel,
        grid=(t // bm,),
        in_specs=[pl.BlockSpec((bm, d), lambda i: (i, 0)), _resident((d, n))],
        out_specs=[pl.BlockSpec((bm, n), lambda i: (i, 0)), pl.BlockSpec((bm, F32_W), lambda i: (i, 0))],
        out_shape=[jax.ShapeDtypeStruct((t, n), BF16), jax.ShapeDtypeStruct((t, F32_W), F32)],
        compiler_params=_params(("parallel",)),
        name="in_projection",
    )(h, w)


def _mm_kernel(x_ref, w_ref, o_ref):
    o_ref[...] = _dot(x_ref[...].astype(BF16), w_ref[...]).astype(o_ref.dtype)


def matmul(x, w, *, bm, out_dtype=F32):
    m, k = x.shape
    n = w.shape[1]
    return pl.pallas_call(
        _mm_kernel,
        grid=(m // bm,),
        in_specs=[pl.BlockSpec((bm, k), lambda i: (i, 0)), _resident((k, n))],
        out_specs=pl.BlockSpec((bm, n), lambda i: (i, 0)),
        out_shape=jax.ShapeDtypeStruct((m, n), out_dtype),
        compiler_params=_params(("parallel",)),
        name="matmul",
    )(x, w)


def _softmax_tile(s, v_t, m, l, acc):
    m_new = jnp.maximum(m, jnp.max(s, axis=0, keepdims=True))
    alpha = jnp.exp(m - m_new)
    p = jnp.exp(s - m_new)
    l = alpha * l + jnp.sum(p, axis=0, keepdims=True)
    acc = alpha * acc + _dot(v_t, p.astype(BF16))
    return m_new, l, acc


def _softmax_init(dv, q):
    return (jnp.full((1, q), NEG, F32), jnp.zeros((1, q), F32), jnp.zeros((dv, q), F32))


def _causal_softmax(qi, qb, dv, heads, scores, values, finish, finish_diag):
    def tile(j, state, s_fin):
        return tuple(_softmax_tile(s_fin[h], values(j, h), *state[h]) for h in range(heads))

    state = tuple(_softmax_init(dv, qb) for _ in range(heads))
    state = lax.fori_loop(0, qi, lambda j, st: tile(j, st, finish(j, scores(j))), state)
    state = tile(qi, state, finish_diag(scores(qi)))
    return [acc / l for _, l, acc in state]


def _transposed(x):
    return x.astype(F32).T


def _pair_rows(qt_all, h):
    g = qt_all[(h // 2) * LANES:(h // 2 + 1) * LANES]
    row = lax.broadcasted_iota(jnp.int32, g.shape, 0)
    keep = (row >= HEAD_DIM) if h % 2 else (row < HEAD_DIM)
    return jnp.where(keep, g, 0.0).astype(BF16)


def _pair_cols(h):
    return slice((h // 2) * LANES, (h // 2 + 1) * LANES)


def _head_rows(h):
    return slice(h * HEAD_DIM, (h + 1) * HEAD_DIM)


def _fill_transposed(src_ref, dst_ref, kb, extra=None):
    def chunk(c, _):
        r0 = pl.multiple_of(c * kb, kb)
        x = src_ref[0, pl.ds(r0, kb), :].astype(F32)
        dst_ref[:, pl.ds(r0, kb)] = x.T.astype(BF16)
        if extra is not None:
            extra(c, x)
        return 0
    lax.fori_loop(0, src_ref.shape[1] // kb, chunk, 0)


def _store_heads(o_ref, outs):
    o_ref[0] = jnp.concatenate(outs, axis=0).T.astype(o_ref.dtype)


def _qkv_specs(s, kb, col):
    c = col // BRANCH_W
    return [pl.BlockSpec((1, kb, BRANCH_W), lambda bi, qi: (bi, qi, c)),
            pl.BlockSpec((1, s, BRANCH_W), lambda bi, qi: (bi, 0, c + 1)),
            pl.BlockSpec((1, s, BRANCH_W), lambda bi, qi: (bi, 0, c + 2))]


def _sb_kernel(q_ref, k_ref, v_ref, tri_ref, o_ref, vt_ref, *, scale):
    qi = pl.program_id(1)
    kb = q_ref.shape[1]

    @pl.when(qi == 0)
    def _():
        _fill_transposed(v_ref, vt_ref, kb)

    qt_all = _transposed(q_ref[0])
    qts = [_pair_rows(qt_all, h) for h in range(N_HEADS)]
    tri = tri_ref[...]
    rk = lax.broadcasted_iota(jnp.int32, (kb, kb), 0)
    rq = lax.broadcasted_iota(jnp.int32, (kb, kb), 1)
    past = rk < rq

    def all_heads(j, state, diag):
        k0 = pl.multiple_of(j * kb, kb)
        raw = [_dot(k_ref[0, pl.ds(k0, kb), _pair_cols(h)], qts[h]) for h in range(N_HEADS)]
        log_beta, later, col_sum = [], [], []
        for h in range(N_HEADS):
            z = raw[h] * scale
            nz = -z
            soft = jnp.log(1.0 + jnp.exp(jnp.minimum(z, nz)))
            log_keep = jnp.minimum(nz, 0.0) - soft
            log_beta.append(log_keep + z)
            if diag:
                log_keep = jnp.where(past, log_keep, 0.0)
            hi = log_keep.astype(BF16)
            lo = (log_keep - hi.astype(F32)).astype(BF16)
            later.append(_dot(tri, jnp.concatenate([hi, lo], axis=0)))
            col_sum.append(jnp.sum(log_keep, axis=0, keepdims=True))
        out = []
        for h in range(N_HEADS):
            carry, acc = state[h]
            a = jnp.exp(log_beta[h] + later[h] + carry)
            if diag:
                a = jnp.where(past, a, 0.0)
            out.append((carry + col_sum[h], acc + _dot(vt_ref[_head_rows(h), pl.ds(k0, kb)], a.astype(BF16))))
        return tuple(out)

    state = tuple((jnp.zeros((1, kb), F32), jnp.zeros((HEAD_DIM, kb), F32)) for _ in range(N_HEADS))
    state = all_heads(qi, state, True)
    state = lax.fori_loop(0, qi, lambda t, st: all_heads(qi - 1 - t, st, False), state)
    _store_heads(o_ref, [st[1] for st in state])


def sb_attention(a):
    b, s, _ = a.shape
    kb = KEY_TILE
    upper = np.triu(np.ones((kb, kb), np.float32), 1)
    tri = jnp.asarray(np.concatenate([upper, upper], axis=1), BF16)
    return pl.pallas_call(
        functools.partial(_sb_kernel, scale=HEAD_DIM ** -0.5),
        grid=(b, s // kb),
        in_specs=_qkv_specs(s, kb, COL_SB) + [_resident((kb, 2 * kb))],
        out_specs=pl.BlockSpec((1, kb, BRANCH_W), lambda bi, qi: (bi, qi, 0)),
        out_shape=jax.ShapeDtypeStruct((b, s, BRANCH_W), BF16),
        scratch_shapes=[pltpu.VMEM((BRANCH_W, s), BF16)],
        compiler_params=_params(("parallel", "arbitrary")),
        name="sb_attention",
    )(a, a, a, tri)


def _alibi_slopes(n):
    return np.power(2.0, -8.0 * np.arange(1, n + 1, dtype=np.float64) / n).astype(np.float32)


def _select_top(score, ids, count, floor):
    def step(_, c):
        score, sel = c
        mx = jnp.max(score, axis=0, keepdims=True)
        idx = jnp.min(jnp.where(score == mx, ids, 1e9), axis=0, keepdims=True)
        pick = ids == idx
        sel = jnp.where(jnp.logical_and(pick, mx > floor), 1.0, sel)
        score = jnp.where(pick, -jnp.inf, score)
        return score, sel
    return lax.fori_loop(0, count, step, (score, jnp.zeros_like(score)))[1]


def _moba_kernel(slope_ref, q_ref, k_ref, v_ref, o_ref, vt_ref, km_ref, sel_ref, *, scale, topk):
    qi = pl.program_id(1)
    kb = q_ref.shape[1]
    nblk = km_ref.shape[0]

    @pl.when(qi == 0)
    def _():
        def key_mean(c, _):
            km_ref[pl.ds(c, 1), :] = jnp.mean(k_ref[0, pl.ds(pl.multiple_of(c * kb, kb), kb), :].astype(F32),
                                              axis=0, keepdims=True)
            return 0
        _fill_transposed(v_ref, vt_ref, kb)
        lax.fori_loop(0, nblk, key_mean, 0)

    qt_all = _transposed(q_ref[0])
    qts = [_pair_rows(qt_all, h) for h in range(N_HEADS)]
    ids = lax.broadcasted_iota(jnp.int32, (nblk, kb), 0)
    row = lax.broadcasted_iota(jnp.int32, qt_all.shape, 0)
    km = km_ref[...]
    for h in range(N_HEADS):
        in_head = jnp.logical_and(row >= h * HEAD_DIM, row < (h + 1) * HEAD_DIM)
        gscore = _dot_exact(km, jnp.where(in_head, qt_all, 0.0))
        gscore = jnp.where(ids < qi, gscore, NEG)
        sel_ref[h] = _select_top(gscore, ids.astype(F32), topk, 0.5 * NEG)

    rk = lax.broadcasted_iota(jnp.int32, (kb, kb), 0)
    rq = lax.broadcasted_iota(jnp.int32, (kb, kb), 1)
    rel = (rk - rq).astype(F32)

    def scores(j):
        k0 = pl.multiple_of(j * kb, kb)
        return tuple(_dot(k_ref[0, pl.ds(k0, kb), _pair_cols(h)], qts[h]) for h in range(N_HEADS))

    def values(j, h):
        return vt_ref[_head_rows(h), pl.ds(pl.multiple_of(j * kb, kb), kb)]

    def finish(j, raw):
        shift = ((qi - j) * kb).astype(F32)
        return tuple(jnp.where(sel_ref[h, pl.ds(j, 1), :] > 0.5, raw[h] * scale + slope_ref[h] * (rel - shift), NEG)
                     for h in range(N_HEADS))

    def finish_diag(raw):
        return tuple(jnp.where(rk <= rq, raw[h] * scale + slope_ref[h] * rel, NEG) for h in range(N_HEADS))

    _store_heads(o_ref, _causal_softmax(qi, kb, HEAD_DIM, N_HEADS, scores, values, finish, finish_diag))


def moba_attention(a):
    b, s, _ = a.shape
    kb = MOBA_BLOCK
    nblk = s // kb
    topk = min(MOBA_TOPK, nblk - 1)
    slopes = jnp.asarray(_alibi_slopes(N_HEADS))
    c = COL_MOBA // BRANCH_W
    return pl.pallas_call(
        functools.partial(_moba_kernel, scale=HEAD_DIM ** -0.5, topk=topk),
        grid_spec=pltpu.PrefetchScalarGridSpec(
            num_scalar_prefetch=1,
            grid=(b, nblk),
            in_specs=[pl.BlockSpec((1, kb, BRANCH_W), lambda bi, qi, sl: (bi, qi, c)),
                      pl.BlockSpec((1, s, BRANCH_W), lambda bi, qi, sl: (bi, 0, c + 1)),
                      pl.BlockSpec((1, s, BRANCH_W), lambda bi, qi, sl: (bi, 0, c + 2))],
            out_specs=pl.BlockSpec((1, kb, BRANCH_W), lambda bi, qi, sl: (bi, qi, 0)),
            scratch_shapes=[pltpu.VMEM((BRANCH_W, s), BF16),
                            pltpu.VMEM((nblk, BRANCH_W), F32),
                            pltpu.VMEM((N_HEADS, nblk, kb), F32)]),
        out_shape=jax.ShapeDtypeStruct((b, s, BRANCH_W), BF16),
        compiler_params=_params(("parallel", "arbitrary")),
        name="moba_attention",
    )(slopes, a, a, a)


MLA_SLOT = 128


def _rms(x, g):
    return x * lax.rsqrt(jnp.mean(x * x, axis=-1, keepdims=True) + RMS_EPS) * g


def _mla_proj_kernel(x_ref, gq_ref, gkv_ref, wq_ref, wqr_ref, wkv_ref, cq_ref, sq_ref, ck_ref, sk_ref,
                     q_ref, kv_ref, kr_ref):
    x = x_ref[...]
    c_q = _rms(x[:, :MLA_Q_RANK], gq_ref[...]).astype(BF16)
    c_kv = _rms(x[:, MLA_Q_RANK:MLA_Q_RANK + MLA_KV_RANK], gkv_ref[...]).astype(BF16)
    q_ref[...] = (_dot(c_q, wq_ref[...]) * cq_ref[...] + _dot(c_q, wqr_ref[...]) * sq_ref[...]).astype(q_ref.dtype)
    kv_ref[...] = _dot(c_kv, wkv_ref[...]).astype(kv_ref.dtype)
    tail = x[:, MLA_Q_RANK + MLA_KV_RANK:]
    rot = pltpu.roll(tail, LANES - MLA_ROPE, axis=1)
    kr_ref[...] = (tail * ck_ref[...] + rot * sk_ref[...]).astype(kr_ref.dtype)


def _rope_tables(s_len):
    half = MLA_ROPE // 2
    freqs = jnp.power(ROPE_THETA, -jnp.arange(half, dtype=F32) / half)
    ang = jnp.arange(s_len).astype(F32)[:, None] * freqs
    cos = jnp.concatenate([jnp.cos(ang)] * 2, axis=-1)
    sin = jnp.concatenate([jnp.sin(ang)] * 2, axis=-1)
    zk = jnp.zeros((s_len, LANES - MLA_ROPE), F32)
    zq = jnp.zeros((s_len, MLA_SLOT - MLA_NOPE - MLA_ROPE), F32)
    cq = jnp.tile(jnp.concatenate([jnp.ones((s_len, MLA_NOPE), F32), cos, zq], axis=-1), (1, N_HEADS))
    sq = jnp.tile(jnp.concatenate([jnp.zeros((s_len, MLA_NOPE), F32), sin, zq], axis=-1), (1, N_HEADS))
    return cq, sq, jnp.concatenate([cos, zk], axis=-1), jnp.concatenate([sin, zk], axis=-1)


def _rotate_half_cols(w):
    half = w.shape[-1] // 2
    return jnp.concatenate([-w[..., half:], w[..., :half]], axis=-1)


def mla_project(f, g_cq, g_ckv, w_uq, w_ukv, s_len, bm):
    t = f.shape[0]
    wq = w_uq.reshape(MLA_Q_RANK, N_HEADS, MLA_NOPE + MLA_ROPE)
    pad = jnp.zeros((MLA_Q_RANK, N_HEADS, MLA_SLOT - MLA_NOPE - MLA_ROPE), wq.dtype)
    wq_s = jnp.concatenate([wq, pad], axis=-1).reshape(MLA_Q_RANK, -1)
    wqr_s = jnp.concatenate([jnp.zeros_like(wq[..., :MLA_NOPE]), _rotate_half_cols(wq[..., MLA_NOPE:]), pad],
                            axis=-1).reshape(MLA_Q_RANK, -1)
    cq, sq, ck, sk = _rope_tables(s_len)
    nrow = s_len // bm
    row = lambda i: (i, 0)
    pos = lambda i: (i % nrow, 0)
    qw = N_HEADS * MLA_SLOT
    kvw = N_HEADS * (MLA_NOPE + MLA_V)
    return pl.pallas_call(
        _mla_proj_kernel,
        grid=(t // bm,),
        in_specs=[pl.BlockSpec((bm, 512), row),
                  _resident((1, MLA_Q_RANK)), _resident((1, MLA_KV_RANK)),
                  _resident((MLA_Q_RANK, qw)), _resident((MLA_Q_RANK, qw)), _resident((MLA_KV_RANK, kvw)),
                  pl.BlockSpec((bm, qw), pos), pl.BlockSpec((bm, qw), pos),
                  pl.BlockSpec((bm, LANES), pos), pl.BlockSpec((bm, LANES), pos)],
        out_specs=[pl.BlockSpec((bm, qw), row), pl.BlockSpec((bm, kvw), row), pl.BlockSpec((bm, LANES), row)],
        out_shape=[jax.ShapeDtypeStruct((t, qw), BF16), jax.ShapeDtypeStruct((t, kvw), BF16),
                   jax.ShapeDtypeStruct((t, LANES), BF16)],
        compiler_params=_params(("parallel",)),
        name="mla_project",
    )(f, g_cq.reshape(1, -1), g_ckv.reshape(1, -1), wq_s.astype(BF16), wqr_s.astype(BF16), w_ukv.astype(BF16),
      cq, sq, ck, sk)


def _mla_kernel(q_ref, kv_ref, kr_ref, o_ref, vt_ref, *, scale):
    qi = pl.program_id(1)
    kb = q_ref.shape[1]

    @pl.when(qi == 0)
    def _():
        _fill_transposed(kv_ref, vt_ref, kb)

    qt_all = _transposed(q_ref[0])
    qts = []
    for h in range(N_HEADS):
        g = qt_all[h * MLA_SLOT:(h + 1) * MLA_SLOT]
        row = lax.broadcasted_iota(jnp.int32, g.shape, 0)
        qts.append(jnp.concatenate([jnp.where(row < MLA_NOPE, g, 0.0), g[MLA_NOPE:], jnp.zeros_like(g[MLA_NOPE:])],
                                   axis=0).astype(BF16))
    rk = lax.broadcasted_iota(jnp.int32, (kb, kb), 0)
    rq = lax.broadcasted_iota(jnp.int32, (kb, kb), 1)

    def scores(j):
        k0 = pl.multiple_of(j * kb, kb)
        kr = kr_ref[0, pl.ds(k0, kb), :]
        return tuple(_dot(jnp.concatenate([kv_ref[0, pl.ds(k0, kb), h * LANES:(h + 1) * LANES], kr], axis=1), qts[h])
                     for h in range(N_HEADS))

    def values(j, h):
        return vt_ref[h * LANES + MLA_NOPE:(h + 1) * LANES, pl.ds(pl.multiple_of(j * kb, kb), kb)]

    outs = _causal_softmax(qi, kb, MLA_V, N_HEADS, scores, values,
                           lambda j, raw: tuple(s * scale for s in raw),
                           lambda raw: tuple(jnp.where(rk <= rq, s * scale, NEG) for s in raw))
    _store_heads(o_ref, outs)


def mla_attention(q, kv, kr):
    b, s, qw = q.shape
    kb = KEY_TILE
    return pl.pallas_call(
        functools.partial(_mla_kernel, scale=(MLA_NOPE + MLA_ROPE) ** -0.5),
        grid=(b, s // kb),
        in_specs=[pl.BlockSpec((1, kb, qw), lambda bi, qi: (bi, qi, 0)),
                  pl.BlockSpec((1, s, kv.shape[2]), lambda bi, qi: (bi, 0, 0)),
                  pl.BlockSpec((1, s, LANES), lambda bi, qi: (bi, 0, 0))],
        out_specs=pl.BlockSpec((1, kb, BRANCH_W), lambda bi, qi: (bi, qi, 0)),
        out_shape=jax.ShapeDtypeStruct((b, s, BRANCH_W), BF16),
        scratch_shapes=[pltpu.VMEM((kv.shape[2], s), BF16)],
        compiler_params=_params(("parallel", "arbitrary")),
        name="mla_attention",
    )(q, kv, kr)


def _cross_kernel(q_ref, kv_ref, o_ref, *, scale):
    qt_all = _transposed(q_ref[0])
    kv = kv_ref[0]
    vt = _transposed(kv[:, BRANCH_W:]).astype(BF16)
    outs = []
    for h in range(N_HEADS):
        s = _dot(kv[:, _pair_cols(h)], _pair_rows(qt_all, h)) * scale
        p = jnp.exp(s - jnp.max(s, axis=0, keepdims=True))
        outs.append(_dot(vt[_head_rows(h)], p.astype(BF16)) / jnp.sum(p, axis=0, keepdims=True))
    _store_heads(o_ref, outs)


def cross_attention(a, mkv, qb):
    b, s, _ = a.shape
    n = mkv.shape[1]
    c = COL_MEM_Q // BRANCH_W
    return pl.pallas_call(
        functools.partial(_cross_kernel, scale=HEAD_DIM ** -0.5),
        grid=(b, s // qb),
        in_specs=[pl.BlockSpec((1, qb, BRANCH_W), lambda bi, qi: (bi, qi, c)),
                  pl.BlockSpec((1, n, 2 * BRANCH_W), lambda bi, qi: (bi, 0, 0))],
        out_specs=pl.BlockSpec((1, qb, BRANCH_W), lambda bi, qi: (bi, qi, 0)),
        out_shape=jax.ShapeDtypeStruct((b, s, BRANCH_W), BF16),
        compiler_params=_params(("parallel", "parallel")),
        name="cross_attention",
    )(a, mkv)


def _compress_kernel(x_ref, pe_ref, w1_ref, w2_ref, kc_ref, vct_ref):
    n = kc_ref.shape[1]
    first = jnp.zeros((n, 2 * NSA_PHI_HIDDEN), F32)
    second = jnp.zeros((n, 2 * NSA_PHI_HIDDEN), F32)
    for r in range(NSA_CMP_STRIDE):
        x = x_ref[0, pl.ds(r, n, stride=NSA_CMP_STRIDE), :]
        first = first + _dot((x + pe_ref[r:r + 1]).astype(BF16), w1_ref[r])
        second = second + _dot((x + pe_ref[NSA_CMP_STRIDE + r:NSA_CMP_STRIDE + r + 1]).astype(BF16),
                               w1_ref[NSA_CMP_STRIDE + r])
    hidden = jax.nn.gelu(first + pltpu.roll(second, n - 1, axis=0))
    out = _dot(hidden.astype(BF16), w2_ref[...])
    kc_ref[0] = out.astype(BF16)
    vct_ref[0] = out.T[HEAD_DIM:].astype(BF16)


def _pair_diag(wk, wv):
    z = jnp.zeros_like(wk)
    return jnp.concatenate([jnp.concatenate([wk, z], axis=-1), jnp.concatenate([z, wv], axis=-1)], axis=-2)


def nsa_compress(f, nsa_pe, w_k1, w_k2, w_v1, w_v2):
    b, s, _ = f.shape
    n = s // NSA_CMP_STRIDE
    hd = HEAD_DIM
    w1 = _pair_diag(w_k1.reshape(NSA_CMP_LEN, hd, -1), w_v1.reshape(NSA_CMP_LEN, hd, -1)).astype(BF16)
    w2 = _pair_diag(w_k2, w_v2).astype(BF16)
    pe = jnp.concatenate([nsa_pe, nsa_pe], axis=-1)
    c = (COL_NSA_CMP - COL_F32) // LANES
    return pl.pallas_call(
        _compress_kernel,
        grid=(b,),
        in_specs=[pl.BlockSpec((1, s, LANES), lambda bi: (bi, 0, c)),
                  _resident(pe.shape), _resident(w1.shape), _resident(w2.shape)],
        out_specs=[pl.BlockSpec((1, n, LANES), lambda bi: (bi, 0, 0)),
                   pl.BlockSpec((1, hd, n), lambda bi: (bi, 0, 0))],
        out_shape=[jax.ShapeDtypeStruct((b, n, LANES), BF16), jax.ShapeDtypeStruct((b, hd, n), BF16)],
        compiler_params=_params(("parallel",)),
        name="nsa_compress",
    )(f, pe, w1, w2)


def _nsa_kernel(q_ref, g_ref, slope_ref, kc_ref, vct_ref, ovt_ref, slc_ref, win_ref, o_ref,
                vst_ref, vwt_ref, sel_ref, *, scale, topn, n_cmp):
    qi = pl.program_id(1)
    qn = q_ref.shape[1]
    kb = KEY_TILE
    lanes = N_HEADS * qn
    dv = HEAD_DIM
    q0 = qi * qn

    @pl.when(qi == 0)
    def _():
        def chunk(c, _):
            r0 = pl.multiple_of(c * kb, kb)
            vst_ref[:, pl.ds(r0, kb)] = slc_ref[0, pl.ds(r0, kb), :].astype(F32).T[dv:].astype(BF16)
            vwt_ref[:, pl.ds(r0, kb)] = win_ref[0, pl.ds(r0, kb), :].astype(F32).T[dv:].astype(BF16)
            return 0
        lax.fori_loop(0, slc_ref.shape[1] // kb, chunk, 0)

    qt_all = _transposed(q_ref[0])
    qt = jnp.concatenate([qt_all[_head_rows(h)] for h in range(N_HEADS)], axis=1)
    qt = jnp.concatenate([qt, jnp.zeros_like(qt)], axis=0).astype(BF16)
    slope = slope_ref[...]
    lane = lax.broadcasted_iota(jnp.int32, (1, lanes), 1)
    qpos = q0 + jnp.bitwise_and(lane, qn - 1)

    ncp = kc_ref.shape[1]
    cid = lax.broadcasted_iota(jnp.int32, (ncp, 1), 0)
    cmp_end = cid * NSA_CMP_STRIDE + (NSA_CMP_LEN - 1)
    valid = jnp.logical_and(cmp_end <= qpos, cid < n_cmp)
    s = _dot(kc_ref[0], qt) * scale - slope * (qpos - cmp_end).astype(F32)
    s = jnp.where(valid, s, NEG)
    e = jnp.where(valid, jnp.exp(s - jnp.max(s, axis=0, keepdims=True)), 0.0)
    den = jnp.sum(e, axis=0, keepdims=True)
    p_c = e / jnp.where(den > 0.0, den, 1.0)
    o_c = _dot(vct_ref[0], p_c.astype(BF16))

    p_sum = p_c[:, 0:qn]
    for hh in range(1, N_HEADS):
        p_sum = p_sum + p_c[:, hh * qn:(hh + 1) * qn]
    imp = _dot_exact(ovt_ref[...], p_sum)
    nsel = imp.shape[0]
    sid = lax.broadcasted_iota(jnp.int32, (nsel, qn), 0)
    cur = jnp.right_shift(qpos[:, 0:qn], NSA_SEL_LEN.bit_length() - 1)
    forced = jnp.logical_or(sid == 0, sid == cur)
    score = jnp.where(forced, BIG, jnp.where(sid < cur, imp, NEG))
    sel = _select_top(score, sid.astype(F32), topn, 0.5 * NEG)
    sel_ref[...] = jnp.concatenate([sel] * N_HEADS, axis=1)

    kid = lax.broadcasted_iota(jnp.int32, (kb, 1), 0)
    jd = lax.div(q0, kb)
    per_tile = kb // NSA_SEL_LEN

    def scores(kv_ref, j):
        k0 = pl.multiple_of(j * kb, kb)
        dist = qpos - (k0 + kid)
        return _dot(kv_ref[0, pl.ds(k0, kb), :], qt) * scale - slope * dist.astype(F32), dist, k0

    def sel_tile(j, c, diag):
        s, dist, k0 = scores(slc_ref, j)
        rows = [jnp.broadcast_to(sel_ref[pl.ds(j * per_tile + r, 1), :], (NSA_SEL_LEN, lanes))
                for r in range(per_tile)]
        keep = jnp.concatenate(rows, axis=0) > 0.5
        if diag:
            keep = jnp.logical_and(keep, dist >= 0)
        return _softmax_tile(jnp.where(keep, s, NEG), vst_ref[:, pl.ds(k0, kb)], *c)

    c = sel_tile(jd, _softmax_init(dv, lanes), True)
    m, l, acc = lax.fori_loop(0, jd, lambda j, c: sel_tile(j, c, False), c)
    o_s = acc / l

    c = _softmax_init(dv, lanes)
    for back in range(NSA_WINDOW // kb + 1):
        s, dist, k0 = scores(win_ref, jnp.maximum(jd - back, 0))
        width = jnp.where(jd - back >= 0, NSA_WINDOW, 0)
        keep = jnp.logical_and(dist >= 0, dist < width)
        c = _softmax_tile(jnp.where(keep, s, NEG), vwt_ref[:, pl.ds(k0, kb)], *c)
    o_w = c[2] / c[1]

    gt = _sigmoid(g_ref[0]).T
    def gate(ci):
        return jnp.concatenate([gt[h * 3 + ci:h * 3 + ci + 1] for h in range(N_HEADS)], axis=1)
    out = gate(0) * o_c + gate(1) * o_s + gate(2) * o_w
    _store_heads(o_ref, [out[:, h * qn:(h + 1) * qn] for h in range(N_HEADS)])


def nsa_attention(a, f, kc, vct):
    b, s, _ = a.shape
    ncp = kc.shape[1]
    n_cmp = ncp - NSA_CMP_LEN // NSA_CMP_STRIDE + 1
    nsel = s // NSA_SEL_LEN
    qn = NSA_Q
    lanes = N_HEADS * qn
    cs = np.arange(ncp) * NSA_CMP_STRIDE
    ss = np.arange(nsel) * NSA_SEL_LEN
    ov = ((cs[:, None] < ss[None, :] + NSA_SEL_LEN) & (cs[:, None] + NSA_CMP_LEN > ss[None, :])
          & (np.arange(ncp)[:, None] < n_cmp)).astype(np.float32)
    slopes = jnp.asarray(np.repeat(_alibi_slopes(N_HEADS), qn)[None, :])
    per_b = lambda bi, qi: (bi, 0, 0)
    return pl.pallas_call(
        functools.partial(_nsa_kernel, scale=HEAD_DIM ** -0.5, topn=min(NSA_TOPN, nsel), n_cmp=n_cmp),
        grid=(b, s // qn),
        in_specs=[pl.BlockSpec((1, qn, BRANCH_W), lambda bi, qi: (bi, qi, COL_NSA_Q // BRANCH_W)),
                  pl.BlockSpec((1, qn, LANES), lambda bi, qi: (bi, qi, (COL_NSA_G - COL_F32) // LANES)),
                  _resident((1, lanes)),
                  pl.BlockSpec((1, ncp, LANES), per_b),
                  pl.BlockSpec((1, HEAD_DIM, ncp), per_b),
                  _resident((nsel, ncp)),
                  pl.BlockSpec((1, s, LANES), lambda bi, qi: (bi, 0, COL_NSA_SLC // LANES)),
                  pl.BlockSpec((1, s, LANES), lambda bi, qi: (bi, 0, COL_NSA_WIN // LANES))],
        out_specs=pl.BlockSpec((1, qn, BRANCH_W), lambda bi, qi: (bi, qi, 0)),
        out_shape=jax.ShapeDtypeStruct((b, s, BRANCH_W), BF16),
        scratch_shapes=[pltpu.VMEM((HEAD_DIM, s), BF16), pltpu.VMEM((HEAD_DIM, s), BF16),
                        pltpu.VMEM((nsel, lanes), F32)],
        compiler_params=_params(("parallel", "arbitrary")),
        name="nsa_attention",
    )(a, f, slopes, kc, vct, jnp.asarray(ov.T), a, a)


def _layer_norm(r, g, b):
    mu = jnp.mean(r, axis=-1, keepdims=True)
    c = r - mu
    var = jnp.mean(c * c, axis=-1, keepdims=True)
    return c * lax.rsqrt(var + LN_EPS) * g + b


def _merge_kernel(h_ref, o0_ref, o1_ref, o2_ref, o3_ref, o4_ref, wg_ref, bg_ref, wbr_ref, wout_ref, g_ref, b_ref,
                  out_ref):
    h = h_ref[...]
    hb = h.astype(BF16)
    merged = jnp.zeros(h.shape, F32)
    for i, o_ref in enumerate((o0_ref, o1_ref, o2_ref, o3_ref, o4_ref)):
        gate = _sigmoid(_dot(hb, wg_ref[i]) + bg_ref[i])
        merged = merged + gate * _dot(o_ref[...], wbr_ref[i])
    y = _dot(merged.astype(BF16), wout_ref[...])
    out_ref[...] = _layer_norm(DEEPNORM_ALPHA * h + y, g_ref[...], b_ref[...])


def gated_merge(h, branches, w_gate, b_gate, w_br, w_out, ln_g, ln_b, bm):
    t, d = h.shape
    nb, bw = len(branches), branches[0].shape[1]
    row = lambda i: (i, 0)
    return pl.pallas_call(
        _merge_kernel,
        grid=(t // bm,),
        in_specs=[pl.BlockSpec((bm, d), row)] + [pl.BlockSpec((bm, bw), row)] * nb
                 + [_resident((nb, d, d)), _resident((nb, 1, d)), _resident((nb, bw, d)), _resident((d, d)),
                    _resident((1, d)), _resident((1, d))],
        out_specs=pl.BlockSpec((bm, d), row),
        out_shape=jax.ShapeDtypeStruct((t, d), F32),
        compiler_params=_params(("parallel",)),
        name="gated_merge",
    )(h, *branches, w_gate.astype(BF16), b_gate.reshape(nb, 1, d), w_br.astype(BF16), w_out.astype(BF16),
      ln_g.reshape(1, d), ln_b.reshape(1, d))


ROUTER_LANES = 128


def _moe_kernel(h_ref, wr_ref, br_ref, wup_ref, wdn_ref, g_ref, b_ref, out_ref, hid_ref):
    h = h_ref[...]
    hb = h.astype(BF16)
    bm = h.shape[0]
    logits = _dot_exact(h, wr_ref[...]) + br_ref[...]
    lane = lax.broadcasted_iota(jnp.int32, (bm, ROUTER_LANES), 1)
    lane_f = lane.astype(F32)
    is_g = lane < N_GROUPS
    glog = jnp.where(is_g, logits, NEG)
    gmax = jnp.max(glog, axis=-1, keepdims=True)
    g_sel = jnp.min(jnp.where(glog == gmax, lane_f, 1e9), axis=-1, keepdims=True)
    pg_sel = 1.0 / jnp.sum(jnp.where(is_g, jnp.exp(glog - gmax), 0.0), axis=-1, keepdims=True)
    lo = N_GROUPS + g_sel * EXPERTS_PER_GROUP
    in_grp = jnp.logical_and(lane_f >= lo, lane_f < lo + EXPERTS_PER_GROUP)
    elog = jnp.where(in_grp, logits, NEG)
    emax = jnp.max(elog, axis=-1, keepdims=True)
    ee = jnp.where(in_grp, jnp.exp(elog - emax), 0.0)
    pe = ee / jnp.sum(ee, axis=-1, keepdims=True)
    pe_m = jnp.where(in_grp, pe, -1.0)
    v1 = jnp.max(pe_m, axis=-1, keepdims=True)
    i1 = jnp.min(jnp.where(pe_m == v1, lane_f, 1e9), axis=-1, keepdims=True)
    pe_m2 = jnp.where(lane_f == i1, -1.0, pe_m)
    v2 = jnp.max(pe_m2, axis=-1, keepdims=True)
    i2 = jnp.min(jnp.where(pe_m2 == v2, lane_f, 1e9), axis=-1, keepdims=True)
    norm = pg_sel / (v1 + v2)
    gate = jnp.where(lane_f == i1, v1 * norm, jnp.where(lane_f == i2, v2 * norm, 0.0))
    for e in range(N_EXPERTS):
        au = _dot(hb, wup_ref[e])
        a, u = au[:, :D_EXPERT], au[:, D_EXPERT:]
        w_e = jnp.sum(jnp.where(lane == N_GROUPS + e, gate, 0.0), axis=-1, keepdims=True)
        hid_ref[:, e * D_EXPERT:(e + 1) * D_EXPERT] = (w_e * (a * _sigmoid(a) * u)).astype(BF16)
    y = _dot(hid_ref[...], wdn_ref[...])
    out_ref[...] = _layer_norm(DEEPNORM_ALPHA * h + y, g_ref[...], b_ref[...])


def hierarchical_moe(h, w_rg, b_rg, w_re, b_re, w_up, w_down, ln_g, ln_b, bm):
    t, d = h.shape
    ne = N_EXPERTS
    w_r = jnp.concatenate([w_rg, w_re.transpose(1, 0, 2).reshape(d, ne)], axis=1)
    w_r = jnp.pad(w_r, ((0, 0), (0, ROUTER_LANES - w_r.shape[1])))
    b_r = jnp.pad(jnp.concatenate([b_rg, b_re.reshape(ne)]), (0, ROUTER_LANES - N_GROUPS - ne)).reshape(1, -1)
    row = lambda i: (i, 0)
    return pl.pallas_call(
        _moe_kernel,
        grid=(t // bm,),
        in_specs=[pl.BlockSpec((bm, d), row), _resident((d, ROUTER_LANES)), _resident((1, ROUTER_LANES)),
                  _resident((ne, d, 2 * D_EXPERT)), _resident((ne * D_EXPERT, d)), _resident((1, d)),
                  _resident((1, d))],
        out_specs=pl.BlockSpec((bm, d), row),
        out_shape=jax.ShapeDtypeStruct((t, d), F32),
        scratch_shapes=[pltpu.VMEM((bm, ne * D_EXPERT), BF16)],
        compiler_params=_params(("parallel",)),
        name="hierarchical_moe",
    )(h, w_r, b_r, w_up.astype(BF16), w_down.reshape(ne * D_EXPERT, d).astype(BF16),
      ln_g.reshape(1, d), ln_b.reshape(1, d))


def _pad_in_weight(w_in):
    d = w_in.shape[0]
    sizes = (768, 768, MLA_Q_RANK, MLA_KV_RANK, MLA_ROPE, 256, 384, 12, 256)
    offs = np.concatenate([[0], np.cumsum(sizes)])
    sb, moba, c_q, c_kv, k_rope, nsa_q, nsa_kv, nsa_g, mem_q = (w_in[:, offs[i]:offs[i + 1]] for i in range(len(sizes)))
    z = lambda n: jnp.zeros((d, n), w_in.dtype)
    return jnp.concatenate([sb, moba, c_q, c_kv, k_rope, _rotate_half_cols(k_rope), z(64),
                            nsa_kv[:, :128], nsa_g, z(116), nsa_kv[:, 128:], nsa_q, mem_q], axis=1)


def kernel(x, mem, w_in, g_cq, g_ckv, w_uq, w_ukv, nsa_pe, w_phi_k1, w_phi_k2, w_phi_v1, w_phi_v2, w_mem_kv, w_br,
           w_gate, b_gate, w_out, ln1_g, ln1_b, w_rg, b_rg, w_re, b_re, w_up, w_down, ln2_g, ln2_b):
    b, s_len, d = x.shape
    s = -(-s_len // MOBA_BLOCK) * MOBA_BLOCK
    t = b * s
    n_mem = mem.shape[1]
    h = jnp.pad(x, ((0, 0), (0, s - s_len), (0, 0))).reshape(t, d)
    for l in range(w_in.shape[0]):
        a, f = in_projection(h, _pad_in_weight(w_in[l]).astype(BF16), bm=512)
        a3, f3 = a.reshape(b, s, IN_PAD), f.reshape(b, s, F32_W)
        q, kv, kr = mla_project(f, g_cq[l], g_ckv[l], w_uq[l], w_ukv[l], s, bm=512)
        kc, vct = nsa_compress(f3, nsa_pe[l], w_phi_k1[l], w_phi_k2[l], w_phi_v1[l], w_phi_v2[l])
        mkv = matmul(mem.reshape(b * n_mem, d), w_mem_kv[l].astype(BF16), bm=n_mem, out_dtype=BF16)
        branches = [sb_attention(a3),
                    moba_attention(a3),
                    mla_attention(q.reshape(b, s, -1), kv.reshape(b, s, -1), kr.reshape(b, s, -1)),
                    nsa_attention(a3, f3, kc, vct),
                    cross_attention(a3, mkv.reshape(b, n_mem, -1), qb=512)]
        h = gated_merge(h, [o.reshape(t, BRANCH_W) for o in branches], w_gate[l], b_gate[l], w_br[l], w_out[l],
                        ln1_g[l], ln1_b[l], bm=256)
        h = hierarchical_moe(h, w_rg[l], b_rg[l], w_re[l], b_re[l], w_up[l], w_down[l], ln2_g[l], ln2_b[l], bm=256)
    return h.reshape(b, s, d)[:, :s_len]
```

```python
import functools

import jax
import jax.numpy as jnp
import numpy as np
from jax import lax
from jax.experimental import pallas as pl
from jax.experimental.pallas import tpu as pltpu

DEPTH = 4
HEAD_DIM = 64
N_HEADS = 4
BRANCH_W = N_HEADS * HEAD_DIM
N_BRANCHES = 5
MOBA_BLOCK = 256
MOBA_TOPK = 3
MLA_Q_RANK = 256
MLA_KV_RANK = 128
MLA_NOPE = 64
MLA_ROPE = 32
MLA_V = 64
ROPE_THETA = 10000.0
NSA_CMP_LEN = 32
NSA_CMP_STRIDE = 16
NSA_SEL_LEN = 64
NSA_TOPN = 16
NSA_WINDOW = 512
NSA_PHI_HIDDEN = 128
N_GROUPS = 4
EXPERTS_PER_GROUP = 4
N_EXPERTS = N_GROUPS * EXPERTS_PER_GROUP
D_EXPERT = 256
DEEPNORM_ALPHA = (2.0 * DEPTH) ** 0.25
LN_EPS = 1e-5
RMS_EPS = 1e-6
NEG = -1e30
BIG = 1e30

LANES = 128
QUERY_BLOCK = 256
KEY_TILE = 256
SOFTMAX_KEY_TILE = 1024
SB_KEY_TILE = 512
LOG2E = 1.4426950408889634
VMEM_LIMIT_BYTES = 56 * 1024 * 1024

F32 = jnp.float32
BF16 = jnp.bfloat16
HIGHEST = lax.Precision.HIGHEST

COL_SB = 0
COL_MOBA = 768
COL_MLA = 1536
COL_NSA_CMP = 2048
COL_NSA_G = 2176
COL_NSA_SLC = 2304
COL_NSA_WIN = 2432
COL_NSA_Q = 2560
COL_MEM_Q = 2816
IN_PAD = 3072
COL_F32 = COL_MLA
F32_W = COL_NSA_SLC - COL_MLA


def _params(semantics):
    return pltpu.CompilerParams(dimension_semantics=semantics, vmem_limit_bytes=VMEM_LIMIT_BYTES)


def _dot(a, b):
    return jnp.dot(a, b, preferred_element_type=F32)


def _dot_exact(a, b):
    return jnp.dot(a, b, preferred_element_type=F32, precision=HIGHEST)


def _sigmoid(x):
    return 1.0 / (1.0 + jnp.exp(-x))


def _resident(shape):
    zeros = (0,) * len(shape)
    return pl.BlockSpec(shape, lambda *_: zeros, pipeline_mode=pl.Buffered(1))


def _in_proj_kernel(x_ref, w_ref, o_ref, f_ref):
    y = _dot(x_ref[...].astype(BF16), w_ref[...])
    o_ref[...] = y.astype(BF16)
    f_ref[...] = y[:, COL_F32:COL_F32 + F32_W]


def in_projection(h, w, bm):
    t, d = h.shape
    n = w.shape[1]
    return pl.pallas_call(
        _in_proj_kernel,
        grid=(t // bm,),
        in_specs=[pl.BlockSpec((bm, d), lambda i: (i, 0)), _resident((d, n))],
        out_specs=[pl.BlockSpec((bm, n), lambda i: (i, 0)), pl.BlockSpec((bm, F32_W), lambda i: (i, 0))],
        out_shape=[jax.ShapeDtypeStruct((t, n), BF16), jax.ShapeDtypeStruct((t, F32_W), F32)],
        compiler_params=_params(("parallel",)),
        name="in_projection",
    )(h, w)


def _mm_kernel(x_ref, w_ref, o_ref):
    o_ref[...] = _dot(x_ref[...].astype(BF16), w_ref[...]).astype(o_ref.dtype)


def matmul(x, w, *, bm, out_dtype=F32):
    m, k = x.shape
    n = w.shape[1]
    return pl.pallas_call(
        _mm_kernel,
        grid=(m // bm,),
        in_specs=[pl.BlockSpec((bm, k), lambda i: (i, 0)), _resident((k, n))],
        out_specs=pl.BlockSpec((bm, n), lambda i: (i, 0)),
        out_shape=jax.ShapeDtypeStruct((m, n), out_dtype),
        compiler_params=_params(("parallel",)),
        name="matmul",
    )(x, w)


def _softmax_tile(s, v_t, m, l, acc, shift=None):
    top = jnp.max(s, axis=0, keepdims=True)
    m_new = jnp.maximum(m, top if shift is None else top - shift)
    alpha = jnp.exp2(m - m_new)
    p = jnp.exp2(s - (m_new if shift is None else m_new + shift))
    l = alpha * l + jnp.sum(p, axis=0, keepdims=True)
    acc = alpha * acc + _dot(v_t, p.astype(BF16))
    return m_new, l, acc


def _softmax_init(dv, q):
    return (jnp.full((1, q), NEG, F32), jnp.zeros((1, q), F32), jnp.zeros((dv, q), F32))


def _causal_softmax(qi, qb, kt, dv, heads, scores, values, finish, finish_last):
    def tile(j, state, fin):
        return tuple(_softmax_tile(fin[h][0], values(j, h), *state[h], shift=fin[h][1]) for h in range(heads))

    last = lax.div(qi * qb, kt)
    state = tuple(_softmax_init(dv, qb) for _ in range(heads))
    state = lax.fori_loop(0, last, lambda j, st: tile(j, st, finish(j, scores(j))), state)
    state = tile(last, state, finish_last(last, scores(last)))
    return [acc / l for _, l, acc in state]


def _transposed(x):
    return x.astype(F32).T


def _pair_rows(qt_all, h):
    g = qt_all[(h // 2) * LANES:(h // 2 + 1) * LANES]
    row = lax.broadcasted_iota(jnp.int32, g.shape, 0)
    keep = (row >= HEAD_DIM) if h % 2 else (row < HEAD_DIM)
    return jnp.where(keep, g, 0.0).astype(BF16)


def _pair_cols(h):
    return slice((h // 2) * LANES, (h // 2 + 1) * LANES)


def _head_rows(h):
    return slice(h * HEAD_DIM, (h + 1) * HEAD_DIM)


def _fill_transposed(src_ref, dst_ref, kb, extra=None):
    def chunk(c, _):
        r0 = pl.multiple_of(c * kb, kb)
        x = src_ref[0, pl.ds(r0, kb), :].astype(F32)
        dst_ref[:, pl.ds(r0, kb)] = x.T.astype(BF16)
        if extra is not None:
            extra(c, x)
        return 0
    lax.fori_loop(0, src_ref.shape[1] // kb, chunk, 0)


def _store_heads(o_ref, outs):
    o_ref[0] = jnp.concatenate(outs, axis=0).T.astype(o_ref.dtype)


def _qkv_specs(s, kb, col):
    c = col // BRANCH_W
    return [pl.BlockSpec((1, kb, BRANCH_W), lambda bi, qi: (bi, qi, c)),
            pl.BlockSpec((1, s, BRANCH_W), lambda bi, qi: (bi, 0, c + 1)),
            pl.BlockSpec((1, s, BRANCH_W), lambda bi, qi: (bi, 0, c + 2))]


def _sb_kernel(q_ref, k_ref, v_ref, tri_ref, o_ref, vt_ref, *, kt):
    qi = pl.program_id(1)
    qb = q_ref.shape[1]
    kb = KEY_TILE
    pieces = kt // kb

    @pl.when(qi == 0)
    def _():
        _fill_transposed(v_ref, vt_ref, kb)

    qt_all = _transposed(q_ref[0])
    qts = [_pair_rows(qt_all, h) for h in range(N_HEADS)]
    tri = tri_ref[...]
    rel = lax.broadcasted_iota(jnp.int32, (kb, qb), 0) - lax.broadcasted_iota(jnp.int32, (kb, qb), 1)

    def tile(j, state, last):
        order = [(u, h) for u in reversed(range(pieces)) for h in range(N_HEADS)]
        k0 = {u: pl.multiple_of(j * kt + u * kb, kb) for u in range(pieces)}
        raw = {(u, h): _dot(k_ref[0, pl.ds(k0[u], kb), _pair_cols(h)], qts[h]) for u, h in order}
        log_beta, later, col_sum, past = {}, {}, {}, {}
        for u, h in order:
            z = raw[u, h]
            nz = -z
            soft = jnp.log(1.0 + jnp.exp2(jnp.minimum(z, nz))) * LOG2E
            log_keep = jnp.minimum(nz, 0.0) - soft
            log_beta[u, h] = log_keep + z
            if last:
                past[u] = rel < qi * qb - k0[u]
                log_keep = jnp.where(past[u], log_keep, 0.0)
            hi = log_keep.astype(BF16)
            lo = (log_keep - hi.astype(F32)).astype(BF16)
            later[u, h] = _dot(tri, jnp.concatenate([hi, lo], axis=0))
            col_sum[u, h] = jnp.sum(log_keep, axis=0, keepdims=True)
        state = list(state)
        for u, h in order:
            carry, acc = state[h]
            a = jnp.exp2(log_beta[u, h] + later[u, h] + carry)
            if last:
                a = jnp.where(past[u], a, 0.0)
            state[h] = (carry + col_sum[u, h], acc + _dot(vt_ref[_head_rows(h), pl.ds(k0[u], kb)], a.astype(BF16)))
        return tuple(state)

    last = lax.div(qi * qb, kt)
    state = tuple((jnp.zeros((1, qb), F32), jnp.zeros((HEAD_DIM, qb), F32)) for _ in range(N_HEADS))
    state = tile(last, state, True)
    state = lax.fori_loop(0, last, lambda t, st: tile(last - 1 - t, st, False), state)
    _store_heads(o_ref, [st[1] for st in state])


def sb_attention(a):
    b, s, _ = a.shape
    kb, qb = KEY_TILE, QUERY_BLOCK
    upper = np.triu(np.ones((kb, kb), np.float32), 1)
    tri = jnp.asarray(np.concatenate([upper, upper], axis=1), BF16)
    return pl.pallas_call(
        functools.partial(_sb_kernel, kt=min(SB_KEY_TILE, s)),
        grid=(b, s // qb),
        in_specs=_qkv_specs(s, qb, COL_SB) + [_resident((kb, 2 * kb))],
        out_specs=pl.BlockSpec((1, qb, BRANCH_W), lambda bi, qi: (bi, qi, 0)),
        out_shape=jax.ShapeDtypeStruct((b, s, BRANCH_W), BF16),
        scratch_shapes=[pltpu.VMEM((BRANCH_W, s), BF16)],
        compiler_params=_params(("parallel", "arbitrary")),
        name="sb_attention",
    )(a, a, a, tri)


def _alibi_slopes(n):
    return np.power(2.0, -8.0 * np.arange(1, n + 1, dtype=np.float64) / n).astype(np.float32)


def _select_top(score, ids, count, floor):
    def step(_, c):
        score, sel = c
        mx = jnp.max(score, axis=0, keepdims=True)
        idx = jnp.min(jnp.where(score == mx, ids, 1e9), axis=0, keepdims=True)
        pick = ids == idx
        sel = jnp.where(jnp.logical_and(pick, mx > floor), 1.0, sel)
        score = jnp.where(pick, -jnp.inf, score)
        return score, sel
    return lax.fori_loop(0, count, step, (score, jnp.zeros_like(score)))[1]


def _moba_kernel(slope_ref, q_ref, k_ref, v_ref, o_ref, vt_ref, km_ref, sel_ref, *, kt, topk):
    qi = pl.program_id(1)
    kb = q_ref.shape[1]
    nblk = km_ref.shape[0]
    per_tile = kt // kb

    @pl.when(qi == 0)
    def _():
        def key_mean(c, _):
            km_ref[pl.ds(c, 1), :] = jnp.mean(k_ref[0, pl.ds(pl.multiple_of(c * kb, kb), kb), :].astype(F32),
                                              axis=0, keepdims=True)
            return 0
        _fill_transposed(v_ref, vt_ref, kb)
        lax.fori_loop(0, nblk, key_mean, 0)

    qt_all = _transposed(q_ref[0])
    qts = [_pair_rows(qt_all, h) for h in range(N_HEADS)]
    ids = lax.broadcasted_iota(jnp.int32, (nblk, kb), 0)
    row = lax.broadcasted_iota(jnp.int32, qt_all.shape, 0)
    km = km_ref[...]
    for h in range(N_HEADS):
        in_head = jnp.logical_and(row >= h * HEAD_DIM, row < (h + 1) * HEAD_DIM)
        gscore = _dot_exact(km, jnp.where(in_head, qt_all, 0.0))
        gscore = jnp.where(ids < qi, gscore, NEG)
        sel_ref[h] = _select_top(gscore, ids.astype(F32), topk, 0.5 * NEG)

    rk = lax.broadcasted_iota(jnp.int32, (kt, kb), 0)
    rel = rk - lax.broadcasted_iota(jnp.int32, (kt, kb), 1)
    bias = [slope_ref[h] * rel.astype(F32) for h in range(N_HEADS)]

    def scores(j):
        k0 = pl.multiple_of(j * kt, kt)
        return tuple(_dot(k_ref[0, pl.ds(k0, kt), _pair_cols(h)], qts[h]) for h in range(N_HEADS))

    def values(j, h):
        return vt_ref[_head_rows(h), pl.ds(pl.multiple_of(j * kt, kt), kt)]

    def selected(j, h):
        rows = [jnp.broadcast_to(sel_ref[h, pl.ds(j * per_tile + r, 1), :], (kb, kb)) for r in range(per_tile)]
        return jnp.concatenate(rows, axis=0) > 0.5

    def finish(j, raw):
        off = (qi * kb - j * kt).astype(F32)
        return tuple((jnp.where(selected(j, h), raw[h] + bias[h], NEG), slope_ref[h] * off) for h in range(N_HEADS))

    def finish_last(j, raw):
        d0 = qi * kb - j * kt
        own_causal = jnp.logical_and(rk >= d0, rel <= d0)
        out = []
        for h in range(N_HEADS):
            s = raw[h] + bias[h]
            out.append((jnp.where(own_causal, s, jnp.where(selected(j, h), s, NEG)), slope_ref[h] * d0.astype(F32)))
        return tuple(out)

    _store_heads(o_ref, _causal_softmax(qi, kb, kt, HEAD_DIM, N_HEADS, scores, values, finish, finish_last))


def moba_attention(a):
    b, s, _ = a.shape
    kb = MOBA_BLOCK
    nblk = s // kb
    topk = min(MOBA_TOPK, nblk - 1)
    slopes = jnp.asarray(_alibi_slopes(N_HEADS) * np.float32(LOG2E))
    c = COL_MOBA // BRANCH_W
    return pl.pallas_call(
        functools.partial(_moba_kernel, kt=min(SOFTMAX_KEY_TILE, s), topk=topk),
        grid_spec=pltpu.PrefetchScalarGridSpec(
            num_scalar_prefetch=1,
            grid=(b, nblk),
            in_specs=[pl.BlockSpec((1, kb, BRANCH_W), lambda bi, qi, sl: (bi, qi, c)),
                      pl.BlockSpec((1, s, BRANCH_W), lambda bi, qi, sl: (bi, 0, c + 1)),
                      pl.BlockSpec((1, s, BRANCH_W), lambda bi, qi, sl: (bi, 0, c + 2))],
            out_specs=pl.BlockSpec((1, kb, BRANCH_W), lambda bi, qi, sl: (bi, qi, 0)),
            scratch_shapes=[pltpu.VMEM((BRANCH_W, s), BF16),
                            pltpu.VMEM((nblk, BRANCH_W), F32),
                            pltpu.VMEM((N_HEADS, nblk, kb), F32)]),
        out_shape=jax.ShapeDtypeStruct((b, s, BRANCH_W), BF16),
        compiler_params=_params(("parallel", "arbitrary")),
        name="moba_attention",
    )(slopes, a, a, a)


MLA_SLOT = 128


def _rms(x, g):
    return x * lax.rsqrt(jnp.mean(x * x, axis=-1, keepdims=True) + RMS_EPS) * g


def _mla_proj_kernel(x_ref, gq_ref, gkv_ref, wq_ref, wqr_ref, wkv_ref, cq_ref, sq_ref, ck_ref, sk_ref,
                     q_ref, kv_ref, kr_ref):
    x = x_ref[...]
    c_q = _rms(x[:, :MLA_Q_RANK], gq_ref[...]).astype(BF16)
    c_kv = _rms(x[:, MLA_Q_RANK:MLA_Q_RANK + MLA_KV_RANK], gkv_ref[...]).astype(BF16)
    q_ref[...] = (_dot(c_q, wq_ref[...]) * cq_ref[...] + _dot(c_q, wqr_ref[...]) * sq_ref[...]).astype(q_ref.dtype)
    kv_ref[...] = _dot(c_kv, wkv_ref[...]).astype(kv_ref.dtype)
    tail = x[:, MLA_Q_RANK + MLA_KV_RANK:]
    rot = pltpu.roll(tail, LANES - MLA_ROPE, axis=1)
    kr_ref[...] = (tail * ck_ref[...] + rot * sk_ref[...]).astype(kr_ref.dtype)


def _rope_tables(s_len):
    half = MLA_ROPE // 2
    freqs = jnp.power(ROPE_THETA, -jnp.arange(half, dtype=F32) / half)
    ang = jnp.arange(s_len).astype(F32)[:, None] * freqs
    cos = jnp.concatenate([jnp.cos(ang)] * 2, axis=-1)
    sin = jnp.concatenate([jnp.sin(ang)] * 2, axis=-1)
    zk = jnp.zeros((s_len, LANES - MLA_ROPE), F32)
    zq = jnp.zeros((s_len, MLA_SLOT - MLA_NOPE - MLA_ROPE), F32)
    cq = jnp.tile(jnp.concatenate([jnp.ones((s_len, MLA_NOPE), F32), cos, zq], axis=-1), (1, N_HEADS))
    sq = jnp.tile(jnp.concatenate([jnp.zeros((s_len, MLA_NOPE), F32), sin, zq], axis=-1), (1, N_HEADS))
    return cq, sq, jnp.concatenate([cos, zk], axis=-1), jnp.concatenate([sin, zk], axis=-1)


def _rotate_half_cols(w):
    half = w.shape[-1] // 2
    return jnp.concatenate([-w[..., half:], w[..., :half]], axis=-1)


def mla_project(f, g_cq, g_ckv, w_uq, w_ukv, s_len, bm):
    t = f.shape[0]
    wq = w_uq.reshape(MLA_Q_RANK, N_HEADS, MLA_NOPE + MLA_ROPE)
    pad = jnp.zeros((MLA_Q_RANK, N_HEADS, MLA_SLOT - MLA_NOPE - MLA_ROPE), wq.dtype)
    wq_s = jnp.concatenate([wq, pad], axis=-1).reshape(MLA_Q_RANK, -1)
    wqr_s = jnp.concatenate([jnp.zeros_like(wq[..., :MLA_NOPE]), _rotate_half_cols(wq[..., MLA_NOPE:]), pad],
                            axis=-1).reshape(MLA_Q_RANK, -1)
    cq, sq, ck, sk = _rope_tables(s_len)
    q_scale = (MLA_NOPE + MLA_ROPE) ** -0.5 * LOG2E
    nrow = s_len // bm
    row = lambda i: (i, 0)
    pos = lambda i: (i % nrow, 0)
    qw = N_HEADS * MLA_SLOT
    kvw = N_HEADS * (MLA_NOPE + MLA_V)
    return pl.pallas_call(
        _mla_proj_kernel,
        grid=(t // bm,),
        in_specs=[pl.BlockSpec((bm, 512), row),
                  _resident((1, MLA_Q_RANK)), _resident((1, MLA_KV_RANK)),
                  _resident((MLA_Q_RANK, qw)), _resident((MLA_Q_RANK, qw)), _resident((MLA_KV_RANK, kvw)),
                  pl.BlockSpec((bm, qw), pos), pl.BlockSpec((bm, qw), pos),
                  pl.BlockSpec((bm, LANES), pos), pl.BlockSpec((bm, LANES), pos)],
        out_specs=[pl.BlockSpec((bm, qw), row), pl.BlockSpec((bm, kvw), row), pl.BlockSpec((bm, LANES), row)],
        out_shape=[jax.ShapeDtypeStruct((t, qw), BF16), jax.ShapeDtypeStruct((t, kvw), BF16),
                   jax.ShapeDtypeStruct((t, LANES), BF16)],
        compiler_params=_params(("parallel",)),
        name="mla_project",
    )(f, g_cq.reshape(1, -1), g_ckv.reshape(1, -1), (wq_s * q_scale).astype(BF16), (wqr_s * q_scale).astype(BF16),
      w_ukv.astype(BF16), cq, sq, ck, sk)


def _mla_kernel(q_ref, kv_ref, kr_ref, o_ref, vt_ref, *, kt):
    qi = pl.program_id(1)
    qb = q_ref.shape[1]

    @pl.when(qi == 0)
    def _():
        _fill_transposed(kv_ref, vt_ref, KEY_TILE)

    qt_all = _transposed(q_ref[0])
    qts = []
    for h in range(N_HEADS):
        g = qt_all[h * MLA_SLOT:(h + 1) * MLA_SLOT]
        row = lax.broadcasted_iota(jnp.int32, g.shape, 0)
        qts.append(jnp.concatenate([jnp.where(row < MLA_NOPE, g, 0.0), g[MLA_NOPE:], jnp.zeros_like(g[MLA_NOPE:])],
                                   axis=0).astype(BF16))
    rel = lax.broadcasted_iota(jnp.int32, (kt, qb), 0) - lax.broadcasted_iota(jnp.int32, (kt, qb), 1)

    def scores(j):
        k0 = pl.multiple_of(j * kt, kt)
        kr = kr_ref[0, pl.ds(k0, kt), :]
        return tuple(_dot(jnp.concatenate([kv_ref[0, pl.ds(k0, kt), h * LANES:(h + 1) * LANES], kr], axis=1), qts[h])
                     for h in range(N_HEADS))

    def values(j, h):
        return vt_ref[h * LANES + MLA_NOPE:(h + 1) * LANES, pl.ds(pl.multiple_of(j * kt, kt), kt)]

    def finish_last(j, raw):
        causal = rel <= qi * qb - j * kt
        return tuple((jnp.where(causal, s, NEG), None) for s in raw)

    outs = _causal_softmax(qi, qb, kt, MLA_V, N_HEADS, scores, values,
                           lambda j, raw: tuple((s, None) for s in raw), finish_last)
    _store_heads(o_ref, outs)


def mla_attention(q, kv, kr):
    b, s, qw = q.shape
    kb = QUERY_BLOCK
    return pl.pallas_call(
        functools.partial(_mla_kernel, kt=min(SOFTMAX_KEY_TILE, s)),
        grid=(b, s // kb),
        in_specs=[pl.BlockSpec((1, kb, qw), lambda bi, qi: (bi, qi, 0)),
                  pl.BlockSpec((1, s, kv.shape[2]), lambda bi, qi: (bi, 0, 0)),
                  pl.BlockSpec((1, s, LANES), lambda bi, qi: (bi, 0, 0))],
        out_specs=pl.BlockSpec((1, kb, BRANCH_W), lambda bi, qi: (bi, qi, 0)),
        out_shape=jax.ShapeDtypeStruct((b, s, BRANCH_W), BF16),
        scratch_shapes=[pltpu.VMEM((kv.shape[2], s), BF16)],
        compiler_params=_params(("parallel", "arbitrary")),
        name="mla_attention",
    )(q, kv, kr)


def _cross_kernel(q_ref, kv_ref, o_ref):
    qt_all = _transposed(q_ref[0])
    kv = kv_ref[0]
    vt = _transposed(kv[:, BRANCH_W:]).astype(BF16)
    outs = []
    for h in range(N_HEADS):
        s = _dot(kv[:, _pair_cols(h)], _pair_rows(qt_all, h))
        p = jnp.exp2(s - jnp.max(s, axis=0, keepdims=True))
        outs.append(_dot(vt[_head_rows(h)], p.astype(BF16)) / jnp.sum(p, axis=0, keepdims=True))
    _store_heads(o_ref, outs)


def cross_attention(a, mkv, qb):
    b, s, _ = a.shape
    n = mkv.shape[1]
    c = COL_MEM_Q // BRANCH_W
    return pl.pallas_call(
        _cross_kernel,
        grid=(b, s // qb),
        in_specs=[pl.BlockSpec((1, qb, BRANCH_W), lambda bi, qi: (bi, qi, c)),
                  pl.BlockSpec((1, n, 2 * BRANCH_W), lambda bi, qi: (bi, 0, 0))],
        out_specs=pl.BlockSpec((1, qb, BRANCH_W), lambda bi, qi: (bi, qi, 0)),
        out_shape=jax.ShapeDtypeStruct((b, s, BRANCH_W), BF16),
        compiler_params=_params(("parallel", "parallel")),
        name="cross_attention",
    )(a, mkv)


def _compress_kernel(x_ref, pe_ref, w1_ref, w2_ref, kc_ref, vct_ref):
    n = kc_ref.shape[1]
    first = jnp.zeros((n, 2 * NSA_PHI_HIDDEN), F32)
    second = jnp.zeros((n, 2 * NSA_PHI_HIDDEN), F32)
    for r in range(NSA_CMP_STRIDE):
        x = x_ref[0, pl.ds(r, n, stride=NSA_CMP_STRIDE), :]
        first = first + _dot((x + pe_ref[r:r + 1]).astype(BF16), w1_ref[r])
        second = second + _dot((x + pe_ref[NSA_CMP_STRIDE + r:NSA_CMP_STRIDE + r + 1]).astype(BF16),
                               w1_ref[NSA_CMP_STRIDE + r])
    hidden = jax.nn.gelu(first + pltpu.roll(second, n - 1, axis=0))
    out = _dot(hidden.astype(BF16), w2_ref[...])
    kc_ref[0] = out.astype(BF16)
    vct_ref[0] = out.T[HEAD_DIM:].astype(BF16)


def _pair_diag(wk, wv):
    z = jnp.zeros_like(wk)
    return jnp.concatenate([jnp.concatenate([wk, z], axis=-1), jnp.concatenate([z, wv], axis=-1)], axis=-2)


def nsa_compress(f, nsa_pe, w_k1, w_k2, w_v1, w_v2):
    b, s, _ = f.shape
    n = s // NSA_CMP_STRIDE
    hd = HEAD_DIM
    w1 = _pair_diag(w_k1.reshape(NSA_CMP_LEN, hd, -1), w_v1.reshape(NSA_CMP_LEN, hd, -1)).astype(BF16)
    w2 = _pair_diag(w_k2, w_v2).astype(BF16)
    pe = jnp.concatenate([nsa_pe, nsa_pe], axis=-1)
    c = (COL_NSA_CMP - COL_F32) // LANES
    return pl.pallas_call(
        _compress_kernel,
        grid=(b,),
        in_specs=[pl.BlockSpec((1, s, LANES), lambda bi: (bi, 0, c)),
                  _resident(pe.shape), _resident(w1.shape), _resident(w2.shape)],
        out_specs=[pl.BlockSpec((1, n, LANES), lambda bi: (bi, 0, 0)),
                   pl.BlockSpec((1, hd, n), lambda bi: (bi, 0, 0))],
        out_shape=[jax.ShapeDtypeStruct((b, n, LANES), BF16), jax.ShapeDtypeStruct((b, hd, n), BF16)],
        compiler_params=_params(("parallel",)),
        name="nsa_compress",
    )(f, pe, w1, w2)


def _nsa_kernel(q_ref, g_ref, slope_ref, kc_ref, vct_ref, ovt_ref, slc_ref, win_ref, o_ref,
                vst_ref, vwt_ref, sel_ref, *, topn, n_cmp):
    qi = pl.program_id(1)
    qn = q_ref.shape[1]
    kb = KEY_TILE
    lanes = N_HEADS * qn
    dv = HEAD_DIM
    q0 = qi * qn

    @pl.when(qi == 0)
    def _():
        def chunk(c, _):
            r0 = pl.multiple_of(c * kb, kb)
            vst_ref[:, pl.ds(r0, kb)] = slc_ref[0, pl.ds(r0, kb), :].astype(F32).T[dv:].astype(BF16)
            vwt_ref[:, pl.ds(r0, kb)] = win_ref[0, pl.ds(r0, kb), :].astype(F32).T[dv:].astype(BF16)
            return 0
        lax.fori_loop(0, slc_ref.shape[1] // kb, chunk, 0)

    qt_all = _transposed(q_ref[0])
    qt = jnp.concatenate([qt_all[_head_rows(h)] for h in range(N_HEADS)], axis=1)
    qt = jnp.concatenate([qt, jnp.zeros_like(qt)], axis=0).astype(BF16)
    slope = slope_ref[...]
    lane = lax.broadcasted_iota(jnp.int32, (1, lanes), 1)
    qpos = q0 + jnp.bitwise_and(lane, qn - 1)

    ncp = kc_ref.shape[1]
    cid = lax.broadcasted_iota(jnp.int32, (ncp, 1), 0)
    cmp_end = cid * NSA_CMP_STRIDE + (NSA_CMP_LEN - 1)
    valid = jnp.logical_and(cmp_end <= qpos, cid < n_cmp)
    s = _dot(kc_ref[0], qt) - slope * (qpos - cmp_end).astype(F32)
    s = jnp.where(valid, s, NEG)
    e = jnp.where(valid, jnp.exp2(s - jnp.max(s, axis=0, keepdims=True)), 0.0)
    den = jnp.sum(e, axis=0, keepdims=True)
    p_c = e / jnp.where(den > 0.0, den, 1.0)
    o_c = _dot(vct_ref[0], p_c.astype(BF16))

    p_sum = p_c[:, 0:qn]
    for hh in range(1, N_HEADS):
        p_sum = p_sum + p_c[:, hh * qn:(hh + 1) * qn]
    imp = _dot_exact(ovt_ref[...], p_sum)
    nsel = imp.shape[0]
    sid = lax.broadcasted_iota(jnp.int32, (nsel, qn), 0)
    cur = jnp.right_shift(qpos[:, 0:qn], NSA_SEL_LEN.bit_length() - 1)
    forced = jnp.logical_or(sid == 0, sid == cur)
    score = jnp.where(forced, BIG, jnp.where(sid < cur, imp, NEG))
    sel = _select_top(score, sid.astype(F32), topn, 0.5 * NEG)
    sel_ref[...] = jnp.concatenate([sel] * N_HEADS, axis=1)

    kid = lax.broadcasted_iota(jnp.int32, (kb, 1), 0)
    jd = lax.div(q0, kb)
    per_tile = kb // NSA_SEL_LEN

    def scores(kv_ref, j):
        k0 = pl.multiple_of(j * kb, kb)
        dist = qpos - (k0 + kid)
        return _dot(kv_ref[0, pl.ds(k0, kb), :], qt) - slope * dist.astype(F32), dist, k0

    def sel_tile(j, c, diag):
        s, dist, k0 = scores(slc_ref, j)
        rows = [jnp.broadcast_to(sel_ref[pl.ds(j * per_tile + r, 1), :], (NSA_SEL_LEN, lanes))
                for r in range(per_tile)]
        keep = jnp.concatenate(rows, axis=0) > 0.5
        if diag:
            keep = jnp.logical_and(keep, dist >= 0)
        return _softmax_tile(jnp.where(keep, s, NEG), vst_ref[:, pl.ds(k0, kb)], *c)

    c = sel_tile(jd, _softmax_init(dv, lanes), True)

    def maybe_sel_tile(j, c):
        rows = [sel_ref[pl.ds(j * per_tile + r, 1), :] for r in range(per_tile)]
        wanted = jnp.max(functools.reduce(jnp.maximum, rows)) > 0.5
        return lax.cond(wanted, lambda c: sel_tile(j, c, False), lambda c: c, c)

    m, l, acc = lax.fori_loop(0, jd, maybe_sel_tile, c)
    o_s = acc / l

    c = _softmax_init(dv, lanes)
    for back in range(NSA_WINDOW // kb + 1):
        s, dist, k0 = scores(win_ref, jnp.maximum(jd - back, 0))
        width = jnp.where(jd - back >= 0, NSA_WINDOW, 0)
        keep = jnp.logical_and(dist >= 0, dist < width)
        c = _softmax_tile(jnp.where(keep, s, NEG), vwt_ref[:, pl.ds(k0, kb)], *c)
    o_w = c[2] / c[1]

    gt = _sigmoid(g_ref[0]).T
    def gate(ci):
        return jnp.concatenate([gt[h * 3 + ci:h * 3 + ci + 1] for h in range(N_HEADS)], axis=1)
    out = gate(0) * o_c + gate(1) * o_s + gate(2) * o_w
    _store_heads(o_ref, [out[:, h * qn:(h + 1) * qn] for h in range(N_HEADS)])


def nsa_attention(a, f, kc, vct):
    b, s, _ = a.shape
    ncp = kc.shape[1]
    n_cmp = ncp - NSA_CMP_LEN // NSA_CMP_STRIDE + 1
    nsel = s // NSA_SEL_LEN
    qn = QUERY_BLOCK
    lanes = N_HEADS * qn
    cs = np.arange(ncp) * NSA_CMP_STRIDE
    ss = np.arange(nsel) * NSA_SEL_LEN
    ov = ((cs[:, None] < ss[None, :] + NSA_SEL_LEN) & (cs[:, None] + NSA_CMP_LEN > ss[None, :])
          & (np.arange(ncp)[:, None] < n_cmp)).astype(np.float32)
    slopes = jnp.asarray(np.repeat(_alibi_slopes(N_HEADS) * np.float32(LOG2E), qn)[None, :])
    per_b = lambda bi, qi: (bi, 0, 0)
    return pl.pallas_call(
        functools.partial(_nsa_kernel, topn=min(NSA_TOPN, nsel), n_cmp=n_cmp),
        grid=(b, s // qn),
        in_specs=[pl.BlockSpec((1, qn, BRANCH_W), lambda bi, qi: (bi, qi, COL_NSA_Q // BRANCH_W)),
                  pl.BlockSpec((1, qn, LANES), lambda bi, qi: (bi, qi, (COL_NSA_G - COL_F32) // LANES)),
                  _resident((1, lanes)),
                  pl.BlockSpec((1, ncp, LANES), per_b),
                  pl.BlockSpec((1, HEAD_DIM, ncp), per_b),
                  _resident((nsel, ncp)),
                  pl.BlockSpec((1, s, LANES), lambda bi, qi: (bi, 0, COL_NSA_SLC // LANES)),
                  pl.BlockSpec((1, s, LANES), lambda bi, qi: (bi, 0, COL_NSA_WIN // LANES))],
        out_specs=pl.BlockSpec((1, qn, BRANCH_W), lambda bi, qi: (bi, qi, 0)),
        out_shape=jax.ShapeDtypeStruct((b, s, BRANCH_W), BF16),
        scratch_shapes=[pltpu.VMEM((HEAD_DIM, s), BF16), pltpu.VMEM((HEAD_DIM, s), BF16),
                        pltpu.VMEM((nsel, lanes), F32)],
        compiler_params=_params(("parallel", "arbitrary")),
        name="nsa_attention",
    )(a, f, slopes, kc, vct, jnp.asarray(ov.T), a, a)


def _layer_norm(r, g, b):
    mu = jnp.mean(r, axis=-1, keepdims=True)
    c = r - mu
    var = jnp.mean(c * c, axis=-1, keepdims=True)
    return c * lax.rsqrt(var + LN_EPS) * g + b


def _merge_kernel(h_ref, o0_ref, o1_ref, o2_ref, o3_ref, o4_ref, wg_ref, bg_ref, wbr_ref, wout_ref, g_ref, b_ref,
                  out_ref):
    h = h_ref[...]
    hb = h.astype(BF16)
    merged = jnp.zeros(h.shape, F32)
    for i, o_ref in enumerate((o0_ref, o1_ref, o2_ref, o3_ref, o4_ref)):
        gate = _sigmoid(_dot(hb, wg_ref[i]) + bg_ref[i])
        merged = merged + gate * _dot(o_ref[...], wbr_ref[i])
    y = _dot(merged.astype(BF16), wout_ref[...])
    out_ref[...] = _layer_norm(DEEPNORM_ALPHA * h + y, g_ref[...], b_ref[...])


def gated_merge(h, branches, w_gate, b_gate, w_br, w_out, ln_g, ln_b, bm):
    t, d = h.shape
    nb, bw = len(branches), branches[0].shape[1]
    row = lambda i: (i, 0)
    return pl.pallas_call(
        _merge_kernel,
        grid=(t // bm,),
        in_specs=[pl.BlockSpec((bm, d), row)] + [pl.BlockSpec((bm, bw), row)] * nb
                 + [_resident((nb, d, d)), _resident((nb, 1, d)), _resident((nb, bw, d)), _resident((d, d)),
                    _resident((1, d)), _resident((1, d))],
        out_specs=pl.BlockSpec((bm, d), row),
        out_shape=jax.ShapeDtypeStruct((t, d), F32),
        compiler_params=_params(("parallel",)),
        name="gated_merge",
    )(h, *branches, w_gate.astype(BF16), b_gate.reshape(nb, 1, d), w_br.astype(BF16), w_out.astype(BF16),
      ln_g.reshape(1, d), ln_b.reshape(1, d))


ROUTER_LANES = 128


def _moe_kernel(h_ref, wr_ref, br_ref, wup_ref, wdn_ref, g_ref, b_ref, out_ref, hid_ref):
    h = h_ref[...]
    hb = h.astype(BF16)
    bm = h.shape[0]
    logits = _dot_exact(h, wr_ref[...]) + br_ref[...]
    lane = lax.broadcasted_iota(jnp.int32, (bm, ROUTER_LANES), 1)
    lane_f = lane.astype(F32)
    is_g = lane < N_GROUPS
    glog = jnp.where(is_g, logits, NEG)
    gmax = jnp.max(glog, axis=-1, keepdims=True)
    g_sel = jnp.min(jnp.where(glog == gmax, lane_f, 1e9), axis=-1, keepdims=True)
    pg_sel = 1.0 / jnp.sum(jnp.where(is_g, jnp.exp(glog - gmax), 0.0), axis=-1, keepdims=True)
    lo = N_GROUPS + g_sel * EXPERTS_PER_GROUP
    in_grp = jnp.logical_and(lane_f >= lo, lane_f < lo + EXPERTS_PER_GROUP)
    elog = jnp.where(in_grp, logits, NEG)
    emax = jnp.max(elog, axis=-1, keepdims=True)
    ee = jnp.where(in_grp, jnp.exp(elog - emax), 0.0)
    pe = ee / jnp.sum(ee, axis=-1, keepdims=True)
    pe_m = jnp.where(in_grp, pe, -1.0)
    v1 = jnp.max(pe_m, axis=-1, keepdims=True)
    i1 = jnp.min(jnp.where(pe_m == v1, lane_f, 1e9), axis=-1, keepdims=True)
    pe_m2 = jnp.where(lane_f == i1, -1.0, pe_m)
    v2 = jnp.max(pe_m2, axis=-1, keepdims=True)
    i2 = jnp.min(jnp.where(pe_m2 == v2, lane_f, 1e9), axis=-1, keepdims=True)
    norm = pg_sel / (v1 + v2)
    gate = jnp.where(lane_f == i1, v1 * norm, jnp.where(lane_f == i2, v2 * norm, 0.0))
    for e in range(N_EXPERTS):
        au = _dot(hb, wup_ref[e])
        a, u = au[:, :D_EXPERT], au[:, D_EXPERT:]
        w_e = jnp.sum(jnp.where(lane == N_GROUPS + e, gate, 0.0), axis=-1, keepdims=True)
        hid_ref[:, e * D_EXPERT:(e + 1) * D_EXPERT] = (w_e * (a * _sigmoid(a) * u)).astype(BF16)
    y = _dot(hid_ref[...], wdn_ref[...])
    out_ref[...] = _layer_norm(DEEPNORM_ALPHA * h + y, g_ref[...], b_ref[...])


def hierarchical_moe(h, w_rg, b_rg, w_re, b_re, w_up, w_down, ln_g, ln_b, bm):
    t, d = h.shape
    ne = N_EXPERTS
    w_r = jnp.concatenate([w_rg, w_re.transpose(1, 0, 2).reshape(d, ne)], axis=1)
    w_r = jnp.pad(w_r, ((0, 0), (0, ROUTER_LANES - w_r.shape[1])))
    b_r = jnp.pad(jnp.concatenate([b_rg, b_re.reshape(ne)]), (0, ROUTER_LANES - N_GROUPS - ne)).reshape(1, -1)
    row = lambda i: (i, 0)
    return pl.pallas_call(
        _moe_kernel,
        grid=(t // bm,),
        in_specs=[pl.BlockSpec((bm, d), row), _resident((d, ROUTER_LANES)), _resident((1, ROUTER_LANES)),
                  _resident((ne, d, 2 * D_EXPERT)), _resident((ne * D_EXPERT, d)), _resident((1, d)),
                  _resident((1, d))],
        out_specs=pl.BlockSpec((bm, d), row),
        out_shape=jax.ShapeDtypeStruct((t, d), F32),
        scratch_shapes=[pltpu.VMEM((bm, ne * D_EXPERT), BF16)],
        compiler_params=_params(("parallel",)),
        name="hierarchical_moe",
    )(h, w_r, b_r, w_up.astype(BF16), w_down.reshape(ne * D_EXPERT, d).astype(BF16),
      ln_g.reshape(1, d), ln_b.reshape(1, d))


def _pad_in_weight(w_in):
    d = w_in.shape[0]
    sizes = (768, 768, MLA_Q_RANK, MLA_KV_RANK, MLA_ROPE, 256, 384, 12, 256)
    offs = np.concatenate([[0], np.cumsum(sizes)])
    sb, moba, c_q, c_kv, k_rope, nsa_q, nsa_kv, nsa_g, mem_q = (w_in[:, offs[i]:offs[i + 1]] for i in range(len(sizes)))
    z = lambda n: jnp.zeros((d, n), w_in.dtype)
    q_scale = HEAD_DIM ** -0.5 * LOG2E
    scale_q = lambda qkv: jnp.concatenate([qkv[:, :BRANCH_W] * q_scale, qkv[:, BRANCH_W:]], axis=1)
    return jnp.concatenate([scale_q(sb), scale_q(moba), c_q, c_kv, k_rope, _rotate_half_cols(k_rope), z(64),
                            nsa_kv[:, :128], nsa_g, z(116), nsa_kv[:, 128:], nsa_q * q_scale, mem_q * q_scale], axis=1)


def kernel(x, mem, w_in, g_cq, g_ckv, w_uq, w_ukv, nsa_pe, w_phi_k1, w_phi_k2, w_phi_v1, w_phi_v2, w_mem_kv, w_br,
           w_gate, b_gate, w_out, ln1_g, ln1_b, w_rg, b_rg, w_re, b_re, w_up, w_down, ln2_g, ln2_b):
    b, s_len, d = x.shape
    s = -(-s_len // MOBA_BLOCK) * MOBA_BLOCK
    t = b * s
    n_mem = mem.shape[1]
    h = jnp.pad(x, ((0, 0), (0, s - s_len), (0, 0))).reshape(t, d)
    for l in range(w_in.shape[0]):
        a, f = in_projection(h, _pad_in_weight(w_in[l]).astype(BF16), bm=512)
        a3, f3 = a.reshape(b, s, IN_PAD), f.reshape(b, s, F32_W)
        q, kv, kr = mla_project(f, g_cq[l], g_ckv[l], w_uq[l], w_ukv[l], s, bm=512)
        kc, vct = nsa_compress(f3, nsa_pe[l], w_phi_k1[l], w_phi_k2[l], w_phi_v1[l], w_phi_v2[l])
        mkv = matmul(mem.reshape(b * n_mem, d), w_mem_kv[l].astype(BF16), bm=n_mem, out_dtype=BF16)
        branches = [sb_attention(a3),
                    moba_attention(a3),
                    mla_attention(q.reshape(b, s, -1), kv.reshape(b, s, -1), kr.reshape(b, s, -1)),
                    nsa_attention(a3, f3, kc, vct),
                    cross_attention(a3, mkv.reshape(b, n_mem, -1), qb=512)]
        h = gated_merge(h, [o.reshape(t, BRANCH_W) for o in branches], w_gate[l], b_gate[l], w_br[l], w_out[l],
                        ln1_g[l], ln1_b[l], bm=256)
        h = hierarchical_moe(h, w_rg[l], b_rg[l], w_re[l], b_re[l], w_up[l], w_down[l], ln2_g[l], ln2_b[l], bm=256)
    return h.reshape(b, s, d)[:, :s_len]
```

```python
import functools

import jax
import jax.numpy as jnp
import numpy as np
from jax import lax
from jax.experimental import pallas as pl
from jax.experimental.pallas import tpu as pltpu

DEPTH = 4
HEAD_DIM = 64
N_HEADS = 4
BRANCH_W = N_HEADS * HEAD_DIM
N_BRANCHES = 5
MOBA_BLOCK = 256
MOBA_TOPK = 3
MLA_Q_RANK = 256
MLA_KV_RANK = 128
MLA_NOPE = 64
MLA_ROPE = 32
MLA_V = 64
ROPE_THETA = 10000.0
NSA_CMP_LEN = 32
NSA_CMP_STRIDE = 16
NSA_SEL_LEN = 64
NSA_TOPN = 16
NSA_WINDOW = 512
NSA_PHI_HIDDEN = 128
N_GROUPS = 4
EXPERTS_PER_GROUP = 4
N_EXPERTS = N_GROUPS * EXPERTS_PER_GROUP
D_EXPERT = 256
DEEPNORM_ALPHA = (2.0 * DEPTH) ** 0.25
LN_EPS = 1e-5
RMS_EPS = 1e-6
NEG = -1e30
BIG = 1e30

LANES = 128
BF16_SUBLANES = 16
VALUE_SLOT = HEAD_DIM + BF16_SUBLANES
QUERY_BLOCK = 256
KEY_TILE = 256
SOFTMAX_KEY_TILE = 1024
SB_KEY_TILE = 512
LOG2E = 1.4426950408889634
VMEM_LIMIT_BYTES = 56 * 1024 * 1024

F32 = jnp.float32
BF16 = jnp.bfloat16
HIGHEST = lax.Precision.HIGHEST

COL_SB = 0
COL_MOBA = 768
COL_MLA = 1536
COL_NSA_CMP = 2048
COL_NSA_G = 2176
COL_NSA_SLC = 2304
COL_NSA_WIN = 2432
COL_NSA_Q = 2560
COL_MEM_Q = 2816
IN_PAD = 3072
COL_F32 = COL_MLA
F32_W = COL_NSA_SLC - COL_MLA


def _params(semantics):
    return pltpu.CompilerParams(dimension_semantics=semantics, vmem_limit_bytes=VMEM_LIMIT_BYTES)


def _dot(a, b):
    return jnp.dot(a, b, preferred_element_type=F32)


def _dot_exact(a, b):
    return jnp.dot(a, b, preferred_element_type=F32, precision=HIGHEST)


def _sigmoid(x):
    return 1.0 / (1.0 + jnp.exp(-x))


def _resident(shape):
    zeros = (0,) * len(shape)
    return pl.BlockSpec(shape, lambda *_: zeros, pipeline_mode=pl.Buffered(1))


def _in_proj_kernel(x_ref, w_ref, o_ref, f_ref):
    y = _dot(x_ref[...].astype(BF16), w_ref[...])
    o_ref[...] = y.astype(BF16)
    f_ref[...] = y[:, COL_F32:COL_F32 + F32_W]


def in_projection(h, w, bm):
    t, d = h.shape
    n = w.shape[1]
    return pl.pallas_call(
        _in_proj_kernel,
        grid=(t // bm,),
        in_specs=[pl.BlockSpec((bm, d), lambda i: (i, 0)), _resident((d, n))],
        out_specs=[pl.BlockSpec((bm, n), lambda i: (i, 0)), pl.BlockSpec((bm, F32_W), lambda i: (i, 0))],
        out_shape=[jax.ShapeDtypeStruct((t, n), BF16), jax.ShapeDtypeStruct((t, F32_W), F32)],
        compiler_params=_params(("parallel",)),
        name="in_projection",
    )(h, w)


def _mm_kernel(x_ref, w_ref, o_ref):
    o_ref[...] = _dot(x_ref[...].astype(BF16), w_ref[...]).astype(o_ref.dtype)


def matmul(x, w, *, bm, out_dtype=F32):
    m, k = x.shape
    n = w.shape[1]
    return pl.pallas_call(
        _mm_kernel,
        grid=(m // bm,),
        in_specs=[pl.BlockSpec((bm, k), lambda i: (i, 0)), _resident((k, n))],
        out_specs=pl.BlockSpec((bm, n), lambda i: (i, 0)),
        out_shape=jax.ShapeDtypeStruct((m, n), out_dtype),
        compiler_params=_params(("parallel",)),
        name="matmul",
    )(x, w)


def _softmax_tile(s, v_slot, m, acc, shift=None):
    top = jnp.max(s, axis=0, keepdims=True)
    m_new = jnp.maximum(m, top if shift is None else top - shift)
    p = jnp.exp2(s - (m_new if shift is None else m_new + shift))
    return m_new, jnp.exp2(m - m_new) * acc + _dot(v_slot, p.astype(BF16))


def _softmax_init(q):
    return (jnp.full((1, q), NEG, F32), jnp.zeros((VALUE_SLOT, q), F32))


def _softmax_result(acc):
    return acc[:HEAD_DIM] / acc[HEAD_DIM:HEAD_DIM + 1]


def _causal_softmax(qi, qb, kt, heads, scores, values, finish, finish_last):
    def tile(j, state, fin):
        return tuple(_softmax_tile(fin[h][0], values(j, h), *state[h], shift=fin[h][1]) for h in range(heads))

    last = lax.div(qi * qb, kt)
    state = tuple(_softmax_init(qb) for _ in range(heads))
    state = lax.fori_loop(0, last, lambda j, st: tile(j, st, finish(j, scores(j))), state)
    state = tile(last, state, finish_last(last, scores(last)))
    return [_softmax_result(acc) for _, acc in state]


def _transposed(x):
    return x.astype(F32).T


def _pair_rows(qt_all, h):
    g = qt_all[(h // 2) * LANES:(h // 2 + 1) * LANES]
    row = lax.broadcasted_iota(jnp.int32, g.shape, 0)
    keep = (row >= HEAD_DIM) if h % 2 else (row < HEAD_DIM)
    return jnp.where(keep, g, 0.0).astype(BF16)


def _pair_cols(h):
    return slice((h // 2) * LANES, (h // 2 + 1) * LANES)


def _head_rows(h):
    return slice(h * HEAD_DIM, (h + 1) * HEAD_DIM)


def _fill_transposed(src_ref, dst_ref, kb, extra=None):
    def chunk(c, _):
        r0 = pl.multiple_of(c * kb, kb)
        x = src_ref[0, pl.ds(r0, kb), :].astype(F32)
        dst_ref[:, pl.ds(r0, kb)] = x.T.astype(BF16)
        if extra is not None:
            extra(c, x)
        return 0
    lax.fori_loop(0, src_ref.shape[1] // kb, chunk, 0)


def _fill_value_slots(src_ref, dst_ref, kb, first_rows):
    ones = jnp.ones((BF16_SUBLANES, kb), BF16)

    def chunk(c, _):
        r0 = pl.multiple_of(c * kb, kb)
        xt = src_ref[0, pl.ds(r0, kb), :].astype(F32).T
        for h, r in enumerate(first_rows):
            dst_ref[h * VALUE_SLOT:h * VALUE_SLOT + HEAD_DIM, pl.ds(r0, kb)] = xt[r:r + HEAD_DIM].astype(BF16)
            dst_ref[h * VALUE_SLOT + HEAD_DIM:(h + 1) * VALUE_SLOT, pl.ds(r0, kb)] = ones
        return 0
    lax.fori_loop(0, src_ref.shape[1] // kb, chunk, 0)


def _value_slot(h):
    return slice(h * VALUE_SLOT, (h + 1) * VALUE_SLOT)


def _store_heads(o_ref, outs):
    o_ref[0] = jnp.concatenate(outs, axis=0).T.astype(o_ref.dtype)


def _qkv_specs(s, kb, col):
    c = col // BRANCH_W
    return [pl.BlockSpec((1, kb, BRANCH_W), lambda bi, qi: (bi, qi, c)),
            pl.BlockSpec((1, s, BRANCH_W), lambda bi, qi: (bi, 0, c + 1)),
            pl.BlockSpec((1, s, BRANCH_W), lambda bi, qi: (bi, 0, c + 2))]


def _sb_kernel(q_ref, k_ref, v_ref, tri_ref, o_ref, vt_ref, *, kt):
    qi = pl.program_id(1)
    qb = q_ref.shape[1]
    kb = KEY_TILE
    pieces = kt // kb

    @pl.when(qi == 0)
    def _():
        _fill_transposed(v_ref, vt_ref, kb)

    qt_all = _transposed(q_ref[0])
    qts = [_pair_rows(qt_all, h) for h in range(N_HEADS)]
    tri = tri_ref[...]
    rel = lax.broadcasted_iota(jnp.int32, (kb, qb), 0) - lax.broadcasted_iota(jnp.int32, (kb, qb), 1)

    def tile(j, state, last):
        order = [(u, h) for u in reversed(range(pieces)) for h in range(N_HEADS)]
        k0 = {u: pl.multiple_of(j * kt + u * kb, kb) for u in range(pieces)}
        raw = {(u, h): _dot(k_ref[0, pl.ds(k0[u], kb), _pair_cols(h)], qts[h]) for u, h in order}
        log_beta, later, col_sum, past = {}, {}, {}, {}
        for u, h in order:
            z = raw[u, h]
            nz = -z
            soft = jnp.log(1.0 + jnp.exp2(jnp.minimum(z, nz))) * LOG2E
            log_keep = jnp.minimum(nz, 0.0) - soft
            log_beta[u, h] = log_keep + z
            if last:
                past[u] = rel < qi * qb - k0[u]
                log_keep = jnp.where(past[u], log_keep, 0.0)
            sums = _dot(tri, log_keep.astype(BF16))
            later[u, h] = sums[:kb]
            col_sum[u, h] = sums[kb:kb + 1]
        state = list(state)
        for u, h in order:
            carry, acc = state[h]
            a = jnp.exp2(log_beta[u, h] + later[u, h] + carry)
            if last:
                a = jnp.where(past[u], a, 0.0)
            state[h] = (carry + col_sum[u, h], acc + _dot(vt_ref[_head_rows(h), pl.ds(k0[u], kb)], a.astype(BF16)))
        return tuple(state)

    last = lax.div(qi * qb, kt)
    state = tuple((jnp.zeros((1, qb), F32), jnp.zeros((HEAD_DIM, qb), F32)) for _ in range(N_HEADS))
    state = tile(last, state, True)
    state = lax.fori_loop(0, last, lambda t, st: tile(last - 1 - t, st, False), state)
    _store_heads(o_ref, [st[1] for st in state])


def sb_attention(a):
    b, s, _ = a.shape
    kb, qb = KEY_TILE, QUERY_BLOCK
    tri = jnp.asarray(np.concatenate([np.triu(np.ones((kb, kb), np.float32), 1), np.ones((BF16_SUBLANES, kb), np.float32)]),
                      BF16)
    return pl.pallas_call(
        functools.partial(_sb_kernel, kt=min(SB_KEY_TILE, s)),
        grid=(b, s // qb),
        in_specs=_qkv_specs(s, qb, COL_SB) + [_resident(tri.shape)],
        out_specs=pl.BlockSpec((1, qb, BRANCH_W), lambda bi, qi: (bi, qi, 0)),
        out_shape=jax.ShapeDtypeStruct((b, s, BRANCH_W), BF16),
        scratch_shapes=[pltpu.VMEM((BRANCH_W, s), BF16)],
        compiler_params=_params(("parallel", "arbitrary")),
        name="sb_attention",
    )(a, a, a, tri)


def _alibi_slopes(n):
    return np.power(2.0, -8.0 * np.arange(1, n + 1, dtype=np.float64) / n).astype(np.float32)


def _select_top(score, ids, count, floor):
    def step(_, c):
        score, sel = c
        mx = jnp.max(score, axis=0, keepdims=True)
        idx = jnp.min(jnp.where(score == mx, ids, 1e9), axis=0, keepdims=True)
        pick = ids == idx
        sel = jnp.where(jnp.logical_and(pick, mx > floor), 1.0, sel)
        score = jnp.where(pick, -jnp.inf, score)
        return score, sel
    return lax.fori_loop(0, count, step, (score, jnp.zeros_like(score)))[1]


def _moba_kernel(slope_ref, q_ref, k_ref, v_ref, o_ref, vt_ref, km_ref, sel_ref, *, kt, topk):
    qi = pl.program_id(1)
    kb = q_ref.shape[1]
    nblk = km_ref.shape[0]
    per_tile = kt // kb

    @pl.when(qi == 0)
    def _():
        def key_mean(c, _):
            km_ref[pl.ds(c, 1), :] = jnp.mean(k_ref[0, pl.ds(pl.multiple_of(c * kb, kb), kb), :].astype(F32),
                                              axis=0, keepdims=True)
            return 0
        _fill_value_slots(v_ref, vt_ref, kb, [h * HEAD_DIM for h in range(N_HEADS)])
        lax.fori_loop(0, nblk, key_mean, 0)

    qt_all = _transposed(q_ref[0])
    qts = [_pair_rows(qt_all, h) for h in range(N_HEADS)]
    ids = lax.broadcasted_iota(jnp.int32, (nblk, kb), 0)
    row = lax.broadcasted_iota(jnp.int32, qt_all.shape, 0)
    km = km_ref[...]
    for h in range(N_HEADS):
        in_head = jnp.logical_and(row >= h * HEAD_DIM, row < (h + 1) * HEAD_DIM)
        gscore = _dot_exact(km, jnp.where(in_head, qt_all, 0.0))
        gscore = jnp.where(ids < qi, gscore, NEG)
        sel_ref[h] = jnp.where(_select_top(gscore, ids.astype(F32), topk, 0.5 * NEG) > 0.5, 0.0, NEG)

    rk = lax.broadcasted_iota(jnp.int32, (kt, kb), 0)
    rel = rk - lax.broadcasted_iota(jnp.int32, (kt, kb), 1)
    bias = [slope_ref[h] * rel.astype(F32) for h in range(N_HEADS)]

    def scores(j):
        k0 = pl.multiple_of(j * kt, kt)
        return tuple(_dot(k_ref[0, pl.ds(k0, kt), _pair_cols(h)], qts[h]) for h in range(N_HEADS))

    def values(j, h):
        return vt_ref[_value_slot(h), pl.ds(pl.multiple_of(j * kt, kt), kt)]

    def unselected(j, h):
        rows = [jnp.broadcast_to(sel_ref[h, pl.ds(j * per_tile + r, 1), :], (kb, kb)) for r in range(per_tile)]
        return jnp.concatenate(rows, axis=0)

    def finish(j, raw):
        off = (qi * kb - j * kt).astype(F32)
        return tuple((raw[h] + bias[h] + unselected(j, h), slope_ref[h] * off) for h in range(N_HEADS))

    def finish_last(j, raw):
        d0 = qi * kb - j * kt
        own_causal = jnp.logical_and(rk >= d0, rel <= d0)
        out = []
        for h in range(N_HEADS):
            s = raw[h] + bias[h]
            out.append((jnp.where(own_causal, s, s + unselected(j, h)), slope_ref[h] * d0.astype(F32)))
        return tuple(out)

    _store_heads(o_ref, _causal_softmax(qi, kb, kt, N_HEADS, scores, values, finish, finish_last))


def moba_attention(a):
    b, s, _ = a.shape
    kb = MOBA_BLOCK
    nblk = s // kb
    topk = min(MOBA_TOPK, nblk - 1)
    slopes = jnp.asarray(_alibi_slopes(N_HEADS) * np.float32(LOG2E))
    c = COL_MOBA // BRANCH_W
    return pl.pallas_call(
        functools.partial(_moba_kernel, kt=min(SOFTMAX_KEY_TILE, s), topk=topk),
        grid_spec=pltpu.PrefetchScalarGridSpec(
            num_scalar_prefetch=1,
            grid=(b, nblk),
            in_specs=[pl.BlockSpec((1, kb, BRANCH_W), lambda bi, qi, sl: (bi, qi, c)),
                      pl.BlockSpec((1, s, BRANCH_W), lambda bi, qi, sl: (bi, 0, c + 1)),
                      pl.BlockSpec((1, s, BRANCH_W), lambda bi, qi, sl: (bi, 0, c + 2))],
            out_specs=pl.BlockSpec((1, kb, BRANCH_W), lambda bi, qi, sl: (bi, qi, 0)),
            scratch_shapes=[pltpu.VMEM((N_HEADS * VALUE_SLOT, s), BF16),
                            pltpu.VMEM((nblk, BRANCH_W), F32),
                            pltpu.VMEM((N_HEADS, nblk, kb), F32)]),
        out_shape=jax.ShapeDtypeStruct((b, s, BRANCH_W), BF16),
        compiler_params=_params(("parallel", "arbitrary")),
        name="moba_attention",
    )(slopes, a, a, a)


MLA_SLOT = 128


def _rms(x, g):
    return x * lax.rsqrt(jnp.mean(x * x, axis=-1, keepdims=True) + RMS_EPS) * g


def _mla_proj_kernel(x_ref, gq_ref, gkv_ref, wq_ref, wqr_ref, wkv_ref, cq_ref, sq_ref, ck_ref, sk_ref,
                     q_ref, kv_ref, kr_ref):
    x = x_ref[...]
    c_q = _rms(x[:, :MLA_Q_RANK], gq_ref[...]).astype(BF16)
    c_kv = _rms(x[:, MLA_Q_RANK:MLA_Q_RANK + MLA_KV_RANK], gkv_ref[...]).astype(BF16)
    q_ref[...] = (_dot(c_q, wq_ref[...]) * cq_ref[...] + _dot(c_q, wqr_ref[...]) * sq_ref[...]).astype(q_ref.dtype)
    kv_ref[...] = _dot(c_kv, wkv_ref[...]).astype(kv_ref.dtype)
    tail = x[:, MLA_Q_RANK + MLA_KV_RANK:]
    rot = pltpu.roll(tail, LANES - MLA_ROPE, axis=1)
    kr_ref[...] = (tail * ck_ref[...] + rot * sk_ref[...]).astype(kr_ref.dtype)


def _rope_tables(s_len):
    half = MLA_ROPE // 2
    freqs = jnp.power(ROPE_THETA, -jnp.arange(half, dtype=F32) / half)
    ang = jnp.arange(s_len).astype(F32)[:, None] * freqs
    cos = jnp.concatenate([jnp.cos(ang)] * 2, axis=-1)
    sin = jnp.concatenate([jnp.sin(ang)] * 2, axis=-1)
    zk = jnp.zeros((s_len, LANES - MLA_ROPE), F32)
    zq = jnp.zeros((s_len, MLA_SLOT - MLA_NOPE - MLA_ROPE), F32)
    cq = jnp.tile(jnp.concatenate([jnp.ones((s_len, MLA_NOPE), F32), cos, zq], axis=-1), (1, N_HEADS))
    sq = jnp.tile(jnp.concatenate([jnp.zeros((s_len, MLA_NOPE), F32), sin, zq], axis=-1), (1, N_HEADS))
    return cq, sq, jnp.concatenate([cos, zk], axis=-1), jnp.concatenate([sin, zk], axis=-1)


def _rotate_half_cols(w):
    half = w.shape[-1] // 2
    return jnp.concatenate([-w[..., half:], w[..., :half]], axis=-1)


def mla_project(f, g_cq, g_ckv, w_uq, w_ukv, s_len, bm):
    t = f.shape[0]
    wq = w_uq.reshape(MLA_Q_RANK, N_HEADS, MLA_NOPE + MLA_ROPE)
    pad = jnp.zeros((MLA_Q_RANK, N_HEADS, MLA_SLOT - MLA_NOPE - MLA_ROPE), wq.dtype)
    wq_s = jnp.concatenate([wq, pad], axis=-1).reshape(MLA_Q_RANK, -1)
    wqr_s = jnp.concatenate([jnp.zeros_like(wq[..., :MLA_NOPE]), _rotate_half_cols(wq[..., MLA_NOPE:]), pad],
                            axis=-1).reshape(MLA_Q_RANK, -1)
    cq, sq, ck, sk = _rope_tables(s_len)
    q_scale = (MLA_NOPE + MLA_ROPE) ** -0.5 * LOG2E
    nrow = s_len // bm
    row = lambda i: (i, 0)
    pos = lambda i: (i % nrow, 0)
    qw = N_HEADS * MLA_SLOT
    kvw = N_HEADS * (MLA_NOPE + MLA_V)
    return pl.pallas_call(
        _mla_proj_kernel,
        grid=(t // bm,),
        in_specs=[pl.BlockSpec((bm, 512), row),
                  _resident((1, MLA_Q_RANK)), _resident((1, MLA_KV_RANK)),
                  _resident((MLA_Q_RANK, qw)), _resident((MLA_Q_RANK, qw)), _resident((MLA_KV_RANK, kvw)),
                  pl.BlockSpec((bm, qw), pos), pl.BlockSpec((bm, qw), pos),
                  pl.BlockSpec((bm, LANES), pos), pl.BlockSpec((bm, LANES), pos)],
        out_specs=[pl.BlockSpec((bm, qw), row), pl.BlockSpec((bm, kvw), row), pl.BlockSpec((bm, LANES), row)],
        out_shape=[jax.ShapeDtypeStruct((t, qw), BF16), jax.ShapeDtypeStruct((t, kvw), BF16),
                   jax.ShapeDtypeStruct((t, LANES), BF16)],
        compiler_params=_params(("parallel",)),
        name="mla_project",
    )(f, g_cq.reshape(1, -1), g_ckv.reshape(1, -1), (wq_s * q_scale).astype(BF16), (wqr_s * q_scale).astype(BF16),
      w_ukv.astype(BF16), cq, sq, ck, sk)


def _mla_kernel(q_ref, kv_ref, kr_ref, o_ref, vt_ref, *, kt):
    qi = pl.program_id(1)
    qb = q_ref.shape[1]

    @pl.when(qi == 0)
    def _():
        _fill_value_slots(kv_ref, vt_ref, KEY_TILE, [h * LANES + MLA_NOPE for h in range(N_HEADS)])

    qt_all = _transposed(q_ref[0])
    qts = []
    for h in range(N_HEADS):
        g = qt_all[h * MLA_SLOT:(h + 1) * MLA_SLOT]
        row = lax.broadcasted_iota(jnp.int32, g.shape, 0)
        qts.append(jnp.concatenate([jnp.where(row < MLA_NOPE, g, 0.0), g[MLA_NOPE:], jnp.zeros_like(g[MLA_NOPE:])],
                                   axis=0).astype(BF16))
    rel = lax.broadcasted_iota(jnp.int32, (kt, qb), 0) - lax.broadcasted_iota(jnp.int32, (kt, qb), 1)

    def scores(j):
        k0 = pl.multiple_of(j * kt, kt)
        kr = kr_ref[0, pl.ds(k0, kt), :]
        return tuple(_dot(jnp.concatenate([kv_ref[0, pl.ds(k0, kt), h * LANES:(h + 1) * LANES], kr], axis=1), qts[h])
                     for h in range(N_HEADS))

    def values(j, h):
        return vt_ref[_value_slot(h), pl.ds(pl.multiple_of(j * kt, kt), kt)]

    def finish_last(j, raw):
        causal = rel <= qi * qb - j * kt
        return tuple((jnp.where(causal, s, NEG), None) for s in raw)

    outs = _causal_softmax(qi, qb, kt, N_HEADS, scores, values,
                           lambda j, raw: tuple((s, None) for s in raw), finish_last)
    _store_heads(o_ref, outs)


def mla_attention(q, kv, kr):
    b, s, qw = q.shape
    kb = QUERY_BLOCK
    return pl.pallas_call(
        functools.partial(_mla_kernel, kt=min(SOFTMAX_KEY_TILE, s)),
        grid=(b, s // kb),
        in_specs=[pl.BlockSpec((1, kb, qw), lambda bi, qi: (bi, qi, 0)),
                  pl.BlockSpec((1, s, kv.shape[2]), lambda bi, qi: (bi, 0, 0)),
                  pl.BlockSpec((1, s, LANES), lambda bi, qi: (bi, 0, 0))],
        out_specs=pl.BlockSpec((1, kb, BRANCH_W), lambda bi, qi: (bi, qi, 0)),
        out_shape=jax.ShapeDtypeStruct((b, s, BRANCH_W), BF16),
        scratch_shapes=[pltpu.VMEM((N_HEADS * VALUE_SLOT, s), BF16)],
        compiler_params=_params(("parallel", "arbitrary")),
        name="mla_attention",
    )(q, kv, kr)


def _cross_kernel(q_ref, kv_ref, o_ref):
    qt_all = _transposed(q_ref[0])
    kv = kv_ref[0]
    vt = _transposed(kv[:, BRANCH_W:]).astype(BF16)
    outs = []
    for h in range(N_HEADS):
        s = _dot(kv[:, _pair_cols(h)], _pair_rows(qt_all, h))
        p = jnp.exp2(s - jnp.max(s, axis=0, keepdims=True))
        outs.append(_dot(vt[_head_rows(h)], p.astype(BF16)) / jnp.sum(p, axis=0, keepdims=True))
    _store_heads(o_ref, outs)


def cross_attention(a, mkv, qb):
    b, s, _ = a.shape
    n = mkv.shape[1]
    c = COL_MEM_Q // BRANCH_W
    return pl.pallas_call(
        _cross_kernel,
        grid=(b, s // qb),
        in_specs=[pl.BlockSpec((1, qb, BRANCH_W), lambda bi, qi: (bi, qi, c)),
                  pl.BlockSpec((1, n, 2 * BRANCH_W), lambda bi, qi: (bi, 0, 0))],
        out_specs=pl.BlockSpec((1, qb, BRANCH_W), lambda bi, qi: (bi, qi, 0)),
        out_shape=jax.ShapeDtypeStruct((b, s, BRANCH_W), BF16),
        compiler_params=_params(("parallel", "parallel")),
        name="cross_attention",
    )(a, mkv)


def _compress_kernel(x_ref, pe_ref, w1_ref, w2_ref, kc_ref, vct_ref):
    n = kc_ref.shape[1]
    first = jnp.zeros((n, 2 * NSA_PHI_HIDDEN), F32)
    second = jnp.zeros((n, 2 * NSA_PHI_HIDDEN), F32)
    for r in range(NSA_CMP_STRIDE):
        x = x_ref[0, pl.ds(r, n, stride=NSA_CMP_STRIDE), :]
        first = first + _dot((x + pe_ref[r:r + 1]).astype(BF16), w1_ref[r])
        second = second + _dot((x + pe_ref[NSA_CMP_STRIDE + r:NSA_CMP_STRIDE + r + 1]).astype(BF16),
                               w1_ref[NSA_CMP_STRIDE + r])
    hidden = jax.nn.gelu(first + pltpu.roll(second, n - 1, axis=0))
    out = _dot(hidden.astype(BF16), w2_ref[...])
    kc_ref[0] = out.astype(BF16)
    vct_ref[0] = out.T[HEAD_DIM:].astype(BF16)


def _pair_diag(wk, wv):
    z = jnp.zeros_like(wk)
    return jnp.concatenate([jnp.concatenate([wk, z], axis=-1), jnp.concatenate([z, wv], axis=-1)], axis=-2)


def nsa_compress(f, nsa_pe, w_k1, w_k2, w_v1, w_v2):
    b, s, _ = f.shape
    n = s // NSA_CMP_STRIDE
    hd = HEAD_DIM
    w1 = _pair_diag(w_k1.reshape(NSA_CMP_LEN, hd, -1), w_v1.reshape(NSA_CMP_LEN, hd, -1)).astype(BF16)
    w2 = _pair_diag(w_k2, w_v2).astype(BF16)
    pe = jnp.concatenate([nsa_pe, nsa_pe], axis=-1)
    c = (COL_NSA_CMP - COL_F32) // LANES
    return pl.pallas_call(
        _compress_kernel,
        grid=(b,),
        in_specs=[pl.BlockSpec((1, s, LANES), lambda bi: (bi, 0, c)),
                  _resident(pe.shape), _resident(w1.shape), _resident(w2.shape)],
        out_specs=[pl.BlockSpec((1, n, LANES), lambda bi: (bi, 0, 0)),
                   pl.BlockSpec((1, hd, n), lambda bi: (bi, 0, 0))],
        out_shape=[jax.ShapeDtypeStruct((b, n, LANES), BF16), jax.ShapeDtypeStruct((b, hd, n), BF16)],
        compiler_params=_params(("parallel",)),
        name="nsa_compress",
    )(f, pe, w1, w2)


def _nsa_kernel(q_ref, g_ref, slope_ref, kc_ref, vct_ref, ovt_ref, slc_ref, win_ref, o_ref,
                vst_ref, vwt_ref, sel_ref, bias_ref, cbias_ref, m_ref, acc_ref, wanted_ref, *, topn, n_cmp):
    qi = pl.program_id(1)
    qn = q_ref.shape[1]
    kb = KEY_TILE
    lanes = N_HEADS * qn
    dv = HEAD_DIM
    q0 = qi * qn

    slope = slope_ref[...]
    ql = jnp.bitwise_and(lax.broadcasted_iota(jnp.int32, (1, lanes), 1), qn - 1)
    qpos = q0 + ql
    kid = lax.broadcasted_iota(jnp.int32, (kb, 1), 0)
    ncp = kc_ref.shape[1]
    cid = lax.broadcasted_iota(jnp.int32, (ncp, 1), 0)
    cmp_last = cid * NSA_CMP_STRIDE + (NSA_CMP_LEN - 1)
    cmp_end = jnp.where(cid < n_cmp, cmp_last, 1 << 30)

    @pl.when(qi == 0)
    def _():
        _fill_value_slots(slc_ref, vst_ref, kb, [HEAD_DIM])
        _fill_value_slots(win_ref, vwt_ref, kb, [HEAD_DIM])
        bias_ref[...] = slope * (ql - kid).astype(F32)
        cbias_ref[...] = slope * (ql - cmp_last).astype(F32)

    qt_all = _transposed(q_ref[0])
    qt = jnp.concatenate([qt_all[_head_rows(h)] for h in range(N_HEADS)], axis=1)
    qt = jnp.concatenate([qt, jnp.zeros_like(qt)], axis=0).astype(BF16)
    jd = lax.div(q0, kb)
    win_tiles = [jnp.maximum(jd - back, 0) for back in range(NSA_WINDOW // kb + 1)]

    def raw_scores(kv_ref, j):
        return _dot(kv_ref[0, pl.ds(pl.multiple_of(j * kb, kb), kb), :], qt)

    raw_c = _dot(kc_ref[0], qt)
    raw_d = raw_scores(slc_ref, jd)
    raw_w = [raw_scores(win_ref, j) for j in win_tiles]
    bias = bias_ref[...]
    causal = ql - kid >= 0

    valid = cmp_end - ql <= q0
    s = jnp.where(valid, raw_c - cbias_ref[...], NEG)
    e = jnp.where(valid, jnp.exp2(s - jnp.max(s, axis=0, keepdims=True)), 0.0)
    den = jnp.sum(e, axis=0, keepdims=True)
    p_c = e / jnp.where(den > 0.0, den, 1.0)
    o_c = _dot(vct_ref[0], p_c.astype(BF16))

    p_sum = p_c[:, 0:qn]
    for hh in range(1, N_HEADS):
        p_sum = p_sum + p_c[:, hh * qn:(hh + 1) * qn]
    imp = _dot_exact(ovt_ref[...], p_sum)
    nsel = imp.shape[0]
    sid = lax.broadcasted_iota(jnp.int32, (nsel, qn), 0)
    cur = jnp.right_shift(qpos[:, 0:qn], NSA_SEL_LEN.bit_length() - 1)
    forced = jnp.logical_or(sid == 0, sid == cur)
    score = jnp.where(forced, BIG, jnp.where(sid < cur, imp, NEG))
    sel = _select_top(score, sid.astype(F32), topn, 0.5 * NEG)
    sel_ref[...] = jnp.concatenate([jnp.where(sel > 0.5, 0.0, NEG)] * N_HEADS, axis=1)
    per_tile = kb // NSA_SEL_LEN
    for j in range(nsel // per_tile):
        wanted_ref[j] = (jnp.max(sel[j * per_tile:(j + 1) * per_tile]) > 0.5).astype(jnp.int32)

    def sel_tile(j, c, raw, own):
        rows = [jnp.broadcast_to(sel_ref[pl.ds(j * per_tile + r, 1), :], (NSA_SEL_LEN, lanes))
                for r in range(per_tile)]
        s = raw - bias + jnp.concatenate(rows, axis=0)
        if own:
            s = jnp.where(causal, s, NEG)
        shift = slope * ((jd - j) * kb).astype(F32)
        return _softmax_tile(s, vst_ref[:, pl.ds(pl.multiple_of(j * kb, kb), kb)], *c, shift=shift)

    m_ref[...], acc_ref[...] = sel_tile(jd, _softmax_init(lanes), raw_d, True)

    def maybe_sel_tile(j, _):
        @pl.when(wanted_ref[j] > 0)
        def _():
            m_ref[...], acc_ref[...] = sel_tile(j, (m_ref[...], acc_ref[...]), raw_scores(slc_ref, j), False)
        return 0

    lax.fori_loop(0, jd, maybe_sel_tile, 0)
    o_s = _softmax_result(acc_ref[...])

    tops, masked = [], []
    for back, raw in enumerate(raw_w):
        s = raw - bias
        if back == 0:
            s = jnp.where(causal, s, NEG)
        else:
            inside = NSA_WINDOW - back * kb if back * kb + kb > NSA_WINDOW else 2 * kb
            s = jnp.where(ql - kid < jnp.where(jd - back >= 0, inside, -2 * kb), s, NEG)
        masked.append(s)
        tops.append(jnp.max(s, axis=0, keepdims=True) - slope * float(back * kb))
    m_w = functools.reduce(jnp.maximum, tops)
    acc_w = jnp.zeros((VALUE_SLOT, lanes), F32)
    for back, s in enumerate(masked):
        p = jnp.exp2(s - (m_w + slope * float(back * kb)))
        acc_w = acc_w + _dot(vwt_ref[:, pl.ds(pl.multiple_of(win_tiles[back] * kb, kb), kb)], p.astype(BF16))
    o_w = _softmax_result(acc_w)

    gt = _sigmoid(g_ref[0]).T
    def gate(ci):
        return jnp.concatenate([gt[h * 3 + ci:h * 3 + ci + 1] for h in range(N_HEADS)], axis=1)
    out = gate(0) * o_c + gate(1) * o_s + gate(2) * o_w
    _store_heads(o_ref, [out[:, h * qn:(h + 1) * qn] for h in range(N_HEADS)])


def nsa_attention(a, f, kc, vct):
    b, s, _ = a.shape
    ncp = kc.shape[1]
    n_cmp = ncp - NSA_CMP_LEN // NSA_CMP_STRIDE + 1
    nsel = s // NSA_SEL_LEN
    qn = QUERY_BLOCK
    lanes = N_HEADS * qn
    cs = np.arange(ncp) * NSA_CMP_STRIDE
    ss = np.arange(nsel) * NSA_SEL_LEN
    ov = ((cs[:, None] < ss[None, :] + NSA_SEL_LEN) & (cs[:, None] + NSA_CMP_LEN > ss[None, :])
          & (np.arange(ncp)[:, None] < n_cmp)).astype(np.float32)
    slopes = jnp.asarray(np.repeat(_alibi_slopes(N_HEADS) * np.float32(LOG2E), qn)[None, :])
    per_b = lambda bi, qi: (bi, 0, 0)
    return pl.pallas_call(
        functools.partial(_nsa_kernel, topn=min(NSA_TOPN, nsel), n_cmp=n_cmp),
        grid=(b, s // qn),
        in_specs=[pl.BlockSpec((1, qn, BRANCH_W), lambda bi, qi: (bi, qi, COL_NSA_Q // BRANCH_W)),
                  pl.BlockSpec((1, qn, LANES), lambda bi, qi: (bi, qi, (COL_NSA_G - COL_F32) // LANES)),
                  _resident((1, lanes)),
                  pl.BlockSpec((1, ncp, LANES), per_b),
                  pl.BlockSpec((1, HEAD_DIM, ncp), per_b),
                  _resident((nsel, ncp)),
                  pl.BlockSpec((1, s, LANES), lambda bi, qi: (bi, 0, COL_NSA_SLC // LANES)),
                  pl.BlockSpec((1, s, LANES), lambda bi, qi: (bi, 0, COL_NSA_WIN // LANES))],
        out_specs=pl.BlockSpec((1, qn, BRANCH_W), lambda bi, qi: (bi, qi, 0)),
        out_shape=jax.ShapeDtypeStruct((b, s, BRANCH_W), BF16),
        scratch_shapes=[pltpu.VMEM((VALUE_SLOT, s), BF16), pltpu.VMEM((VALUE_SLOT, s), BF16),
                        pltpu.VMEM((nsel, lanes), F32), pltpu.VMEM((KEY_TILE, lanes), F32),
                        pltpu.VMEM((ncp, lanes), F32), pltpu.VMEM((1, lanes), F32),
                        pltpu.VMEM((VALUE_SLOT, lanes), F32), pltpu.SMEM((s // KEY_TILE,), jnp.int32)],
        compiler_params=_params(("parallel", "arbitrary")),
        name="nsa_attention",
    )(a, f, slopes, kc, vct, jnp.asarray(ov.T), a, a)


def _layer_norm(r, g, b):
    mu = jnp.mean(r, axis=-1, keepdims=True)
    c = r - mu
    var = jnp.mean(c * c, axis=-1, keepdims=True)
    return c * lax.rsqrt(var + LN_EPS) * g + b


def _merge_kernel(h_ref, o0_ref, o1_ref, o2_ref, o3_ref, o4_ref, wg_ref, bg_ref, wbr_ref, wout_ref, g_ref, b_ref,
                  out_ref):
    h = h_ref[...]
    hb = h.astype(BF16)
    merged = jnp.zeros(h.shape, F32)
    for i, o_ref in enumerate((o0_ref, o1_ref, o2_ref, o3_ref, o4_ref)):
        gate = _sigmoid(_dot(hb, wg_ref[i]) + bg_ref[i])
        merged = merged + gate * _dot(o_ref[...], wbr_ref[i])
    y = _dot(merged.astype(BF16), wout_ref[...])
    out_ref[...] = _layer_norm(DEEPNORM_ALPHA * h + y, g_ref[...], b_ref[...])


def gated_merge(h, branches, w_gate, b_gate, w_br, w_out, ln_g, ln_b, bm):
    t, d = h.shape
    nb, bw = len(branches), branches[0].shape[1]
    row = lambda i: (i, 0)
    return pl.pallas_call(
        _merge_kernel,
        grid=(t // bm,),
        in_specs=[pl.BlockSpec((bm, d), row)] + [pl.BlockSpec((bm, bw), row)] * nb
                 + [_resident((nb, d, d)), _resident((nb, 1, d)), _resident((nb, bw, d)), _resident((d, d)),
                    _resident((1, d)), _resident((1, d))],
        out_specs=pl.BlockSpec((bm, d), row),
        out_shape=jax.ShapeDtypeStruct((t, d), F32),
        compiler_params=_params(("parallel",)),
        name="gated_merge",
    )(h, *branches, w_gate.astype(BF16), b_gate.reshape(nb, 1, d), w_br.astype(BF16), w_out.astype(BF16),
      ln_g.reshape(1, d), ln_b.reshape(1, d))


ROUTER_LANES = 128


def _moe_kernel(h_ref, wr_ref, br_ref, wup_ref, wdn_ref, g_ref, b_ref, out_ref, hid_ref):
    h = h_ref[...]
    hb = h.astype(BF16)
    bm = h.shape[0]
    logits = _dot_exact(h, wr_ref[...]) + br_ref[...]
    lane = lax.broadcasted_iota(jnp.int32, (bm, ROUTER_LANES), 1)
    lane_f = lane.astype(F32)
    is_g = lane < N_GROUPS
    glog = jnp.where(is_g, logits, NEG)
    gmax = jnp.max(glog, axis=-1, keepdims=True)
    g_sel = jnp.min(jnp.where(glog == gmax, lane_f, 1e9), axis=-1, keepdims=True)
    pg_sel = 1.0 / jnp.sum(jnp.where(is_g, jnp.exp(glog - gmax), 0.0), axis=-1, keepdims=True)
    lo = N_GROUPS + g_sel * EXPERTS_PER_GROUP
    in_grp = jnp.logical_and(lane_f >= lo, lane_f < lo + EXPERTS_PER_GROUP)
    elog = jnp.where(in_grp, logits, NEG)
    emax = jnp.max(elog, axis=-1, keepdims=True)
    ee = jnp.where(in_grp, jnp.exp(elog - emax), 0.0)
    pe = ee / jnp.sum(ee, axis=-1, keepdims=True)
    pe_m = jnp.where(in_grp, pe, -1.0)
    v1 = jnp.max(pe_m, axis=-1, keepdims=True)
    i1 = jnp.min(jnp.where(pe_m == v1, lane_f, 1e9), axis=-1, keepdims=True)
    pe_m2 = jnp.where(lane_f == i1, -1.0, pe_m)
    v2 = jnp.max(pe_m2, axis=-1, keepdims=True)
    i2 = jnp.min(jnp.where(pe_m2 == v2, lane_f, 1e9), axis=-1, keepdims=True)
    norm = pg_sel / (v1 + v2)
    gate = jnp.where(lane_f == i1, v1 * norm, jnp.where(lane_f == i2, v2 * norm, 0.0))
    for e in range(N_EXPERTS):
        au = _dot(hb, wup_ref[e])
        a, u = au[:, :D_EXPERT], au[:, D_EXPERT:]
        w_e = jnp.sum(jnp.where(lane == N_GROUPS + e, gate, 0.0), axis=-1, keepdims=True)
        hid_ref[:, e * D_EXPERT:(e + 1) * D_EXPERT] = (w_e * (a * _sigmoid(a) * u)).astype(BF16)
    y = _dot(hid_ref[...], wdn_ref[...])
    out_ref[...] = _layer_norm(DEEPNORM_ALPHA * h + y, g_ref[...], b_ref[...])


def hierarchical_moe(h, w_rg, b_rg, w_re, b_re, w_up, w_down, ln_g, ln_b, bm):
    t, d = h.shape
    ne = N_EXPERTS
    w_r = jnp.concatenate([w_rg, w_re.transpose(1, 0, 2).reshape(d, ne)], axis=1)
    w_r = jnp.pad(w_r, ((0, 0), (0, ROUTER_LANES - w_r.shape[1])))
    b_r = jnp.pad(jnp.concatenate([b_rg, b_re.reshape(ne)]), (0, ROUTER_LANES - N_GROUPS - ne)).reshape(1, -1)
    row = lambda i: (i, 0)
    return pl.pallas_call(
        _moe_kernel,
        grid=(t // bm,),
        in_specs=[pl.BlockSpec((bm, d), row), _resident((d, ROUTER_LANES)), _resident((1, ROUTER_LANES)),
                  _resident((ne, d, 2 * D_EXPERT)), _resident((ne * D_EXPERT, d)), _resident((1, d)),
                  _resident((1, d))],
        out_specs=pl.BlockSpec((bm, d), row),
        out_shape=jax.ShapeDtypeStruct((t, d), F32),
        scratch_shapes=[pltpu.VMEM((bm, ne * D_EXPERT), BF16)],
        compiler_params=_params(("parallel",)),
        name="hierarchical_moe",
    )(h, w_r, b_r, w_up.astype(BF16), w_down.reshape(ne * D_EXPERT, d).astype(BF16),
      ln_g.reshape(1, d), ln_b.reshape(1, d))


def _pad_in_weight(w_in):
    d = w_in.shape[0]
    sizes = (768, 768, MLA_Q_RANK, MLA_KV_RANK, MLA_ROPE, 256, 384, 12, 256)
    offs = np.concatenate([[0], np.cumsum(sizes)])
    sb, moba, c_q, c_kv, k_rope, nsa_q, nsa_kv, nsa_g, mem_q = (w_in[:, offs[i]:offs[i + 1]] for i in range(len(sizes)))
    z = lambda n: jnp.zeros((d, n), w_in.dtype)
    q_scale = HEAD_DIM ** -0.5 * LOG2E
    scale_q = lambda qkv: jnp.concatenate([qkv[:, :BRANCH_W] * q_scale, qkv[:, BRANCH_W:]], axis=1)
    return jnp.concatenate([scale_q(sb), scale_q(moba), c_q, c_kv, k_rope, _rotate_half_cols(k_rope), z(64),
                            nsa_kv[:, :128], nsa_g, z(116), nsa_kv[:, 128:], nsa_q * q_scale, mem_q * q_scale], axis=1)


def kernel(x, mem, w_in, g_cq, g_ckv, w_uq, w_ukv, nsa_pe, w_phi_k1, w_phi_k2, w_phi_v1, w_phi_v2, w_mem_kv, w_br,
           w_gate, b_gate, w_out, ln1_g, ln1_b, w_rg, b_rg, w_re, b_re, w_up, w_down, ln2_g, ln2_b):
    b, s_len, d = x.shape
    s = -(-s_len // MOBA_BLOCK) * MOBA_BLOCK
    t = b * s
    n_mem = mem.shape[1]
    h = jnp.pad(x, ((0, 0), (0, s - s_len), (0, 0))).reshape(t, d)
    for l in range(w_in.shape[0]):
        a, f = in_projection(h, _pad_in_weight(w_in[l]).astype(BF16), bm=512)
        a3, f3 = a.reshape(b, s, IN_PAD), f.reshape(b, s, F32_W)
        q, kv, kr = mla_project(f, g_cq[l], g_ckv[l], w_uq[l], w_ukv[l], s, bm=512)
        kc, vct = nsa_compress(f3, nsa_pe[l], w_phi_k1[l], w_phi_k2[l], w_phi_v1[l], w_phi_v2[l])
        mkv = matmul(mem.reshape(b * n_mem, d), w_mem_kv[l].astype(BF16), bm=n_mem, out_dtype=BF16)
        branches = [sb_attention(a3),
                    moba_attention(a3),
                    mla_attention(q.reshape(b, s, -1), kv.reshape(b, s, -1), kr.reshape(b, s, -1)),
                    nsa_attention(a3, f3, kc, vct),
                    cross_attention(a3, mkv.reshape(b, n_mem, -1), qb=512)]
        h = gated_merge(h, [o.reshape(t, BRANCH_W) for o in branches], w_gate[l], b_gate[l], w_br[l], w_out[l],
                        ln1_g[l], ln1_b[l], bm=256)
        h = hierarchical_moe(h, w_rg[l], b_rg[l], w_re[l], b_re[l], w_up[l], w_down[l], ln2_g[l], ln2_b[l], bm=256)
    return h.reshape(b, s, d)[:, :s_len]
```

```python
import functools

import jax
import jax.numpy as jnp
import numpy as np
from jax import lax
from jax.experimental import pallas as pl
from jax.experimental.pallas import tpu as pltpu

DEPTH = 4
HEAD_DIM = 64
N_HEADS = 4
BRANCH_W = N_HEADS * HEAD_DIM
N_BRANCHES = 5
MOBA_BLOCK = 256
MOBA_TOPK = 3
MLA_Q_RANK = 256
MLA_KV_RANK = 128
MLA_NOPE = 64
MLA_ROPE = 32
MLA_V = 64
ROPE_THETA = 10000.0
NSA_CMP_LEN = 32
NSA_CMP_STRIDE = 16
NSA_SEL_LEN = 64
NSA_TOPN = 16
NSA_WINDOW = 512
NSA_PHI_HIDDEN = 128
N_GROUPS = 4
EXPERTS_PER_GROUP = 4
N_EXPERTS = N_GROUPS * EXPERTS_PER_GROUP
D_EXPERT = 256
DEEPNORM_ALPHA = (2.0 * DEPTH) ** 0.25
LN_EPS = 1e-5
RMS_EPS = 1e-6
NEG = -1e30
BIG = 1e30

LANES = 128
BF16_SUBLANES = 16
VALUE_SLOT = HEAD_DIM + BF16_SUBLANES
QUERY_BLOCK = 256
KEY_TILE = 256
SOFTMAX_KEY_TILE = 1024
SB_KEY_TILE = 512
LOG2E = 1.4426950408889634
VMEM_LIMIT_BYTES = 56 * 1024 * 1024

F32 = jnp.float32
BF16 = jnp.bfloat16
HIGHEST = lax.Precision.HIGHEST

COL_SB = 0
COL_MOBA = 768
COL_MLA = 1536
COL_NSA_CMP = 2048
COL_NSA_G = 2176
COL_NSA_SLC = 2304
COL_NSA_WIN = 2432
COL_NSA_Q = 2560
COL_MEM_Q = 2816
IN_PAD = 3072
COL_F32 = COL_MLA
F32_W = COL_NSA_SLC - COL_MLA


def _params(semantics):
    return pltpu.CompilerParams(dimension_semantics=semantics, vmem_limit_bytes=VMEM_LIMIT_BYTES)


def _dot(a, b):
    return jnp.dot(a, b, preferred_element_type=F32)


def _dot_exact(a, b):
    return jnp.dot(a, b, preferred_element_type=F32, precision=HIGHEST)


def _sigmoid(x):
    return 1.0 / (1.0 + jnp.exp(-x))


def _resident(shape):
    zeros = (0,) * len(shape)
    return pl.BlockSpec(shape, lambda *_: zeros, pipeline_mode=pl.Buffered(1))


def _in_proj_kernel(x_ref, w_ref, o_ref, f_ref):
    y = _dot(x_ref[...].astype(BF16), w_ref[...])
    o_ref[...] = y.astype(BF16)
    f_ref[...] = y[:, COL_F32:COL_F32 + F32_W]


def in_projection(h, w, bm):
    t, d = h.shape
    n = w.shape[1]
    return pl.pallas_call(
        _in_proj_kernel,
        grid=(t // bm,),
        in_specs=[pl.BlockSpec((bm, d), lambda i: (i, 0)), _resident((d, n))],
        out_specs=[pl.BlockSpec((bm, n), lambda i: (i, 0)), pl.BlockSpec((bm, F32_W), lambda i: (i, 0))],
        out_shape=[jax.ShapeDtypeStruct((t, n), BF16), jax.ShapeDtypeStruct((t, F32_W), F32)],
        compiler_params=_params(("parallel",)),
        name="in_projection",
    )(h, w)


def _mm_kernel(x_ref, w_ref, o_ref):
    o_ref[...] = _dot(x_ref[...].astype(BF16), w_ref[...]).astype(o_ref.dtype)


def matmul(x, w, *, bm, out_dtype=F32):
    m, k = x.shape
    n = w.shape[1]
    return pl.pallas_call(
        _mm_kernel,
        grid=(m // bm,),
        in_specs=[pl.BlockSpec((bm, k), lambda i: (i, 0)), _resident((k, n))],
        out_specs=pl.BlockSpec((bm, n), lambda i: (i, 0)),
        out_shape=jax.ShapeDtypeStruct((m, n), out_dtype),
        compiler_params=_params(("parallel",)),
        name="matmul",
    )(x, w)


def _softmax_tile(s, v_slot, m, acc, shift=None):
    top = jnp.max(s, axis=0, keepdims=True)
    m_new = jnp.maximum(m, top if shift is None else top - shift)
    p = jnp.exp2(s - (m_new if shift is None else m_new + shift))
    return m_new, jnp.exp2(m - m_new) * acc + _dot(v_slot, p.astype(BF16))


def _softmax_init(q):
    return (jnp.full((1, q), NEG, F32), jnp.zeros((VALUE_SLOT, q), F32))


def _softmax_result(acc):
    return acc[:HEAD_DIM] / acc[HEAD_DIM:HEAD_DIM + 1]


def _causal_softmax(qi, qb, kt, heads, scores, values, finish, finish_last):
    def tile(j, state, fin):
        return tuple(_softmax_tile(fin[h][0], values(j, h), *state[h], shift=fin[h][1]) for h in range(heads))

    last = lax.div(qi * qb, kt)
    state = tuple(_softmax_init(qb) for _ in range(heads))
    state = lax.fori_loop(0, last, lambda j, st: tile(j, st, finish(j, scores(j))), state)
    state = tile(last, state, finish_last(last, scores(last)))
    return [_softmax_result(acc) for _, acc in state]


def _causal_softmax_prefetched(qi, qb, kt, heads, scores, values, finish, finish_last, raw_ref):
    def tile(j, state, fin):
        return tuple(_softmax_tile(fin[h][0], values(j, h), *state[h], shift=fin[h][1]) for h in range(heads))

    def prefetch(j, slot):
        for h, raw in enumerate(scores(j)):
            raw_ref[slot, h] = raw

    def held(slot):
        return tuple(raw_ref[slot, h] for h in range(heads))

    def pair(i, state):
        prefetch(2 * i + 1, 1)
        state = tile(2 * i, state, finish(2 * i, held(0)))
        prefetch(2 * i + 2, 0)
        return tile(2 * i + 1, state, finish(2 * i + 1, held(1)))

    def odd_tail(state):
        prefetch(last, 1)
        state = tile(last - 1, state, finish(last - 1, held(0)))
        return tile(last, state, finish_last(last, held(1)))

    def even_tail(state):
        return tile(last, state, finish_last(last, held(0)))

    last = lax.div(qi * qb, kt)
    prefetch(0, 0)
    state = tuple(_softmax_init(qb) for _ in range(heads))
    state = lax.fori_loop(0, lax.div(last, 2), pair, state)
    state = lax.cond(lax.rem(last, 2) == 1, odd_tail, even_tail, state)
    return [_softmax_result(acc) for _, acc in state]


def _transposed(x):
    return x.astype(F32).T


def _pair_rows(qt_all, h):
    g = qt_all[(h // 2) * LANES:(h // 2 + 1) * LANES]
    row = lax.broadcasted_iota(jnp.int32, g.shape, 0)
    keep = (row >= HEAD_DIM) if h % 2 else (row < HEAD_DIM)
    return jnp.where(keep, g, 0.0).astype(BF16)


def _pair_cols(h):
    return slice((h // 2) * LANES, (h // 2 + 1) * LANES)


def _head_rows(h):
    return slice(h * HEAD_DIM, (h + 1) * HEAD_DIM)


def _fill_transposed(src_ref, dst_ref, kb, extra=None):
    def chunk(c, _):
        r0 = pl.multiple_of(c * kb, kb)
        x = src_ref[0, pl.ds(r0, kb), :].astype(F32)
        dst_ref[:, pl.ds(r0, kb)] = x.T.astype(BF16)
        if extra is not None:
            extra(c, x)
        return 0
    lax.fori_loop(0, src_ref.shape[1] // kb, chunk, 0)


def _fill_value_slots(src_ref, dst_ref, kb, first_rows):
    ones = jnp.ones((BF16_SUBLANES, kb), BF16)

    def chunk(c, _):
        r0 = pl.multiple_of(c * kb, kb)
        xt = src_ref[0, pl.ds(r0, kb), :].astype(F32).T
        for h, r in enumerate(first_rows):
            dst_ref[h * VALUE_SLOT:h * VALUE_SLOT + HEAD_DIM, pl.ds(r0, kb)] = xt[r:r + HEAD_DIM].astype(BF16)
            dst_ref[h * VALUE_SLOT + HEAD_DIM:(h + 1) * VALUE_SLOT, pl.ds(r0, kb)] = ones
        return 0
    lax.fori_loop(0, src_ref.shape[1] // kb, chunk, 0)


def _value_slot(h):
    return slice(h * VALUE_SLOT, (h + 1) * VALUE_SLOT)


def _store_heads(o_ref, outs):
    o_ref[0] = jnp.concatenate(outs, axis=0).T.astype(o_ref.dtype)


def _qkv_specs(s, kb, col):
    c = col // BRANCH_W
    return [pl.BlockSpec((1, kb, BRANCH_W), lambda bi, qi: (bi, qi, c)),
            pl.BlockSpec((1, s, BRANCH_W), lambda bi, qi: (bi, 0, c + 1)),
            pl.BlockSpec((1, s, BRANCH_W), lambda bi, qi: (bi, 0, c + 2))]


def _sb_kernel(q_ref, k_ref, v_ref, tri_ref, o_ref, vt_ref, *, kt):
    qi = pl.program_id(1)
    qb = q_ref.shape[1]
    kb = KEY_TILE
    pieces = kt // kb

    @pl.when(qi == 0)
    def _():
        _fill_transposed(v_ref, vt_ref, kb)

    qt_all = _transposed(q_ref[0])
    qts = [_pair_rows(qt_all, h) for h in range(N_HEADS)]
    tri = tri_ref[...]
    rel = lax.broadcasted_iota(jnp.int32, (kb, qb), 0) - lax.broadcasted_iota(jnp.int32, (kb, qb), 1)

    def tile(j, state, last):
        order = [(u, h) for u in reversed(range(pieces)) for h in range(N_HEADS)]
        k0 = {u: pl.multiple_of(j * kt + u * kb, kb) for u in range(pieces)}
        raw = {(u, h): _dot(k_ref[0, pl.ds(k0[u], kb), _pair_cols(h)], qts[h]) for u, h in order}
        log_beta, later, col_sum, past = {}, {}, {}, {}
        for u, h in order:
            z = raw[u, h]
            nz = -z
            soft = jnp.log(1.0 + jnp.exp2(jnp.minimum(z, nz))) * LOG2E
            log_keep = jnp.minimum(nz, 0.0) - soft
            log_beta[u, h] = log_keep + z
            if last:
                past[u] = rel < qi * qb - k0[u]
                log_keep = jnp.where(past[u], log_keep, 0.0)
            sums = _dot(tri, log_keep.astype(BF16))
            later[u, h] = sums[:kb]
            col_sum[u, h] = sums[kb:kb + 1]
        state = list(state)
        for u, h in order:
            carry, acc = state[h]
            a = jnp.exp2(log_beta[u, h] + later[u, h] + carry)
            if last:
                a = jnp.where(past[u], a, 0.0)
            state[h] = (carry + col_sum[u, h], acc + _dot(vt_ref[_head_rows(h), pl.ds(k0[u], kb)], a.astype(BF16)))
        return tuple(state)

    last = lax.div(qi * qb, kt)
    state = tuple((jnp.zeros((1, qb), F32), jnp.zeros((HEAD_DIM, qb), F32)) for _ in range(N_HEADS))
    state = tile(last, state, True)
    state = lax.fori_loop(0, last, lambda t, st: tile(last - 1 - t, st, False), state)
    _store_heads(o_ref, [st[1] for st in state])


def sb_attention(a):
    b, s, _ = a.shape
    kb, qb = KEY_TILE, QUERY_BLOCK
    tri = jnp.asarray(np.concatenate([np.triu(np.ones((kb, kb), np.float32), 1), np.ones((BF16_SUBLANES, kb), np.float32)]),
                      BF16)
    return pl.pallas_call(
        functools.partial(_sb_kernel, kt=min(SB_KEY_TILE, s)),
        grid=(b, s // qb),
        in_specs=_qkv_specs(s, qb, COL_SB) + [_resident(tri.shape)],
        out_specs=pl.BlockSpec((1, qb, BRANCH_W), lambda bi, qi: (bi, qi, 0)),
        out_shape=jax.ShapeDtypeStruct((b, s, BRANCH_W), BF16),
        scratch_shapes=[pltpu.VMEM((BRANCH_W, s), BF16)],
        compiler_params=_params(("parallel", "arbitrary")),
        name="sb_attention",
    )(a, a, a, tri)


def _alibi_slopes(n):
    return np.power(2.0, -8.0 * np.arange(1, n + 1, dtype=np.float64) / n).astype(np.float32)


def _select_top(score, ids, count, floor):
    def step(_, c):
        score, sel = c
        mx = jnp.max(score, axis=0, keepdims=True)
        idx = jnp.min(jnp.where(score == mx, ids, 1e9), axis=0, keepdims=True)
        pick = ids == idx
        sel = jnp.where(jnp.logical_and(pick, mx > floor), 1.0, sel)
        score = jnp.where(pick, -jnp.inf, score)
        return score, sel
    return lax.fori_loop(0, count, step, (score, jnp.zeros_like(score)))[1]


def _moba_kernel(slope_ref, q_ref, k_ref, v_ref, o_ref, vt_ref, km_ref, sel_ref, raw_ref, *, kt, topk):
    qi = pl.program_id(1)
    kb = q_ref.shape[1]
    nblk = km_ref.shape[0]
    per_tile = kt // kb

    @pl.when(qi == 0)
    def _():
        def key_mean(c, _):
            km_ref[pl.ds(c, 1), :] = jnp.mean(k_ref[0, pl.ds(pl.multiple_of(c * kb, kb), kb), :].astype(F32),
                                              axis=0, keepdims=True)
            return 0
        _fill_value_slots(v_ref, vt_ref, kb, [h * HEAD_DIM for h in range(N_HEADS)])
        lax.fori_loop(0, nblk, key_mean, 0)

    qt_all = _transposed(q_ref[0])
    qts = [_pair_rows(qt_all, h) for h in range(N_HEADS)]
    ids = lax.broadcasted_iota(jnp.int32, (nblk, kb), 0)
    row = lax.broadcasted_iota(jnp.int32, qt_all.shape, 0)
    km = km_ref[...]
    for h in range(N_HEADS):
        in_head = jnp.logical_and(row >= h * HEAD_DIM, row < (h + 1) * HEAD_DIM)
        gscore = _dot_exact(km, jnp.where(in_head, qt_all, 0.0))
        gscore = jnp.where(ids < qi, gscore, NEG)
        sel_ref[h] = jnp.where(_select_top(gscore, ids.astype(F32), topk, 0.5 * NEG) > 0.5, 0.0, NEG)

    rk = lax.broadcasted_iota(jnp.int32, (kt, kb), 0)
    rel = rk - lax.broadcasted_iota(jnp.int32, (kt, kb), 1)
    bias = [slope_ref[h] * rel.astype(F32) for h in range(N_HEADS)]

    def scores(j):
        k0 = pl.multiple_of(j * kt, kt)
        return tuple(_dot(k_ref[0, pl.ds(k0, kt), _pair_cols(h)], qts[h]) for h in range(N_HEADS))

    def values(j, h):
        return vt_ref[_value_slot(h), pl.ds(pl.multiple_of(j * kt, kt), kt)]

    def unselected(j, h):
        rows = [jnp.broadcast_to(sel_ref[h, pl.ds(j * per_tile + r, 1), :], (kb, kb)) for r in range(per_tile)]
        return jnp.concatenate(rows, axis=0)

    def finish(j, raw):
        off = (qi * kb - j * kt).astype(F32)
        return tuple((raw[h] + bias[h] + unselected(j, h), slope_ref[h] * off) for h in range(N_HEADS))

    def finish_last(j, raw):
        d0 = qi * kb - j * kt
        own_causal = jnp.logical_and(rk >= d0, rel <= d0)
        out = []
        for h in range(N_HEADS):
            s = raw[h] + bias[h]
            out.append((jnp.where(own_causal, s, s + unselected(j, h)), slope_ref[h] * d0.astype(F32)))
        return tuple(out)

    _store_heads(o_ref, _causal_softmax_prefetched(qi, kb, kt, N_HEADS, scores, values, finish, finish_last, raw_ref))


def moba_attention(a):
    b, s, _ = a.shape
    kb = MOBA_BLOCK
    nblk = s // kb
    topk = min(MOBA_TOPK, nblk - 1)
    slopes = jnp.asarray(_alibi_slopes(N_HEADS) * np.float32(LOG2E))
    c = COL_MOBA // BRANCH_W
    return pl.pallas_call(
        functools.partial(_moba_kernel, kt=min(SOFTMAX_KEY_TILE, s), topk=topk),
        grid_spec=pltpu.PrefetchScalarGridSpec(
            num_scalar_prefetch=1,
            grid=(b, nblk),
            in_specs=[pl.BlockSpec((1, kb, BRANCH_W), lambda bi, qi, sl: (bi, qi, c)),
                      pl.BlockSpec((1, s, BRANCH_W), lambda bi, qi, sl: (bi, 0, c + 1)),
                      pl.BlockSpec((1, s, BRANCH_W), lambda bi, qi, sl: (bi, 0, c + 2))],
            out_specs=pl.BlockSpec((1, kb, BRANCH_W), lambda bi, qi, sl: (bi, qi, 0)),
            scratch_shapes=[pltpu.VMEM((N_HEADS * VALUE_SLOT, s), BF16),
                            pltpu.VMEM((nblk, BRANCH_W), F32),
                            pltpu.VMEM((N_HEADS, nblk, kb), F32),
                            pltpu.VMEM((2, N_HEADS, min(SOFTMAX_KEY_TILE, s), kb), F32)]),
        out_shape=jax.ShapeDtypeStruct((b, s, BRANCH_W), BF16),
        compiler_params=_params(("parallel", "arbitrary")),
        name="moba_attention",
    )(slopes, a, a, a)


MLA_SLOT = 128


def _rms(x, g):
    return x * lax.rsqrt(jnp.mean(x * x, axis=-1, keepdims=True) + RMS_EPS) * g


def _mla_proj_kernel(x_ref, gq_ref, gkv_ref, wq_ref, wqr_ref, wkv_ref, cq_ref, sq_ref, ck_ref, sk_ref,
                     q_ref, kv_ref, kr_ref):
    x = x_ref[...]
    c_q = _rms(x[:, :MLA_Q_RANK], gq_ref[...]).astype(BF16)
    c_kv = _rms(x[:, MLA_Q_RANK:MLA_Q_RANK + MLA_KV_RANK], gkv_ref[...]).astype(BF16)
    q_ref[...] = (_dot(c_q, wq_ref[...]) * cq_ref[...] + _dot(c_q, wqr_ref[...]) * sq_ref[...]).astype(q_ref.dtype)
    kv_ref[...] = _dot(c_kv, wkv_ref[...]).astype(kv_ref.dtype)
    tail = x[:, MLA_Q_RANK + MLA_KV_RANK:]
    rot = pltpu.roll(tail, LANES - MLA_ROPE, axis=1)
    kr_ref[...] = (tail * ck_ref[...] + rot * sk_ref[...]).astype(kr_ref.dtype)


def _rope_tables(s_len):
    half = MLA_ROPE // 2
    freqs = jnp.power(ROPE_THETA, -jnp.arange(half, dtype=F32) / half)
    ang = jnp.arange(s_len).astype(F32)[:, None] * freqs
    cos = jnp.concatenate([jnp.cos(ang)] * 2, axis=-1)
    sin = jnp.concatenate([jnp.sin(ang)] * 2, axis=-1)
    zk = jnp.zeros((s_len, LANES - MLA_ROPE), F32)
    zq = jnp.zeros((s_len, MLA_SLOT - MLA_NOPE - MLA_ROPE), F32)
    cq = jnp.tile(jnp.concatenate([jnp.ones((s_len, MLA_NOPE), F32), cos, zq], axis=-1), (1, N_HEADS))
    sq = jnp.tile(jnp.concatenate([jnp.zeros((s_len, MLA_NOPE), F32), sin, zq], axis=-1), (1, N_HEADS))
    return cq, sq, jnp.concatenate([cos, zk], axis=-1), jnp.concatenate([sin, zk], axis=-1)


def _rotate_half_cols(w):
    half = w.shape[-1] // 2
    return jnp.concatenate([-w[..., half:], w[..., :half]], axis=-1)


def mla_project(f, g_cq, g_ckv, w_uq, w_ukv, s_len, bm):
    t = f.shape[0]
    wq = w_uq.reshape(MLA_Q_RANK, N_HEADS, MLA_NOPE + MLA_ROPE)
    pad = jnp.zeros((MLA_Q_RANK, N_HEADS, MLA_SLOT - MLA_NOPE - MLA_ROPE), wq.dtype)
    wq_s = jnp.concatenate([wq, pad], axis=-1).reshape(MLA_Q_RANK, -1)
    wqr_s = jnp.concatenate([jnp.zeros_like(wq[..., :MLA_NOPE]), _rotate_half_cols(wq[..., MLA_NOPE:]), pad],
                            axis=-1).reshape(MLA_Q_RANK, -1)
    cq, sq, ck, sk = _rope_tables(s_len)
    q_scale = (MLA_NOPE + MLA_ROPE) ** -0.5 * LOG2E
    nrow = s_len // bm
    row = lambda i: (i, 0)
    pos = lambda i: (i % nrow, 0)
    qw = N_HEADS * MLA_SLOT
    kvw = N_HEADS * (MLA_NOPE + MLA_V)
    return pl.pallas_call(
        _mla_proj_kernel,
        grid=(t // bm,),
        in_specs=[pl.BlockSpec((bm, 512), row),
                  _resident((1, MLA_Q_RANK)), _resident((1, MLA_KV_RANK)),
                  _resident((MLA_Q_RANK, qw)), _resident((MLA_Q_RANK, qw)), _resident((MLA_KV_RANK, kvw)),
                  pl.BlockSpec((bm, qw), pos), pl.BlockSpec((bm, qw), pos),
                  pl.BlockSpec((bm, LANES), pos), pl.BlockSpec((bm, LANES), pos)],
        out_specs=[pl.BlockSpec((bm, qw), row), pl.BlockSpec((bm, kvw), row), pl.BlockSpec((bm, LANES), row)],
        out_shape=[jax.ShapeDtypeStruct((t, qw), BF16), jax.ShapeDtypeStruct((t, kvw), BF16),
                   jax.ShapeDtypeStruct((t, LANES), BF16)],
        compiler_params=_params(("parallel",)),
        name="mla_project",
    )(f, g_cq.reshape(1, -1), g_ckv.reshape(1, -1), (wq_s * q_scale).astype(BF16), (wqr_s * q_scale).astype(BF16),
      w_ukv.astype(BF16), cq, sq, ck, sk)


def _mla_kernel(q_ref, kv_ref, kr_ref, o_ref, vt_ref, raw_ref, *, kt):
    qi = pl.program_id(1)
    qb = q_ref.shape[1]

    @pl.when(qi == 0)
    def _():
        _fill_value_slots(kv_ref, vt_ref, KEY_TILE, [h * LANES + MLA_NOPE for h in range(N_HEADS)])

    qt_all = _transposed(q_ref[0])
    qts = []
    for h in range(N_HEADS):
        g = qt_all[h * MLA_SLOT:(h + 1) * MLA_SLOT]
        row = lax.broadcasted_iota(jnp.int32, g.shape, 0)
        qts.append(jnp.concatenate([jnp.where(row < MLA_NOPE, g, 0.0), g[MLA_NOPE:], jnp.zeros_like(g[MLA_NOPE:])],
                                   axis=0).astype(BF16))
    rel = lax.broadcasted_iota(jnp.int32, (kt, qb), 0) - lax.broadcasted_iota(jnp.int32, (kt, qb), 1)

    def scores(j):
        k0 = pl.multiple_of(j * kt, kt)
        kr = kr_ref[0, pl.ds(k0, kt), :]
        return tuple(_dot(jnp.concatenate([kv_ref[0, pl.ds(k0, kt), h * LANES:(h + 1) * LANES], kr], axis=1), qts[h])
                     for h in range(N_HEADS))

    def values(j, h):
        return vt_ref[_value_slot(h), pl.ds(pl.multiple_of(j * kt, kt), kt)]

    def finish_last(j, raw):
        causal = rel <= qi * qb - j * kt
        return tuple((jnp.where(causal, s, NEG), None) for s in raw)

    outs = _causal_softmax_prefetched(qi, qb, kt, N_HEADS, scores, values,
                                      lambda j, raw: tuple((s, None) for s in raw), finish_last, raw_ref)
    _store_heads(o_ref, outs)


def mla_attention(q, kv, kr):
    b, s, qw = q.shape
    kb = QUERY_BLOCK
    return pl.pallas_call(
        functools.partial(_mla_kernel, kt=min(SOFTMAX_KEY_TILE, s)),
        grid=(b, s // kb),
        in_specs=[pl.BlockSpec((1, kb, qw), lambda bi, qi: (bi, qi, 0)),
                  pl.BlockSpec((1, s, kv.shape[2]), lambda bi, qi: (bi, 0, 0)),
                  pl.BlockSpec((1, s, LANES), lambda bi, qi: (bi, 0, 0))],
        out_specs=pl.BlockSpec((1, kb, BRANCH_W), lambda bi, qi: (bi, qi, 0)),
        out_shape=jax.ShapeDtypeStruct((b, s, BRANCH_W), BF16),
        scratch_shapes=[pltpu.VMEM((N_HEADS * VALUE_SLOT, s), BF16),
                        pltpu.VMEM((2, N_HEADS, min(SOFTMAX_KEY_TILE, s), kb), F32)],
        compiler_params=_params(("parallel", "arbitrary")),
        name="mla_attention",
    )(q, kv, kr)


def _cross_kernel(q_ref, kv_ref, o_ref):
    qt_all = _transposed(q_ref[0])
    kv = kv_ref[0]
    vt = _transposed(kv[:, BRANCH_W:]).astype(BF16)
    outs = []
    for h in range(N_HEADS):
        s = _dot(kv[:, _pair_cols(h)], _pair_rows(qt_all, h))
        p = jnp.exp2(s - jnp.max(s, axis=0, keepdims=True))
        outs.append(_dot(vt[_head_rows(h)], p.astype(BF16)) / jnp.sum(p, axis=0, keepdims=True))
    _store_heads(o_ref, outs)


def cross_attention(a, mkv, qb):
    b, s, _ = a.shape
    n = mkv.shape[1]
    c = COL_MEM_Q // BRANCH_W
    return pl.pallas_call(
        _cross_kernel,
        grid=(b, s // qb),
        in_specs=[pl.BlockSpec((1, qb, BRANCH_W), lambda bi, qi: (bi, qi, c)),
                  pl.BlockSpec((1, n, 2 * BRANCH_W), lambda bi, qi: (bi, 0, 0))],
        out_specs=pl.BlockSpec((1, qb, BRANCH_W), lambda bi, qi: (bi, qi, 0)),
        out_shape=jax.ShapeDtypeStruct((b, s, BRANCH_W), BF16),
        compiler_params=_params(("parallel", "parallel")),
        name="cross_attention",
    )(a, mkv)


def _compress_kernel(x_ref, pe_ref, w1_ref, w2_ref, kc_ref, vct_ref):
    n = kc_ref.shape[1]
    first = jnp.zeros((n, 2 * NSA_PHI_HIDDEN), F32)
    second = jnp.zeros((n, 2 * NSA_PHI_HIDDEN), F32)
    for r in range(NSA_CMP_STRIDE):
        x = x_ref[0, pl.ds(r, n, stride=NSA_CMP_STRIDE), :]
        first = first + _dot((x + pe_ref[r:r + 1]).astype(BF16), w1_ref[r])
        second = second + _dot((x + pe_ref[NSA_CMP_STRIDE + r:NSA_CMP_STRIDE + r + 1]).astype(BF16),
                               w1_ref[NSA_CMP_STRIDE + r])
    hidden = jax.nn.gelu(first + pltpu.roll(second, n - 1, axis=0))
    out = _dot(hidden.astype(BF16), w2_ref[...])
    kc_ref[0] = out.astype(BF16)
    vct_ref[0] = out.T[HEAD_DIM:].astype(BF16)


def _pair_diag(wk, wv):
    z = jnp.zeros_like(wk)
    return jnp.concatenate([jnp.concatenate([wk, z], axis=-1), jnp.concatenate([z, wv], axis=-1)], axis=-2)


def nsa_compress(f, nsa_pe, w_k1, w_k2, w_v1, w_v2):
    b, s, _ = f.shape
    n = s // NSA_CMP_STRIDE
    hd = HEAD_DIM
    w1 = _pair_diag(w_k1.reshape(NSA_CMP_LEN, hd, -1), w_v1.reshape(NSA_CMP_LEN, hd, -1)).astype(BF16)
    w2 = _pair_diag(w_k2, w_v2).astype(BF16)
    pe = jnp.concatenate([nsa_pe, nsa_pe], axis=-1)
    c = (COL_NSA_CMP - COL_F32) // LANES
    return pl.pallas_call(
        _compress_kernel,
        grid=(b,),
        in_specs=[pl.BlockSpec((1, s, LANES), lambda bi: (bi, 0, c)),
                  _resident(pe.shape), _resident(w1.shape), _resident(w2.shape)],
        out_specs=[pl.BlockSpec((1, n, LANES), lambda bi: (bi, 0, 0)),
                   pl.BlockSpec((1, hd, n), lambda bi: (bi, 0, 0))],
        out_shape=[jax.ShapeDtypeStruct((b, n, LANES), BF16), jax.ShapeDtypeStruct((b, hd, n), BF16)],
        compiler_params=_params(("parallel",)),
        name="nsa_compress",
    )(f, pe, w1, w2)


def _nsa_kernel(q_ref, g_ref, slope_ref, kc_ref, vct_ref, ovt_ref, slc_ref, win_ref, o_ref,
                vst_ref, vwt_ref, sel_ref, bias_ref, cbias_ref, m_ref, acc_ref, wanted_ref, *, topn, n_cmp):
    qi = pl.program_id(1)
    qn = q_ref.shape[1]
    kb = KEY_TILE
    lanes = N_HEADS * qn
    dv = HEAD_DIM
    q0 = qi * qn

    slope = slope_ref[...]
    ql = jnp.bitwise_and(lax.broadcasted_iota(jnp.int32, (1, lanes), 1), qn - 1)
    qpos = q0 + ql
    kid = lax.broadcasted_iota(jnp.int32, (kb, 1), 0)
    ncp = kc_ref.shape[1]
    cid = lax.broadcasted_iota(jnp.int32, (ncp, 1), 0)
    cmp_last = cid * NSA_CMP_STRIDE + (NSA_CMP_LEN - 1)
    cmp_end = jnp.where(cid < n_cmp, cmp_last, 1 << 30)

    @pl.when(qi == 0)
    def _():
        _fill_value_slots(slc_ref, vst_ref, kb, [HEAD_DIM])
        _fill_value_slots(win_ref, vwt_ref, kb, [HEAD_DIM])
        bias_ref[...] = slope * (ql - kid).astype(F32)
        cbias_ref[...] = slope * (ql - cmp_last).astype(F32)

    qt_all = _transposed(q_ref[0])
    qt = jnp.concatenate([qt_all[_head_rows(h)] for h in range(N_HEADS)], axis=1)
    qt = jnp.concatenate([qt, jnp.zeros_like(qt)], axis=0).astype(BF16)
    jd = lax.div(q0, kb)
    win_tiles = [jnp.maximum(jd - back, 0) for back in range(NSA_WINDOW // kb + 1)]

    def raw_scores(kv_ref, j):
        return _dot(kv_ref[0, pl.ds(pl.multiple_of(j * kb, kb), kb), :], qt)

    raw_c = _dot(kc_ref[0], qt)
    raw_d = raw_scores(slc_ref, jd)
    raw_w = [raw_scores(win_ref, j) for j in win_tiles]
    bias = bias_ref[...]
    causal = ql - kid >= 0

    valid = cmp_end - ql <= q0
    s = jnp.where(valid, raw_c - cbias_ref[...], NEG)
    e = jnp.where(valid, jnp.exp2(s - jnp.max(s, axis=0, keepdims=True)), 0.0)
    den = jnp.sum(e, axis=0, keepdims=True)
    p_c = e / jnp.where(den > 0.0, den, 1.0)
    o_c = _dot(vct_ref[0], p_c.astype(BF16))

    p_sum = p_c[:, 0:qn]
    for hh in range(1, N_HEADS):
        p_sum = p_sum + p_c[:, hh * qn:(hh + 1) * qn]
    imp = _dot_exact(ovt_ref[...], p_sum)
    nsel = imp.shape[0]
    sid = lax.broadcasted_iota(jnp.int32, (nsel, qn), 0)
    cur = jnp.right_shift(qpos[:, 0:qn], NSA_SEL_LEN.bit_length() - 1)
    forced = jnp.logical_or(sid == 0, sid == cur)
    score = jnp.where(forced, BIG, jnp.where(sid < cur, imp, NEG))
    sel = _select_top(score, sid.astype(F32), topn, 0.5 * NEG)
    sel_ref[...] = jnp.concatenate([jnp.where(sel > 0.5, 0.0, NEG)] * N_HEADS, axis=1)
    per_tile = kb // NSA_SEL_LEN
    for j in range(nsel // per_tile):
        wanted_ref[j] = (jnp.max(sel[j * per_tile:(j + 1) * per_tile]) > 0.5).astype(jnp.int32)

    def sel_tile(j, c, raw, own):
        rows = [jnp.broadcast_to(sel_ref[pl.ds(j * per_tile + r, 1), :], (NSA_SEL_LEN, lanes))
                for r in range(per_tile)]
        s = raw - bias + jnp.concatenate(rows, axis=0)
        if own:
            s = jnp.where(causal, s, NEG)
        shift = slope * ((jd - j) * kb).astype(F32)
        return _softmax_tile(s, vst_ref[:, pl.ds(pl.multiple_of(j * kb, kb), kb)], *c, shift=shift)

    m_ref[...], acc_ref[...] = sel_tile(jd, _softmax_init(lanes), raw_d, True)

    def maybe_sel_tile(j, _):
        @pl.when(wanted_ref[j] > 0)
        def _():
            m_ref[...], acc_ref[...] = sel_tile(j, (m_ref[...], acc_ref[...]), raw_scores(slc_ref, j), False)
        return 0

    lax.fori_loop(0, jd, maybe_sel_tile, 0)
    o_s = _softmax_result(acc_ref[...])

    tops, masked = [], []
    for back, raw in enumerate(raw_w):
        s = raw - bias
        if back == 0:
            s = jnp.where(causal, s, NEG)
        else:
            inside = NSA_WINDOW - back * kb if back * kb + kb > NSA_WINDOW else 2 * kb
            s = jnp.where(ql - kid < jnp.where(jd - back >= 0, inside, -2 * kb), s, NEG)
        masked.append(s)
        tops.append(jnp.max(s, axis=0, keepdims=True) - slope * float(back * kb))
    m_w = functools.reduce(jnp.maximum, tops)
    acc_w = jnp.zeros((VALUE_SLOT, lanes), F32)
    for back, s in enumerate(masked):
        p = jnp.exp2(s - (m_w + slope * float(back * kb)))
        acc_w = acc_w + _dot(vwt_ref[:, pl.ds(pl.multiple_of(win_tiles[back] * kb, kb), kb)], p.astype(BF16))
    o_w = _softmax_result(acc_w)

    gt = _sigmoid(g_ref[0]).T
    def gate(ci):
        return jnp.concatenate([gt[h * 3 + ci:h * 3 + ci + 1] for h in range(N_HEADS)], axis=1)
    out = gate(0) * o_c + gate(1) * o_s + gate(2) * o_w
    _store_heads(o_ref, [out[:, h * qn:(h + 1) * qn] for h in range(N_HEADS)])


def nsa_attention(a, f, kc, vct):
    b, s, _ = a.shape
    ncp = kc.shape[1]
    n_cmp = ncp - NSA_CMP_LEN // NSA_CMP_STRIDE + 1
    nsel = s // NSA_SEL_LEN
    qn = QUERY_BLOCK
    lanes = N_HEADS * qn
    cs = np.arange(ncp) * NSA_CMP_STRIDE
    ss = np.arange(nsel) * NSA_SEL_LEN
    ov = ((cs[:, None] < ss[None, :] + NSA_SEL_LEN) & (cs[:, None] + NSA_CMP_LEN > ss[None, :])
          & (np.arange(ncp)[:, None] < n_cmp)).astype(np.float32)
    slopes = jnp.asarray(np.repeat(_alibi_slopes(N_HEADS) * np.float32(LOG2E), qn)[None, :])
    per_b = lambda bi, qi: (bi, 0, 0)
    return pl.pallas_call(
        functools.partial(_nsa_kernel, topn=min(NSA_TOPN, nsel), n_cmp=n_cmp),
        grid=(b, s // qn),
        in_specs=[pl.BlockSpec((1, qn, BRANCH_W), lambda bi, qi: (bi, qi, COL_NSA_Q // BRANCH_W)),
                  pl.BlockSpec((1, qn, LANES), lambda bi, qi: (bi, qi, (COL_NSA_G - COL_F32) // LANES)),
                  _resident((1, lanes)),
                  pl.BlockSpec((1, ncp, LANES), per_b),
                  pl.BlockSpec((1, HEAD_DIM, ncp), per_b),
                  _resident((nsel, ncp)),
                  pl.BlockSpec((1, s, LANES), lambda bi, qi: (bi, 0, COL_NSA_SLC // LANES)),
                  pl.BlockSpec((1, s, LANES), lambda bi, qi: (bi, 0, COL_NSA_WIN // LANES))],
        out_specs=pl.BlockSpec((1, qn, BRANCH_W), lambda bi, qi: (bi, qi, 0)),
        out_shape=jax.ShapeDtypeStruct((b, s, BRANCH_W), BF16),
        scratch_shapes=[pltpu.VMEM((VALUE_SLOT, s), BF16), pltpu.VMEM((VALUE_SLOT, s), BF16),
                        pltpu.VMEM((nsel, lanes), F32), pltpu.VMEM((KEY_TILE, lanes), F32),
                        pltpu.VMEM((ncp, lanes), F32), pltpu.VMEM((1, lanes), F32),
                        pltpu.VMEM((VALUE_SLOT, lanes), F32), pltpu.SMEM((s // KEY_TILE,), jnp.int32)],
        compiler_params=_params(("parallel", "arbitrary")),
        name="nsa_attention",
    )(a, f, slopes, kc, vct, jnp.asarray(ov.T), a, a)


def _layer_norm(r, g, b):
    mu = jnp.mean(r, axis=-1, keepdims=True)
    c = r - mu
    var = jnp.mean(c * c, axis=-1, keepdims=True)
    return c * lax.rsqrt(var + LN_EPS) * g + b


def _merge_kernel(h_ref, o0_ref, o1_ref, o2_ref, o3_ref, o4_ref, wg_ref, bg_ref, wbr_ref, wout_ref, g_ref, b_ref,
                  out_ref):
    h = h_ref[...]
    hb = h.astype(BF16)
    merged = jnp.zeros(h.shape, F32)
    for i, o_ref in enumerate((o0_ref, o1_ref, o2_ref, o3_ref, o4_ref)):
        gate = _sigmoid(_dot(hb, wg_ref[i]) + bg_ref[i])
        merged = merged + gate * _dot(o_ref[...], wbr_ref[i])
    y = _dot(merged.astype(BF16), wout_ref[...])
    out_ref[...] = _layer_norm(DEEPNORM_ALPHA * h + y, g_ref[...], b_ref[...])


def gated_merge(h, branches, w_gate, b_gate, w_br, w_out, ln_g, ln_b, bm):
    t, d = h.shape
    nb, bw = len(branches), branches[0].shape[1]
    row = lambda i: (i, 0)
    return pl.pallas_call(
        _merge_kernel,
        grid=(t // bm,),
        in_specs=[pl.BlockSpec((bm, d), row)] + [pl.BlockSpec((bm, bw), row)] * nb
                 + [_resident((nb, d, d)), _resident((nb, 1, d)), _resident((nb, bw, d)), _resident((d, d)),
                    _resident((1, d)), _resident((1, d))],
        out_specs=pl.BlockSpec((bm, d), row),
        out_shape=jax.ShapeDtypeStruct((t, d), F32),
        compiler_params=_params(("parallel",)),
        name="gated_merge",
    )(h, *branches, w_gate.astype(BF16), b_gate.reshape(nb, 1, d), w_br.astype(BF16), w_out.astype(BF16),
      ln_g.reshape(1, d), ln_b.reshape(1, d))


ROUTER_LANES = 128


def _moe_kernel(h_ref, wr_ref, br_ref, wup_ref, wdn_ref, g_ref, b_ref, out_ref, hid_ref):
    h = h_ref[...]
    hb = h.astype(BF16)
    bm = h.shape[0]
    logits = _dot_exact(h, wr_ref[...]) + br_ref[...]
    lane = lax.broadcasted_iota(jnp.int32, (bm, ROUTER_LANES), 1)
    lane_f = lane.astype(F32)
    is_g = lane < N_GROUPS
    glog = jnp.where(is_g, logits, NEG)
    gmax = jnp.max(glog, axis=-1, keepdims=True)
    g_sel = jnp.min(jnp.where(glog == gmax, lane_f, 1e9), axis=-1, keepdims=True)
    pg_sel = 1.0 / jnp.sum(jnp.where(is_g, jnp.exp(glog - gmax), 0.0), axis=-1, keepdims=True)
    lo = N_GROUPS + g_sel * EXPERTS_PER_GROUP
    in_grp = jnp.logical_and(lane_f >= lo, lane_f < lo + EXPERTS_PER_GROUP)
    elog = jnp.where(in_grp, logits, NEG)
    emax = jnp.max(elog, axis=-1, keepdims=True)
    ee = jnp.where(in_grp, jnp.exp(elog - emax), 0.0)
    pe = ee / jnp.sum(ee, axis=-1, keepdims=True)
    pe_m = jnp.where(in_grp, pe, -1.0)
    v1 = jnp.max(pe_m, axis=-1, keepdims=True)
    i1 = jnp.min(jnp.where(pe_m == v1, lane_f, 1e9), axis=-1, keepdims=True)
    pe_m2 = jnp.where(lane_f == i1, -1.0, pe_m)
    v2 = jnp.max(pe_m2, axis=-1, keepdims=True)
    i2 = jnp.min(jnp.where(pe_m2 == v2, lane_f, 1e9), axis=-1, keepdims=True)
    norm = pg_sel / (v1 + v2)
    gate = jnp.where(lane_f == i1, v1 * norm, jnp.where(lane_f == i2, v2 * norm, 0.0))
    for e in range(N_EXPERTS):
        au = _dot(hb, wup_ref[e])
        a, u = au[:, :D_EXPERT], au[:, D_EXPERT:]
        w_e = jnp.sum(jnp.where(lane == N_GROUPS + e, gate, 0.0), axis=-1, keepdims=True)
        hid_ref[:, e * D_EXPERT:(e + 1) * D_EXPERT] = (w_e * (a * _sigmoid(a) * u)).astype(BF16)
    y = _dot(hid_ref[...], wdn_ref[...])
    out_ref[...] = _layer_norm(DEEPNORM_ALPHA * h + y, g_ref[...], b_ref[...])


def hierarchical_moe(h, w_rg, b_rg, w_re, b_re, w_up, w_down, ln_g, ln_b, bm):
    t, d = h.shape
    ne = N_EXPERTS
    w_r = jnp.concatenate([w_rg, w_re.transpose(1, 0, 2).reshape(d, ne)], axis=1)
    w_r = jnp.pad(w_r, ((0, 0), (0, ROUTER_LANES - w_r.shape[1])))
    b_r = jnp.pad(jnp.concatenate([b_rg, b_re.reshape(ne)]), (0, ROUTER_LANES - N_GROUPS - ne)).reshape(1, -1)
    row = lambda i: (i, 0)
    return pl.pallas_call(
        _moe_kernel,
        grid=(t // bm,),
        in_specs=[pl.BlockSpec((bm, d), row), _resident((d, ROUTER_LANES)), _resident((1, ROUTER_LANES)),
                  _resident((ne, d, 2 * D_EXPERT)), _resident((ne * D_EXPERT, d)), _resident((1, d)),
                  _resident((1, d))],
        out_specs=pl.BlockSpec((bm, d), row),
        out_shape=jax.ShapeDtypeStruct((t, d), F32),
        scratch_shapes=[pltpu.VMEM((bm, ne * D_EXPERT), BF16)],
        compiler_params=_params(("parallel",)),
        name="hierarchical_moe",
    )(h, w_r, b_r, w_up.astype(BF16), w_down.reshape(ne * D_EXPERT, d).astype(BF16),
      ln_g.reshape(1, d), ln_b.reshape(1, d))


def _pad_in_weight(w_in):
    d = w_in.shape[0]
    sizes = (768, 768, MLA_Q_RANK, MLA_KV_RANK, MLA_ROPE, 256, 384, 12, 256)
    offs = np.concatenate([[0], np.cumsum(sizes)])
    sb, moba, c_q, c_kv, k_rope, nsa_q, nsa_kv, nsa_g, mem_q = (w_in[:, offs[i]:offs[i + 1]] for i in range(len(sizes)))
    z = lambda n: jnp.zeros((d, n), w_in.dtype)
    q_scale = HEAD_DIM ** -0.5 * LOG2E
    scale_q = lambda qkv: jnp.concatenate([qkv[:, :BRANCH_W] * q_scale, qkv[:, BRANCH_W:]], axis=1)
    return jnp.concatenate([scale_q(sb), scale_q(moba), c_q, c_kv, k_rope, _rotate_half_cols(k_rope), z(64),
                            nsa_kv[:, :128], nsa_g, z(116), nsa_kv[:, 128:], nsa_q * q_scale, mem_q * q_scale], axis=1)


def kernel(x, mem, w_in, g_cq, g_ckv, w_uq, w_ukv, nsa_pe, w_phi_k1, w_phi_k2, w_phi_v1, w_phi_v2, w_mem_kv, w_br,
           w_gate, b_gate, w_out, ln1_g, ln1_b, w_rg, b_rg, w_re, b_re, w_up, w_down, ln2_g, ln2_b):
    b, s_len, d = x.shape
    s = -(-s_len // MOBA_BLOCK) * MOBA_BLOCK
    t = b * s
    n_mem = mem.shape[1]
    h = jnp.pad(x, ((0, 0), (0, s - s_len), (0, 0))).reshape(t, d)
    for l in range(w_in.shape[0]):
        a, f = in_projection(h, _pad_in_weight(w_in[l]).astype(BF16), bm=512)
        a3, f3 = a.reshape(b, s, IN_PAD), f.reshape(b, s, F32_W)
        q, kv, kr = mla_project(f, g_cq[l], g_ckv[l], w_uq[l], w_ukv[l], s, bm=512)
        kc, vct = nsa_compress(f3, nsa_pe[l], w_phi_k1[l], w_phi_k2[l], w_phi_v1[l], w_phi_v2[l])
        mkv = matmul(mem.reshape(b * n_mem, d), w_mem_kv[l].astype(BF16), bm=n_mem, out_dtype=BF16)
        branches = [sb_attention(a3),
                    moba_attention(a3),
                    mla_attention(q.reshape(b, s, -1), kv.reshape(b, s, -1), kr.reshape(b, s, -1)),
                    nsa_attention(a3, f3, kc, vct),
                    cross_attention(a3, mkv.reshape(b, n_mem, -1), qb=512)]
        h = gated_merge(h, [o.reshape(t, BRANCH_W) for o in branches], w_gate[l], b_gate[l], w_br[l], w_out[l],
                        ln1_g[l], ln1_b[l], bm=256)
        h = hierarchical_moe(h, w_rg[l], b_rg[l], w_re[l], b_re[l], w_up[l], w_down[l], ln2_g[l], ln2_b[l], bm=256)
    return h.reshape(b, s, d)[:, :s_len]
```

```python
import functools

import jax
import jax.numpy as jnp
import numpy as np
from jax import lax
from jax.experimental import pallas as pl
from jax.experimental.pallas import tpu as pltpu

DEPTH = 4
HEAD_DIM = 64
N_HEADS = 4
BRANCH_W = N_HEADS * HEAD_DIM
N_BRANCHES = 5
MOBA_BLOCK = 256
MOBA_TOPK = 3
MLA_Q_RANK = 256
MLA_KV_RANK = 128
MLA_NOPE = 64
MLA_ROPE = 32
MLA_V = 64
ROPE_THETA = 10000.0
NSA_CMP_LEN = 32
NSA_CMP_STRIDE = 16
NSA_SEL_LEN = 64
NSA_TOPN = 16
NSA_WINDOW = 512
NSA_PHI_HIDDEN = 128
N_GROUPS = 4
EXPERTS_PER_GROUP = 4
N_EXPERTS = N_GROUPS * EXPERTS_PER_GROUP
D_EXPERT = 256
DEEPNORM_ALPHA = (2.0 * DEPTH) ** 0.25
LN_EPS = 1e-5
RMS_EPS = 1e-6
NEG = -1e30
BIG = 1e30

LANES = 128
BF16_SUBLANES = 16
VALUE_SLOT = HEAD_DIM + BF16_SUBLANES
QUERY_BLOCK = 256
KEY_TILE = 256
SOFTMAX_KEY_TILE = 1024
SB_KEY_TILE = 512
LOG2E = 1.4426950408889634
VMEM_LIMIT_BYTES = 56 * 1024 * 1024

F32 = jnp.float32
BF16 = jnp.bfloat16
HIGHEST = lax.Precision.HIGHEST

COL_SB = 0
COL_MOBA = 768
COL_MLA = 1536
COL_NSA_CMP = 2048
COL_NSA_G = 2176
COL_NSA_SLC = 2304
COL_NSA_WIN = 2432
COL_NSA_Q = 2560
COL_MEM_Q = 2816
IN_PAD = 3072
COL_F32 = COL_MLA
F32_W = COL_NSA_SLC - COL_MLA


def _params(semantics):
    return pltpu.CompilerParams(dimension_semantics=semantics, vmem_limit_bytes=VMEM_LIMIT_BYTES)


def _dot(a, b):
    return jnp.dot(a, b, preferred_element_type=F32)


def _dot_exact(a, b):
    return jnp.dot(a, b, preferred_element_type=F32, precision=HIGHEST)


def _sigmoid(x):
    return 1.0 / (1.0 + jnp.exp(-x))


def _resident(shape):
    zeros = (0,) * len(shape)
    return pl.BlockSpec(shape, lambda *_: zeros, pipeline_mode=pl.Buffered(1))


def _in_proj_kernel(x_ref, w_ref, o_ref, f_ref):
    y = _dot(x_ref[...].astype(BF16), w_ref[...])
    o_ref[...] = y.astype(BF16)
    f_ref[...] = y[:, COL_F32:COL_F32 + F32_W]


def in_projection(h, w, bm):
    t, d = h.shape
    n = w.shape[1]
    return pl.pallas_call(
        _in_proj_kernel,
        grid=(t // bm,),
        in_specs=[pl.BlockSpec((bm, d), lambda i: (i, 0)), _resident((d, n))],
        out_specs=[pl.BlockSpec((bm, n), lambda i: (i, 0)), pl.BlockSpec((bm, F32_W), lambda i: (i, 0))],
        out_shape=[jax.ShapeDtypeStruct((t, n), BF16), jax.ShapeDtypeStruct((t, F32_W), F32)],
        compiler_params=_params(("parallel",)),
        name="in_projection",
    )(h, w)


def _mm_kernel(x_ref, w_ref, o_ref):
    o_ref[...] = _dot(x_ref[...].astype(BF16), w_ref[...]).astype(o_ref.dtype)


def matmul(x, w, *, bm, out_dtype=F32):
    m, k = x.shape
    n = w.shape[1]
    return pl.pallas_call(
        _mm_kernel,
        grid=(m // bm,),
        in_specs=[pl.BlockSpec((bm, k), lambda i: (i, 0)), _resident((k, n))],
        out_specs=pl.BlockSpec((bm, n), lambda i: (i, 0)),
        out_shape=jax.ShapeDtypeStruct((m, n), out_dtype),
        compiler_params=_params(("parallel",)),
        name="matmul",
    )(x, w)


def _softmax_tile(s, v_slot, m, acc, shift=None):
    top = jnp.max(s, axis=0, keepdims=True)
    m_new = jnp.maximum(m, top if shift is None else top - shift)
    p = jnp.exp2(s - (m_new if shift is None else m_new + shift))
    return m_new, jnp.exp2(m - m_new) * acc + _dot(v_slot, p.astype(BF16))


def _softmax_init(q):
    return (jnp.full((1, q), NEG, F32), jnp.zeros((VALUE_SLOT, q), F32))


def _softmax_result(acc):
    return acc[:HEAD_DIM] / acc[HEAD_DIM:HEAD_DIM + 1]


def _causal_softmax(qi, qb, kt, heads, scores, values, finish, finish_last, raw_ref):
    def tile(j, state, fin):
        return tuple(_softmax_tile(fin[h][0], values(j, h), *state[h], shift=fin[h][1]) for h in range(heads))

    def prefetch(j, slot):
        for h, raw in enumerate(scores(j)):
            raw_ref[slot, h] = raw

    def held(slot):
        return tuple(raw_ref[slot, h] for h in range(heads))

    def pair(i, state):
        prefetch(2 * i + 1, 1)
        state = tile(2 * i, state, finish(2 * i, held(0)))
        prefetch(2 * i + 2, 0)
        return tile(2 * i + 1, state, finish(2 * i + 1, held(1)))

    def odd_tail(state):
        prefetch(last, 1)
        state = tile(last - 1, state, finish(last - 1, held(0)))
        return tile(last, state, finish_last(last, held(1)))

    def even_tail(state):
        return tile(last, state, finish_last(last, held(0)))

    last = lax.div(qi * qb, kt)
    prefetch(0, 0)
    state = tuple(_softmax_init(qb) for _ in range(heads))
    state = lax.fori_loop(0, lax.div(last, 2), pair, state)
    state = lax.cond(lax.rem(last, 2) == 1, odd_tail, even_tail, state)
    return [_softmax_result(acc) for _, acc in state]


def _transposed(x):
    return x.astype(F32).T


def _pair_rows(qt_all, h):
    g = qt_all[(h // 2) * LANES:(h // 2 + 1) * LANES]
    row = lax.broadcasted_iota(jnp.int32, g.shape, 0)
    keep = (row >= HEAD_DIM) if h % 2 else (row < HEAD_DIM)
    return jnp.where(keep, g, 0.0).astype(BF16)


def _pair_cols(h):
    return slice((h // 2) * LANES, (h // 2 + 1) * LANES)


def _head_rows(h):
    return slice(h * HEAD_DIM, (h + 1) * HEAD_DIM)


def _fill_transposed(src_ref, dst_ref, kb, extra=None):
    def chunk(c, _):
        r0 = pl.multiple_of(c * kb, kb)
        x = src_ref[0, pl.ds(r0, kb), :].astype(F32)
        dst_ref[:, pl.ds(r0, kb)] = x.T.astype(BF16)
        if extra is not None:
            extra(c, x)
        return 0
    lax.fori_loop(0, src_ref.shape[1] // kb, chunk, 0)


def _fill_value_slots(src_ref, dst_ref, kb, first_rows):
    ones = jnp.ones((BF16_SUBLANES, kb), BF16)

    def chunk(c, _):
        r0 = pl.multiple_of(c * kb, kb)
        xt = src_ref[0, pl.ds(r0, kb), :].astype(F32).T
        for h, r in enumerate(first_rows):
            dst_ref[h * VALUE_SLOT:h * VALUE_SLOT + HEAD_DIM, pl.ds(r0, kb)] = xt[r:r + HEAD_DIM].astype(BF16)
            dst_ref[h * VALUE_SLOT + HEAD_DIM:(h + 1) * VALUE_SLOT, pl.ds(r0, kb)] = ones
        return 0
    lax.fori_loop(0, src_ref.shape[1] // kb, chunk, 0)


def _value_slot(h):
    return slice(h * VALUE_SLOT, (h + 1) * VALUE_SLOT)


def _store_heads(o_ref, outs):
    o_ref[0] = jnp.concatenate(outs, axis=0).T.astype(o_ref.dtype)


def _qkv_specs(s, kb, col):
    c = col // BRANCH_W
    return [pl.BlockSpec((1, kb, BRANCH_W), lambda bi, qi: (bi, qi, c)),
            pl.BlockSpec((1, s, BRANCH_W), lambda bi, qi: (bi, 0, c + 1)),
            pl.BlockSpec((1, s, BRANCH_W), lambda bi, qi: (bi, 0, c + 2))]


def _sb_kernel(q_ref, k_ref, v_ref, tri_ref, o_ref, vt_ref, raw_ref, *, kt):
    qi = pl.program_id(1)
    qb = q_ref.shape[1]
    kb = KEY_TILE
    pieces = kt // kb

    @pl.when(qi == 0)
    def _():
        _fill_transposed(v_ref, vt_ref, kb)

    qt_all = _transposed(q_ref[0])
    qts = [_pair_rows(qt_all, h) for h in range(N_HEADS)]
    tri = tri_ref[...]
    rel = lax.broadcasted_iota(jnp.int32, (kb, qb), 0) - lax.broadcasted_iota(jnp.int32, (kb, qb), 1)
    order = [(u, h) for u in reversed(range(pieces)) for h in range(N_HEADS)]

    def prefetch(j, slot):
        for u, h in order:
            raw_ref[slot, h, u * kb:(u + 1) * kb] = _dot(
                k_ref[0, pl.ds(pl.multiple_of(j * kt + u * kb, kb), kb), _pair_cols(h)], qts[h])

    def tile(j, state, slot, last):
        k0 = {u: pl.multiple_of(j * kt + u * kb, kb) for u in range(pieces)}
        log_beta, later, col_sum, past = {}, {}, {}, {}
        for u, h in order:
            z = raw_ref[slot, h, u * kb:(u + 1) * kb]
            nz = -z
            soft = jnp.log(1.0 + jnp.exp2(jnp.minimum(z, nz))) * LOG2E
            log_keep = jnp.minimum(nz, 0.0) - soft
            log_beta[u, h] = log_keep + z
            if last:
                past[u] = rel < qi * qb - k0[u]
                log_keep = jnp.where(past[u], log_keep, 0.0)
            sums = _dot(tri, log_keep.astype(BF16))
            later[u, h] = sums[:kb]
            col_sum[u, h] = sums[kb:kb + 1]
        state = list(state)
        for u, h in order:
            carry, acc = state[h]
            a = jnp.exp2(log_beta[u, h] + later[u, h])
            if last:
                a = jnp.where(past[u], a, 0.0)
            part = _dot(vt_ref[_head_rows(h), pl.ds(k0[u], kb)], a.astype(BF16))
            state[h] = (carry + col_sum[u, h], acc + jnp.exp2(carry) * part)
        return tuple(state)

    def pair(i, state):
        first = last - 1 - 2 * i
        prefetch(first - 1, 0)
        state = tile(first, state, 1, False)
        prefetch(jnp.maximum(first - 2, 0), 1)
        return tile(first - 1, state, 0, False)

    last = lax.div(qi * qb, kt)
    prefetch(last, 0)
    prefetch(jnp.maximum(last - 1, 0), 1)
    state = tuple((jnp.zeros((1, qb), F32), jnp.zeros((HEAD_DIM, qb), F32)) for _ in range(N_HEADS))
    state = tile(last, state, 0, True)
    state = lax.fori_loop(0, lax.div(last, 2), pair, state)
    state = lax.cond(lax.rem(last, 2) == 1, lambda st: tile(0, st, 1, False), lambda st: st, state)
    _store_heads(o_ref, [st[1] for st in state])


def sb_attention(a):
    b, s, _ = a.shape
    kb, qb = KEY_TILE, QUERY_BLOCK
    tri = jnp.asarray(np.concatenate([np.triu(np.ones((kb, kb), np.float32), 1), np.ones((BF16_SUBLANES, kb), np.float32)]),
                      BF16)
    return pl.pallas_call(
        functools.partial(_sb_kernel, kt=min(SB_KEY_TILE, s)),
        grid=(b, s // qb),
        in_specs=_qkv_specs(s, qb, COL_SB) + [_resident(tri.shape)],
        out_specs=pl.BlockSpec((1, qb, BRANCH_W), lambda bi, qi: (bi, qi, 0)),
        out_shape=jax.ShapeDtypeStruct((b, s, BRANCH_W), BF16),
        scratch_shapes=[pltpu.VMEM((BRANCH_W, s), BF16), pltpu.VMEM((2, N_HEADS, min(SB_KEY_TILE, s), qb), F32)],
        compiler_params=_params(("parallel", "arbitrary")),
        name="sb_attention",
    )(a, a, a, tri)


def _alibi_slopes(n):
    return np.power(2.0, -8.0 * np.arange(1, n + 1, dtype=np.float64) / n).astype(np.float32)


def _select_top(score, ids, count, floor):
    def step(_, c):
        score, sel = c
        mx = jnp.max(score, axis=0, keepdims=True)
        idx = jnp.min(jnp.where(score == mx, ids, 1e9), axis=0, keepdims=True)
        pick = ids == idx
        sel = jnp.where(jnp.logical_and(pick, mx > floor), 1.0, sel)
        score = jnp.where(pick, -jnp.inf, score)
        return score, sel
    return lax.fori_loop(0, count, step, (score, jnp.zeros_like(score)))[1]


def _moba_kernel(slope_ref, q_ref, k_ref, v_ref, o_ref, vt_ref, km_ref, sel_ref, raw_ref, *, kt, topk):
    qi = pl.program_id(1)
    kb = q_ref.shape[1]
    nblk = km_ref.shape[0]
    per_tile = kt // kb

    @pl.when(qi == 0)
    def _():
        def key_mean(c, _):
            km_ref[pl.ds(c, 1), :] = jnp.mean(k_ref[0, pl.ds(pl.multiple_of(c * kb, kb), kb), :].astype(F32),
                                              axis=0, keepdims=True)
            return 0
        _fill_value_slots(v_ref, vt_ref, kb, [h * HEAD_DIM for h in range(N_HEADS)])
        lax.fori_loop(0, nblk, key_mean, 0)

    qt_all = _transposed(q_ref[0])
    qts = [_pair_rows(qt_all, h) for h in range(N_HEADS)]
    ids = lax.broadcasted_iota(jnp.int32, (nblk, kb), 0)
    row = lax.broadcasted_iota(jnp.int32, qt_all.shape, 0)
    km = km_ref[...]
    for h in range(N_HEADS):
        in_head = jnp.logical_and(row >= h * HEAD_DIM, row < (h + 1) * HEAD_DIM)
        gscore = _dot_exact(km, jnp.where(in_head, qt_all, 0.0))
        gscore = jnp.where(ids < qi, gscore, NEG)
        sel_ref[h] = jnp.where(_select_top(gscore, ids.astype(F32), topk, 0.5 * NEG) > 0.5, 0.0, NEG)

    rk = lax.broadcasted_iota(jnp.int32, (kt, kb), 0)
    rel = rk - lax.broadcasted_iota(jnp.int32, (kt, kb), 1)
    bias = [slope_ref[h] * rel.astype(F32) for h in range(N_HEADS)]

    def scores(j):
        k0 = pl.multiple_of(j * kt, kt)
        return tuple(_dot(k_ref[0, pl.ds(k0, kt), _pair_cols(h)], qts[h]) for h in range(N_HEADS))

    def values(j, h):
        return vt_ref[_value_slot(h), pl.ds(pl.multiple_of(j * kt, kt), kt)]

    def unselected(j, h):
        rows = [jnp.broadcast_to(sel_ref[h, pl.ds(j * per_tile + r, 1), :], (kb, kb)) for r in range(per_tile)]
        return jnp.concatenate(rows, axis=0)

    def finish(j, raw):
        off = (qi * kb - j * kt).astype(F32)
        return tuple((raw[h] + bias[h] + unselected(j, h), slope_ref[h] * off) for h in range(N_HEADS))

    def finish_last(j, raw):
        d0 = qi * kb - j * kt
        own_causal = jnp.logical_and(rk >= d0, rel <= d0)
        out = []
        for h in range(N_HEADS):
            s = raw[h] + bias[h]
            out.append((jnp.where(own_causal, s, s + unselected(j, h)), slope_ref[h] * d0.astype(F32)))
        return tuple(out)

    _store_heads(o_ref, _causal_softmax(qi, kb, kt, N_HEADS, scores, values, finish, finish_last, raw_ref))


def moba_attention(a):
    b, s, _ = a.shape
    kb = MOBA_BLOCK
    nblk = s // kb
    topk = min(MOBA_TOPK, nblk - 1)
    slopes = jnp.asarray(_alibi_slopes(N_HEADS) * np.float32(LOG2E))
    c = COL_MOBA // BRANCH_W
    return pl.pallas_call(
        functools.partial(_moba_kernel, kt=min(SOFTMAX_KEY_TILE, s), topk=topk),
        grid_spec=pltpu.PrefetchScalarGridSpec(
            num_scalar_prefetch=1,
            grid=(b, nblk),
            in_specs=[pl.BlockSpec((1, kb, BRANCH_W), lambda bi, qi, sl: (bi, qi, c)),
                      pl.BlockSpec((1, s, BRANCH_W), lambda bi, qi, sl: (bi, 0, c + 1)),
                      pl.BlockSpec((1, s, BRANCH_W), lambda bi, qi, sl: (bi, 0, c + 2))],
            out_specs=pl.BlockSpec((1, kb, BRANCH_W), lambda bi, qi, sl: (bi, qi, 0)),
            scratch_shapes=[pltpu.VMEM((N_HEADS * VALUE_SLOT, s), BF16),
                            pltpu.VMEM((nblk, BRANCH_W), F32),
                            pltpu.VMEM((N_HEADS, nblk, kb), F32),
                            pltpu.VMEM((2, N_HEADS, min(SOFTMAX_KEY_TILE, s), kb), F32)]),
        out_shape=jax.ShapeDtypeStruct((b, s, BRANCH_W), BF16),
        compiler_params=_params(("parallel", "arbitrary")),
        name="moba_attention",
    )(slopes, a, a, a)


MLA_SLOT = 128


def _rms(x, g):
    return x * lax.rsqrt(jnp.mean(x * x, axis=-1, keepdims=True) + RMS_EPS) * g


def _mla_proj_kernel(x_ref, gq_ref, gkv_ref, wq_ref, wqr_ref, wkv_ref, cq_ref, sq_ref, ck_ref, sk_ref,
                     q_ref, kv_ref, kr_ref):
    x = x_ref[...]
    c_q = _rms(x[:, :MLA_Q_RANK], gq_ref[...]).astype(BF16)
    c_kv = _rms(x[:, MLA_Q_RANK:MLA_Q_RANK + MLA_KV_RANK], gkv_ref[...]).astype(BF16)
    q_ref[...] = (_dot(c_q, wq_ref[...]) * cq_ref[...] + _dot(c_q, wqr_ref[...]) * sq_ref[...]).astype(q_ref.dtype)
    kv_ref[...] = _dot(c_kv, wkv_ref[...]).astype(kv_ref.dtype)
    tail = x[:, MLA_Q_RANK + MLA_KV_RANK:]
    rot = pltpu.roll(tail, LANES - MLA_ROPE, axis=1)
    kr_ref[...] = (tail * ck_ref[...] + rot * sk_ref[...]).astype(kr_ref.dtype)


def _rope_tables(s_len):
    half = MLA_ROPE // 2
    freqs = jnp.power(ROPE_THETA, -jnp.arange(half, dtype=F32) / half)
    ang = jnp.arange(s_len).astype(F32)[:, None] * freqs
    cos = jnp.concatenate([jnp.cos(ang)] * 2, axis=-1)
    sin = jnp.concatenate([jnp.sin(ang)] * 2, axis=-1)
    zk = jnp.zeros((s_len, LANES - MLA_ROPE), F32)
    zq = jnp.zeros((s_len, MLA_SLOT - MLA_NOPE - MLA_ROPE), F32)
    cq = jnp.tile(jnp.concatenate([jnp.ones((s_len, MLA_NOPE), F32), cos, zq], axis=-1), (1, N_HEADS))
    sq = jnp.tile(jnp.concatenate([jnp.zeros((s_len, MLA_NOPE), F32), sin, zq], axis=-1), (1, N_HEADS))
    return cq, sq, jnp.concatenate([cos, zk], axis=-1), jnp.concatenate([sin, zk], axis=-1)


def _rotate_half_cols(w):
    half = w.shape[-1] // 2
    return jnp.concatenate([-w[..., half:], w[..., :half]], axis=-1)


def mla_project(f, g_cq, g_ckv, w_uq, w_ukv, s_len, bm):
    t = f.shape[0]
    wq = w_uq.reshape(MLA_Q_RANK, N_HEADS, MLA_NOPE + MLA_ROPE)
    pad = jnp.zeros((MLA_Q_RANK, N_HEADS, MLA_SLOT - MLA_NOPE - MLA_ROPE), wq.dtype)
    wq_s = jnp.concatenate([wq, pad], axis=-1).reshape(MLA_Q_RANK, -1)
    wqr_s = jnp.concatenate([jnp.zeros_like(wq[..., :MLA_NOPE]), _rotate_half_cols(wq[..., MLA_NOPE:]), pad],
                            axis=-1).reshape(MLA_Q_RANK, -1)
    cq, sq, ck, sk = _rope_tables(s_len)
    q_scale = (MLA_NOPE + MLA_ROPE) ** -0.5 * LOG2E
    nrow = s_len // bm
    row = lambda i: (i, 0)
    pos = lambda i: (i % nrow, 0)
    qw = N_HEADS * MLA_SLOT
    kvw = N_HEADS * (MLA_NOPE + MLA_V)
    return pl.pallas_call(
        _mla_proj_kernel,
        grid=(t // bm,),
        in_specs=[pl.BlockSpec((bm, 512), row),
                  _resident((1, MLA_Q_RANK)), _resident((1, MLA_KV_RANK)),
                  _resident((MLA_Q_RANK, qw)), _resident((MLA_Q_RANK, qw)), _resident((MLA_KV_RANK, kvw)),
                  pl.BlockSpec((bm, qw), pos), pl.BlockSpec((bm, qw), pos),
                  pl.BlockSpec((bm, LANES), pos), pl.BlockSpec((bm, LANES), pos)],
        out_specs=[pl.BlockSpec((bm, qw), row), pl.BlockSpec((bm, kvw), row), pl.BlockSpec((bm, LANES), row)],
        out_shape=[jax.ShapeDtypeStruct((t, qw), BF16), jax.ShapeDtypeStruct((t, kvw), BF16),
                   jax.ShapeDtypeStruct((t, LANES), BF16)],
        compiler_params=_params(("parallel",)),
        name="mla_project",
    )(f, g_cq.reshape(1, -1), g_ckv.reshape(1, -1), (wq_s * q_scale).astype(BF16), (wqr_s * q_scale).astype(BF16),
      w_ukv.astype(BF16), cq, sq, ck, sk)


def _mla_kernel(q_ref, kv_ref, kr_ref, o_ref, vt_ref, raw_ref, *, kt):
    qi = pl.program_id(1)
    qb = q_ref.shape[1]

    @pl.when(qi == 0)
    def _():
        _fill_value_slots(kv_ref, vt_ref, KEY_TILE, [h * LANES + MLA_NOPE for h in range(N_HEADS)])

    qt_all = _transposed(q_ref[0])
    qts = []
    for h in range(N_HEADS):
        g = qt_all[h * MLA_SLOT:(h + 1) * MLA_SLOT]
        row = lax.broadcasted_iota(jnp.int32, g.shape, 0)
        qts.append(jnp.concatenate([jnp.where(row < MLA_NOPE, g, 0.0), g[MLA_NOPE:], jnp.zeros_like(g[MLA_NOPE:])],
                                   axis=0).astype(BF16))
    rel = lax.broadcasted_iota(jnp.int32, (kt, qb), 0) - lax.broadcasted_iota(jnp.int32, (kt, qb), 1)

    def scores(j):
        k0 = pl.multiple_of(j * kt, kt)
        kr = kr_ref[0, pl.ds(k0, kt), :]
        return tuple(_dot(jnp.concatenate([kv_ref[0, pl.ds(k0, kt), h * LANES:(h + 1) * LANES], kr], axis=1), qts[h])
                     for h in range(N_HEADS))

    def values(j, h):
        return vt_ref[_value_slot(h), pl.ds(pl.multiple_of(j * kt, kt), kt)]

    def finish_last(j, raw):
        causal = rel <= qi * qb - j * kt
        return tuple((jnp.where(causal, s, NEG), None) for s in raw)

    outs = _causal_softmax(qi, qb, kt, N_HEADS, scores, values,
                           lambda j, raw: tuple((s, None) for s in raw), finish_last, raw_ref)
    _store_heads(o_ref, outs)


def mla_attention(q, kv, kr):
    b, s, qw = q.shape
    kb = QUERY_BLOCK
    return pl.pallas_call(
        functools.partial(_mla_kernel, kt=min(SOFTMAX_KEY_TILE, s)),
        grid=(b, s // kb),
        in_specs=[pl.BlockSpec((1, kb, qw), lambda bi, qi: (bi, qi, 0)),
                  pl.BlockSpec((1, s, kv.shape[2]), lambda bi, qi: (bi, 0, 0)),
                  pl.BlockSpec((1, s, LANES), lambda bi, qi: (bi, 0, 0))],
        out_specs=pl.BlockSpec((1, kb, BRANCH_W), lambda bi, qi: (bi, qi, 0)),
        out_shape=jax.ShapeDtypeStruct((b, s, BRANCH_W), BF16),
        scratch_shapes=[pltpu.VMEM((N_HEADS * VALUE_SLOT, s), BF16),
                        pltpu.VMEM((2, N_HEADS, min(SOFTMAX_KEY_TILE, s), kb), F32)],
        compiler_params=_params(("parallel", "arbitrary")),
        name="mla_attention",
    )(q, kv, kr)


def _cross_kernel(q_ref, kv_ref, o_ref):
    qt_all = _transposed(q_ref[0])
    kv = kv_ref[0]
    vt = _transposed(kv[:, BRANCH_W:]).astype(BF16)
    outs = []
    for h in range(N_HEADS):
        s = _dot(kv[:, _pair_cols(h)], _pair_rows(qt_all, h))
        p = jnp.exp2(s - jnp.max(s, axis=0, keepdims=True))
        outs.append(_dot(vt[_head_rows(h)], p.astype(BF16)) / jnp.sum(p, axis=0, keepdims=True))
    _store_heads(o_ref, outs)


def cross_attention(a, mkv, qb):
    b, s, _ = a.shape
    n = mkv.shape[1]
    c = COL_MEM_Q // BRANCH_W
    return pl.pallas_call(
        _cross_kernel,
        grid=(b, s // qb),
        in_specs=[pl.BlockSpec((1, qb, BRANCH_W), lambda bi, qi: (bi, qi, c)),
                  pl.BlockSpec((1, n, 2 * BRANCH_W), lambda bi, qi: (bi, 0, 0))],
        out_specs=pl.BlockSpec((1, qb, BRANCH_W), lambda bi, qi: (bi, qi, 0)),
        out_shape=jax.ShapeDtypeStruct((b, s, BRANCH_W), BF16),
        compiler_params=_params(("parallel", "parallel")),
        name="cross_attention",
    )(a, mkv)


def _compress_kernel(x_ref, pe_ref, w1_ref, w2_ref, kc_ref, vct_ref):
    n = kc_ref.shape[1]
    first = jnp.zeros((n, 2 * NSA_PHI_HIDDEN), F32)
    second = jnp.zeros((n, 2 * NSA_PHI_HIDDEN), F32)
    for r in range(NSA_CMP_STRIDE):
        x = x_ref[0, pl.ds(r, n, stride=NSA_CMP_STRIDE), :]
        first = first + _dot((x + pe_ref[r:r + 1]).astype(BF16), w1_ref[r])
        second = second + _dot((x + pe_ref[NSA_CMP_STRIDE + r:NSA_CMP_STRIDE + r + 1]).astype(BF16),
                               w1_ref[NSA_CMP_STRIDE + r])
    hidden = jax.nn.gelu(first + pltpu.roll(second, n - 1, axis=0))
    out = _dot(hidden.astype(BF16), w2_ref[...])
    kc_ref[0] = out.astype(BF16)
    vct_ref[0] = out.T[HEAD_DIM:].astype(BF16)


def _pair_diag(wk, wv):
    z = jnp.zeros_like(wk)
    return jnp.concatenate([jnp.concatenate([wk, z], axis=-1), jnp.concatenate([z, wv], axis=-1)], axis=-2)


def nsa_compress(f, nsa_pe, w_k1, w_k2, w_v1, w_v2):
    b, s, _ = f.shape
    n = s // NSA_CMP_STRIDE
    hd = HEAD_DIM
    w1 = _pair_diag(w_k1.reshape(NSA_CMP_LEN, hd, -1), w_v1.reshape(NSA_CMP_LEN, hd, -1)).astype(BF16)
    w2 = _pair_diag(w_k2, w_v2).astype(BF16)
    pe = jnp.concatenate([nsa_pe, nsa_pe], axis=-1)
    c = (COL_NSA_CMP - COL_F32) // LANES
    return pl.pallas_call(
        _compress_kernel,
        grid=(b,),
        in_specs=[pl.BlockSpec((1, s, LANES), lambda bi: (bi, 0, c)),
                  _resident(pe.shape), _resident(w1.shape), _resident(w2.shape)],
        out_specs=[pl.BlockSpec((1, n, LANES), lambda bi: (bi, 0, 0)),
                   pl.BlockSpec((1, hd, n), lambda bi: (bi, 0, 0))],
        out_shape=[jax.ShapeDtypeStruct((b, n, LANES), BF16), jax.ShapeDtypeStruct((b, hd, n), BF16)],
        compiler_params=_params(("parallel",)),
        name="nsa_compress",
    )(f, pe, w1, w2)


def _nsa_kernel(q_ref, g_ref, slope_ref, kc_ref, vct_ref, ovt_ref, slc_ref, win_ref, o_ref,
                vst_ref, vwt_ref, sel_ref, bias_ref, cbias_ref, m_ref, acc_ref, wanted_ref, *, topn, n_cmp):
    qi = pl.program_id(1)
    qn = q_ref.shape[1]
    kb = KEY_TILE
    lanes = N_HEADS * qn
    dv = HEAD_DIM
    q0 = qi * qn

    slope = slope_ref[...]
    ql = jnp.bitwise_and(lax.broadcasted_iota(jnp.int32, (1, lanes), 1), qn - 1)
    qpos = q0 + ql
    kid = lax.broadcasted_iota(jnp.int32, (kb, 1), 0)
    ncp = kc_ref.shape[1]
    cid = lax.broadcasted_iota(jnp.int32, (ncp, 1), 0)
    cmp_last = cid * NSA_CMP_STRIDE + (NSA_CMP_LEN - 1)
    cmp_end = jnp.where(cid < n_cmp, cmp_last, 1 << 30)

    @pl.when(qi == 0)
    def _():
        _fill_value_slots(slc_ref, vst_ref, kb, [HEAD_DIM])
        _fill_value_slots(win_ref, vwt_ref, kb, [HEAD_DIM])
        bias_ref[...] = slope * (ql - kid).astype(F32)
        cbias_ref[...] = slope * (ql - cmp_last).astype(F32)

    qt_all = _transposed(q_ref[0])
    qt = jnp.concatenate([qt_all[_head_rows(h)] for h in range(N_HEADS)], axis=1)
    qt = jnp.concatenate([qt, jnp.zeros_like(qt)], axis=0).astype(BF16)
    jd = lax.div(q0, kb)
    win_tiles = [jnp.maximum(jd - back, 0) for back in range(NSA_WINDOW // kb + 1)]

    def raw_scores(kv_ref, j):
        return _dot(kv_ref[0, pl.ds(pl.multiple_of(j * kb, kb), kb), :], qt)

    raw_c = _dot(kc_ref[0], qt)
    raw_d = raw_scores(slc_ref, jd)
    raw_w = [raw_scores(win_ref, j) for j in win_tiles]
    bias = bias_ref[...]
    causal = ql - kid >= 0

    valid = cmp_end - ql <= q0
    s = jnp.where(valid, raw_c - cbias_ref[...], NEG)
    e = jnp.where(valid, jnp.exp2(s - jnp.max(s, axis=0, keepdims=True)), 0.0)
    den = jnp.sum(e, axis=0, keepdims=True)
    p_c = e / jnp.where(den > 0.0, den, 1.0)
    o_c = _dot(vct_ref[0], p_c.astype(BF16))

    p_sum = p_c[:, 0:qn]
    for hh in range(1, N_HEADS):
        p_sum = p_sum + p_c[:, hh * qn:(hh + 1) * qn]
    imp = _dot_exact(ovt_ref[...], p_sum)
    nsel = imp.shape[0]
    sid = lax.broadcasted_iota(jnp.int32, (nsel, qn), 0)
    cur = jnp.right_shift(qpos[:, 0:qn], NSA_SEL_LEN.bit_length() - 1)
    forced = jnp.logical_or(sid == 0, sid == cur)
    score = jnp.where(forced, BIG, jnp.where(sid < cur, imp, NEG))
    sel = _select_top(score, sid.astype(F32), topn, 0.5 * NEG)
    sel_ref[...] = jnp.concatenate([jnp.where(sel > 0.5, 0.0, NEG)] * N_HEADS, axis=1)
    per_tile = kb // NSA_SEL_LEN
    for j in range(nsel // per_tile):
        wanted_ref[j] = (jnp.max(sel[j * per_tile:(j + 1) * per_tile]) > 0.5).astype(jnp.int32)

    def sel_tile(j, c, raw, own):
        rows = [jnp.broadcast_to(sel_ref[pl.ds(j * per_tile + r, 1), :], (NSA_SEL_LEN, lanes))
                for r in range(per_tile)]
        s = raw - bias + jnp.concatenate(rows, axis=0)
        if own:
            s = jnp.where(causal, s, NEG)
        shift = slope * ((jd - j) * kb).astype(F32)
        return _softmax_tile(s, vst_ref[:, pl.ds(pl.multiple_of(j * kb, kb), kb)], *c, shift=shift)

    m_ref[...], acc_ref[...] = sel_tile(jd, _softmax_init(lanes), raw_d, True)

    def maybe_sel_tile(j, _):
        @pl.when(wanted_ref[j] > 0)
        def _():
            m_ref[...], acc_ref[...] = sel_tile(j, (m_ref[...], acc_ref[...]), raw_scores(slc_ref, j), False)
        return 0

    lax.fori_loop(0, jd, maybe_sel_tile, 0)
    o_s = _softmax_result(acc_ref[...])

    tops, masked = [], []
    for back, raw in enumerate(raw_w):
        s = raw - bias
        if back == 0:
            s = jnp.where(causal, s, NEG)
        else:
            inside = NSA_WINDOW - back * kb if back * kb + kb > NSA_WINDOW else 2 * kb
            s = jnp.where(ql - kid < jnp.where(jd - back >= 0, inside, -2 * kb), s, NEG)
        masked.append(s)
        tops.append(jnp.max(s, axis=0, keepdims=True) - slope * float(back * kb))
    m_w = functools.reduce(jnp.maximum, tops)
    acc_w = jnp.zeros((VALUE_SLOT, lanes), F32)
    for back, s in enumerate(masked):
        p = jnp.exp2(s - (m_w + slope * float(back * kb)))
        acc_w = acc_w + _dot(vwt_ref[:, pl.ds(pl.multiple_of(win_tiles[back] * kb, kb), kb)], p.astype(BF16))
    o_w = _softmax_result(acc_w)

    gt = _sigmoid(g_ref[0]).T
    def gate(ci):
        return jnp.concatenate([gt[h * 3 + ci:h * 3 + ci + 1] for h in range(N_HEADS)], axis=1)
    out = gate(0) * o_c + gate(1) * o_s + gate(2) * o_w
    _store_heads(o_ref, [out[:, h * qn:(h + 1) * qn] for h in range(N_HEADS)])


def nsa_attention(a, f, kc, vct):
    b, s, _ = a.shape
    ncp = kc.shape[1]
    n_cmp = ncp - NSA_CMP_LEN // NSA_CMP_STRIDE + 1
    nsel = s // NSA_SEL_LEN
    qn = QUERY_BLOCK
    lanes = N_HEADS * qn
    cs = np.arange(ncp) * NSA_CMP_STRIDE
    ss = np.arange(nsel) * NSA_SEL_LEN
    ov = ((cs[:, None] < ss[None, :] + NSA_SEL_LEN) & (cs[:, None] + NSA_CMP_LEN > ss[None, :])
          & (np.arange(ncp)[:, None] < n_cmp)).astype(np.float32)
    slopes = jnp.asarray(np.repeat(_alibi_slopes(N_HEADS) * np.float32(LOG2E), qn)[None, :])
    per_b = lambda bi, qi: (bi, 0, 0)
    return pl.pallas_call(
        functools.partial(_nsa_kernel, topn=min(NSA_TOPN, nsel), n_cmp=n_cmp),
        grid=(b, s // qn),
        in_specs=[pl.BlockSpec((1, qn, BRANCH_W), lambda bi, qi: (bi, qi, COL_NSA_Q // BRANCH_W)),
                  pl.BlockSpec((1, qn, LANES), lambda bi, qi: (bi, qi, (COL_NSA_G - COL_F32) // LANES)),
                  _resident((1, lanes)),
                  pl.BlockSpec((1, ncp, LANES), per_b),
                  pl.BlockSpec((1, HEAD_DIM, ncp), per_b),
                  _resident((nsel, ncp)),
                  pl.BlockSpec((1, s, LANES), lambda bi, qi: (bi, 0, COL_NSA_SLC // LANES)),
                  pl.BlockSpec((1, s, LANES), lambda bi, qi: (bi, 0, COL_NSA_WIN // LANES))],
        out_specs=pl.BlockSpec((1, qn, BRANCH_W), lambda bi, qi: (bi, qi, 0)),
        out_shape=jax.ShapeDtypeStruct((b, s, BRANCH_W), BF16),
        scratch_shapes=[pltpu.VMEM((VALUE_SLOT, s), BF16), pltpu.VMEM((VALUE_SLOT, s), BF16),
                        pltpu.VMEM((nsel, lanes), F32), pltpu.VMEM((KEY_TILE, lanes), F32),
                        pltpu.VMEM((ncp, lanes), F32), pltpu.VMEM((1, lanes), F32),
                        pltpu.VMEM((VALUE_SLOT, lanes), F32), pltpu.SMEM((s // KEY_TILE,), jnp.int32)],
        compiler_params=_params(("parallel", "arbitrary")),
        name="nsa_attention",
    )(a, f, slopes, kc, vct, jnp.asarray(ov.T), a, a)


def _layer_norm(r, g, b):
    mu = jnp.mean(r, axis=-1, keepdims=True)
    c = r - mu
    var = jnp.mean(c * c, axis=-1, keepdims=True)
    return c * lax.rsqrt(var + LN_EPS) * g + b


def _merge_kernel(h_ref, o0_ref, o1_ref, o2_ref, o3_ref, o4_ref, wg_ref, bg_ref, wbr_ref, wout_ref, g_ref, b_ref,
                  out_ref):
    h = h_ref[...]
    hb = h.astype(BF16)
    merged = jnp.zeros(h.shape, F32)
    for i, o_ref in enumerate((o0_ref, o1_ref, o2_ref, o3_ref, o4_ref)):
        gate = _sigmoid(_dot(hb, wg_ref[i]) + bg_ref[i])
        merged = merged + gate * _dot(o_ref[...], wbr_ref[i])
    y = _dot(merged.astype(BF16), wout_ref[...])
    out_ref[...] = _layer_norm(DEEPNORM_ALPHA * h + y, g_ref[...], b_ref[...])


def gated_merge(h, branches, w_gate, b_gate, w_br, w_out, ln_g, ln_b, bm):
    t, d = h.shape
    nb, bw = len(branches), branches[0].shape[1]
    row = lambda i: (i, 0)
    return pl.pallas_call(
        _merge_kernel,
        grid=(t // bm,),
        in_specs=[pl.BlockSpec((bm, d), row)] + [pl.BlockSpec((bm, bw), row)] * nb
                 + [_resident((nb, d, d)), _resident((nb, 1, d)), _resident((nb, bw, d)), _resident((d, d)),
                    _resident((1, d)), _resident((1, d))],
        out_specs=pl.BlockSpec((bm, d), row),
        out_shape=jax.ShapeDtypeStruct((t, d), F32),
        compiler_params=_params(("parallel",)),
        name="gated_merge",
    )(h, *branches, w_gate.astype(BF16), b_gate.reshape(nb, 1, d), w_br.astype(BF16), w_out.astype(BF16),
      ln_g.reshape(1, d), ln_b.reshape(1, d))


ROUTER_LANES = 128


def _moe_kernel(h_ref, wr_ref, br_ref, wup_ref, wdn_ref, g_ref, b_ref, out_ref, hid_ref):
    h = h_ref[...]
    hb = h.astype(BF16)
    bm = h.shape[0]
    logits = _dot_exact(h, wr_ref[...]) + br_ref[...]
    lane = lax.broadcasted_iota(jnp.int32, (bm, ROUTER_LANES), 1)
    lane_f = lane.astype(F32)
    is_g = lane < N_GROUPS
    glog = jnp.where(is_g, logits, NEG)
    gmax = jnp.max(glog, axis=-1, keepdims=True)
    g_sel = jnp.min(jnp.where(glog == gmax, lane_f, 1e9), axis=-1, keepdims=True)
    pg_sel = 1.0 / jnp.sum(jnp.where(is_g, jnp.exp(glog - gmax), 0.0), axis=-1, keepdims=True)
    lo = N_GROUPS + g_sel * EXPERTS_PER_GROUP
    in_grp = jnp.logical_and(lane_f >= lo, lane_f < lo + EXPERTS_PER_GROUP)
    elog = jnp.where(in_grp, logits, NEG)
    emax = jnp.max(elog, axis=-1, keepdims=True)
    ee = jnp.where(in_grp, jnp.exp(elog - emax), 0.0)
    pe = ee / jnp.sum(ee, axis=-1, keepdims=True)
    pe_m = jnp.where(in_grp, pe, -1.0)
    v1 = jnp.max(pe_m, axis=-1, keepdims=True)
    i1 = jnp.min(jnp.where(pe_m == v1, lane_f, 1e9), axis=-1, keepdims=True)
    pe_m2 = jnp.where(lane_f == i1, -1.0, pe_m)
    v2 = jnp.max(pe_m2, axis=-1, keepdims=True)
    i2 = jnp.min(jnp.where(pe_m2 == v2, lane_f, 1e9), axis=-1, keepdims=True)
    norm = pg_sel / (v1 + v2)
    gate = jnp.where(lane_f == i1, v1 * norm, jnp.where(lane_f == i2, v2 * norm, 0.0))
    for e in range(N_EXPERTS):
        au = _dot(hb, wup_ref[e])
        a, u = au[:, :D_EXPERT], au[:, D_EXPERT:]
        w_e = jnp.sum(jnp.where(lane == N_GROUPS + e, gate, 0.0), axis=-1, keepdims=True)
        hid_ref[:, e * D_EXPERT:(e + 1) * D_EXPERT] = (w_e * (a * _sigmoid(a) * u)).astype(BF16)
    y = _dot(hid_ref[...], wdn_ref[...])
    out_ref[...] = _layer_norm(DEEPNORM_ALPHA * h + y, g_ref[...], b_ref[...])


def hierarchical_moe(h, w_rg, b_rg, w_re, b_re, w_up, w_down, ln_g, ln_b, bm):
    t, d = h.shape
    ne = N_EXPERTS
    w_r = jnp.concatenate([w_rg, w_re.transpose(1, 0, 2).reshape(d, ne)], axis=1)
    w_r = jnp.pad(w_r, ((0, 0), (0, ROUTER_LANES - w_r.shape[1])))
    b_r = jnp.pad(jnp.concatenate([b_rg, b_re.reshape(ne)]), (0, ROUTER_LANES - N_GROUPS - ne)).reshape(1, -1)
    row = lambda i: (i, 0)
    return pl.pallas_call(
        _moe_kernel,
        grid=(t // bm,),
        in_specs=[pl.BlockSpec((bm, d), row), _resident((d, ROUTER_LANES)), _resident((1, ROUTER_LANES)),
                  _resident((ne, d, 2 * D_EXPERT)), _resident((ne * D_EXPERT, d)), _resident((1, d)),
                  _resident((1, d))],
        out_specs=pl.BlockSpec((bm, d), row),
        out_shape=jax.ShapeDtypeStruct((t, d), F32),
        scratch_shapes=[pltpu.VMEM((bm, ne * D_EXPERT), BF16)],
        compiler_params=_params(("parallel",)),
        name="hierarchical_moe",
    )(h, w_r, b_r, w_up.astype(BF16), w_down.reshape(ne * D_EXPERT, d).astype(BF16),
      ln_g.reshape(1, d), ln_b.reshape(1, d))


def _pad_in_weight(w_in):
    d = w_in.shape[0]
    sizes = (768, 768, MLA_Q_RANK, MLA_KV_RANK, MLA_ROPE, 256, 384, 12, 256)
    offs = np.concatenate([[0], np.cumsum(sizes)])
    sb, moba, c_q, c_kv, k_rope, nsa_q, nsa_kv, nsa_g, mem_q = (w_in[:, offs[i]:offs[i + 1]] for i in range(len(sizes)))
    z = lambda n: jnp.zeros((d, n), w_in.dtype)
    q_scale = HEAD_DIM ** -0.5 * LOG2E
    scale_q = lambda qkv: jnp.concatenate([qkv[:, :BRANCH_W] * q_scale, qkv[:, BRANCH_W:]], axis=1)
    return jnp.concatenate([scale_q(sb), scale_q(moba), c_q, c_kv, k_rope, _rotate_half_cols(k_rope), z(64),
                            nsa_kv[:, :128], nsa_g, z(116), nsa_kv[:, 128:], nsa_q * q_scale, mem_q * q_scale], axis=1)


def kernel(x, mem, w_in, g_cq, g_ckv, w_uq, w_ukv, nsa_pe, w_phi_k1, w_phi_k2, w_phi_v1, w_phi_v2, w_mem_kv, w_br,
           w_gate, b_gate, w_out, ln1_g, ln1_b, w_rg, b_rg, w_re, b_re, w_up, w_down, ln2_g, ln2_b):
    b, s_len, d = x.shape
    s = -(-s_len // MOBA_BLOCK) * MOBA_BLOCK
    t = b * s
    n_mem = mem.shape[1]
    h = jnp.pad(x, ((0, 0), (0, s - s_len), (0, 0))).reshape(t, d)
    for l in range(w_in.shape[0]):
        a, f = in_projection(h, _pad_in_weight(w_in[l]).astype(BF16), bm=512)
        a3, f3 = a.reshape(b, s, IN_PAD), f.reshape(b, s, F32_W)
        q, kv, kr = mla_project(f, g_cq[l], g_ckv[l], w_uq[l], w_ukv[l], s, bm=512)
        kc, vct = nsa_compress(f3, nsa_pe[l], w_phi_k1[l], w_phi_k2[l], w_phi_v1[l], w_phi_v2[l])
        mkv = matmul(mem.reshape(b * n_mem, d), w_mem_kv[l].astype(BF16), bm=n_mem, out_dtype=BF16)
        branches = [sb_attention(a3),
                    moba_attention(a3),
                    mla_attention(q.reshape(b, s, -1), kv.reshape(b, s, -1), kr.reshape(b, s, -1)),
                    nsa_attention(a3, f3, kc, vct),
                    cross_attention(a3, mkv.reshape(b, n_mem, -1), qb=512)]
        h = gated_merge(h, [o.reshape(t, BRANCH_W) for o in branches], w_gate[l], b_gate[l], w_br[l], w_out[l],
                        ln1_g[l], ln1_b[l], bm=256)
        h = hierarchical_moe(h, w_rg[l], b_rg[l], w_re[l], b_re[l], w_up[l], w_down[l], ln2_g[l], ln2_b[l], bm=256)
    return h.reshape(b, s, d)[:, :s_len]
```

```python
import functools

import jax
import jax.numpy as jnp
import numpy as np
from jax import lax
from jax.experimental import pallas as pl
from jax.experimental.pallas import tpu as pltpu

DEPTH = 4
HEAD_DIM = 64
N_HEADS = 4
BRANCH_W = N_HEADS * HEAD_DIM
N_BRANCHES = 5
MOBA_BLOCK = 256
MOBA_TOPK = 3
MLA_Q_RANK = 256
MLA_KV_RANK = 128
MLA_NOPE = 64
MLA_ROPE = 32
MLA_V = 64
ROPE_THETA = 10000.0
NSA_CMP_LEN = 32
NSA_CMP_STRIDE = 16
NSA_SEL_LEN = 64
NSA_TOPN = 16
NSA_WINDOW = 512
NSA_PHI_HIDDEN = 128
N_GROUPS = 4
EXPERTS_PER_GROUP = 4
N_EXPERTS = N_GROUPS * EXPERTS_PER_GROUP
D_EXPERT = 256
DEEPNORM_ALPHA = (2.0 * DEPTH) ** 0.25
LN_EPS = 1e-5
RMS_EPS = 1e-6
NEG = -1e30
BIG = 1e30

LANES = 128
BF16_SUBLANES = 16
VALUE_SLOT = HEAD_DIM + BF16_SUBLANES
QUERY_BLOCK = 256
KEY_TILE = 256
SOFTMAX_KEY_TILE = 512
SB_KEY_TILE = 256
LOG2E = 1.4426950408889634
VMEM_LIMIT_BYTES = 56 * 1024 * 1024

F32 = jnp.float32
BF16 = jnp.bfloat16
HIGHEST = lax.Precision.HIGHEST

COL_SB = 0
COL_MOBA = 768
COL_MLA = 1536
COL_NSA_CMP = 2048
COL_NSA_G = 2176
COL_NSA_SLC = 2304
COL_NSA_WIN = 2432
COL_NSA_Q = 2560
COL_MEM_Q = 2816
IN_PAD = 3072
COL_F32 = COL_MLA
F32_W = COL_NSA_SLC - COL_MLA


def _params(semantics):
    return pltpu.CompilerParams(dimension_semantics=semantics, vmem_limit_bytes=VMEM_LIMIT_BYTES)


def _dot(a, b):
    return jnp.dot(a, b, preferred_element_type=F32)


def _dot_exact(a, b):
    return jnp.dot(a, b, preferred_element_type=F32, precision=HIGHEST)


def _sigmoid(x):
    return 1.0 / (1.0 + jnp.exp(-x))


def _resident(shape):
    zeros = (0,) * len(shape)
    return pl.BlockSpec(shape, lambda *_: zeros, pipeline_mode=pl.Buffered(1))


def _in_proj_kernel(x_ref, w_ref, o_ref, f_ref):
    y = _dot(x_ref[...].astype(BF16), w_ref[...])
    o_ref[...] = y.astype(BF16)
    f_ref[...] = y[:, COL_F32:COL_F32 + F32_W]


def in_projection(h, w, bm):
    t, d = h.shape
    n = w.shape[1]
    return pl.pallas_call(
        _in_proj_kernel,
        grid=(t // bm,),
        in_specs=[pl.BlockSpec((bm, d), lambda i: (i, 0)), _resident((d, n))],
        out_specs=[pl.BlockSpec((bm, n), lambda i: (i, 0)), pl.BlockSpec((bm, F32_W), lambda i: (i, 0))],
        out_shape=[jax.ShapeDtypeStruct((t, n), BF16), jax.ShapeDtypeStruct((t, F32_W), F32)],
        compiler_params=_params(("parallel",)),
        name="in_projection",
    )(h, w)


def _mm_kernel(x_ref, w_ref, o_ref):
    o_ref[...] = _dot(x_ref[...].astype(BF16), w_ref[...]).astype(o_ref.dtype)


def matmul(x, w, *, bm, out_dtype=F32):
    m, k = x.shape
    n = w.shape[1]
    return pl.pallas_call(
        _mm_kernel,
        grid=(m // bm,),
        in_specs=[pl.BlockSpec((bm, k), lambda i: (i, 0)), _resident((k, n))],
        out_specs=pl.BlockSpec((bm, n), lambda i: (i, 0)),
        out_shape=jax.ShapeDtypeStruct((m, n), out_dtype),
        compiler_params=_params(("parallel",)),
        name="matmul",
    )(x, w)


def _softmax_tile(s, v_slot, m, acc, shift=None):
    top = jnp.max(s, axis=0, keepdims=True)
    m_new = jnp.maximum(m, top if shift is None else top - shift)
    p = jnp.exp2(s - (m_new if shift is None else m_new + shift))
    return m_new, jnp.exp2(m - m_new) * acc + _dot(v_slot, p.astype(BF16))


def _softmax_init(q):
    return (jnp.full((1, q), NEG, F32), jnp.zeros((VALUE_SLOT, q), F32))


def _softmax_result(acc):
    return acc[:HEAD_DIM] / acc[HEAD_DIM:HEAD_DIM + 1]


def _causal_softmax(qi, qb, kt, heads, scores, values, finish, finish_last, raw_ref):
    def tile(j, state, fin):
        return tuple(_softmax_tile(fin[h][0], values(j, h), *state[h], shift=fin[h][1]) for h in range(heads))

    def prefetch(j, slot):
        for h, raw in enumerate(scores(j)):
            raw_ref[slot, h] = raw

    def held(slot):
        return tuple(raw_ref[slot, h] for h in range(heads))

    def pair(i, state):
        prefetch(2 * i + 1, 1)
        state = tile(2 * i, state, finish(2 * i, held(0)))
        prefetch(2 * i + 2, 0)
        return tile(2 * i + 1, state, finish(2 * i + 1, held(1)))

    def odd_tail(state):
        prefetch(last, 1)
        state = tile(last - 1, state, finish(last - 1, held(0)))
        return tile(last, state, finish_last(last, held(1)))

    def even_tail(state):
        return tile(last, state, finish_last(last, held(0)))

    last = lax.div(qi * qb, kt)
    prefetch(0, 0)
    state = tuple(_softmax_init(qb) for _ in range(heads))
    state = lax.fori_loop(0, lax.div(last, 2), pair, state)
    state = lax.cond(lax.rem(last, 2) == 1, odd_tail, even_tail, state)
    return [_softmax_result(acc) for _, acc in state]


def _transposed(x):
    return x.astype(F32).T


def _pair_rows(qt_all, h):
    g = qt_all[(h // 2) * LANES:(h // 2 + 1) * LANES]
    row = lax.broadcasted_iota(jnp.int32, g.shape, 0)
    keep = (row >= HEAD_DIM) if h % 2 else (row < HEAD_DIM)
    return jnp.where(keep, g, 0.0).astype(BF16)


def _pair_cols(h):
    return slice((h // 2) * LANES, (h // 2 + 1) * LANES)


def _head_rows(h):
    return slice(h * HEAD_DIM, (h + 1) * HEAD_DIM)


def _fill_transposed(src_ref, dst_ref, kb, extra=None):
    def chunk(c, _):
        r0 = pl.multiple_of(c * kb, kb)
        x = src_ref[0, pl.ds(r0, kb), :].astype(F32)
        dst_ref[:, pl.ds(r0, kb)] = x.T.astype(BF16)
        if extra is not None:
            extra(c, x)
        return 0
    lax.fori_loop(0, src_ref.shape[1] // kb, chunk, 0)


def _fill_value_slots(src_ref, dst_ref, kb, first_rows):
    ones = jnp.ones((BF16_SUBLANES, kb), BF16)

    def chunk(c, _):
        r0 = pl.multiple_of(c * kb, kb)
        xt = src_ref[0, pl.ds(r0, kb), :].astype(F32).T
        for h, r in enumerate(first_rows):
            dst_ref[h * VALUE_SLOT:h * VALUE_SLOT + HEAD_DIM, pl.ds(r0, kb)] = xt[r:r + HEAD_DIM].astype(BF16)
            dst_ref[h * VALUE_SLOT + HEAD_DIM:(h + 1) * VALUE_SLOT, pl.ds(r0, kb)] = ones
        return 0
    lax.fori_loop(0, src_ref.shape[1] // kb, chunk, 0)


def _value_slot(h):
    return slice(h * VALUE_SLOT, (h + 1) * VALUE_SLOT)


def _store_heads(o_ref, outs):
    o_ref[0] = jnp.concatenate(outs, axis=0).T.astype(o_ref.dtype)


def _qkv_specs(s, kb, col):
    c = col // BRANCH_W
    return [pl.BlockSpec((1, kb, BRANCH_W), lambda bi, qi: (bi, qi, c)),
            pl.BlockSpec((1, s, BRANCH_W), lambda bi, qi: (bi, 0, c + 1)),
            pl.BlockSpec((1, s, BRANCH_W), lambda bi, qi: (bi, 0, c + 2))]


def _sb_kernel(q_ref, k_ref, v_ref, tri_ref, o_ref, vt_ref, raw_ref, *, kt):
    qi = pl.program_id(1)
    qb = q_ref.shape[1]
    kb = KEY_TILE
    pieces = kt // kb

    @pl.when(qi == 0)
    def _():
        _fill_transposed(v_ref, vt_ref, kb)

    qt_all = _transposed(q_ref[0])
    qts = [_pair_rows(qt_all, h) for h in range(N_HEADS)]
    tri = tri_ref[...]
    rel = lax.broadcasted_iota(jnp.int32, (kb, qb), 0) - lax.broadcasted_iota(jnp.int32, (kb, qb), 1)
    order = [(u, h) for u in reversed(range(pieces)) for h in range(N_HEADS)]

    def prefetch(j, slot):
        for u, h in order:
            raw_ref[slot, h, u * kb:(u + 1) * kb] = _dot(
                k_ref[0, pl.ds(pl.multiple_of(j * kt + u * kb, kb), kb), _pair_cols(h)], qts[h])

    def tile(j, state, slot, last):
        k0 = {u: pl.multiple_of(j * kt + u * kb, kb) for u in range(pieces)}
        log_beta, later, col_sum, past = {}, {}, {}, {}
        for u, h in order:
            z = raw_ref[slot, h, u * kb:(u + 1) * kb]
            nz = -z
            soft = jnp.log(1.0 + jnp.exp2(jnp.minimum(z, nz))) * LOG2E
            log_keep = jnp.minimum(nz, 0.0) - soft
            log_beta[u, h] = log_keep + z
            if last:
                past[u] = rel < qi * qb - k0[u]
                log_keep = jnp.where(past[u], log_keep, 0.0)
            sums = _dot(tri, log_keep.astype(BF16))
            later[u, h] = sums[:kb]
            col_sum[u, h] = sums[kb:kb + 1]
        state = list(state)
        for u, h in order:
            carry, acc = state[h]
            a = jnp.exp2(log_beta[u, h] + later[u, h])
            if last:
                a = jnp.where(past[u], a, 0.0)
            part = _dot(vt_ref[_head_rows(h), pl.ds(k0[u], kb)], a.astype(BF16))
            state[h] = (carry + col_sum[u, h], acc + jnp.exp2(carry) * part)
        return tuple(state)

    def pair(i, state):
        first = last - 1 - 2 * i
        prefetch(first - 1, 0)
        state = tile(first, state, 1, False)
        prefetch(jnp.maximum(first - 2, 0), 1)
        return tile(first - 1, state, 0, False)

    last = lax.div(qi * qb, kt)
    prefetch(last, 0)
    prefetch(jnp.maximum(last - 1, 0), 1)
    state = tuple((jnp.zeros((1, qb), F32), jnp.zeros((HEAD_DIM, qb), F32)) for _ in range(N_HEADS))
    state = tile(last, state, 0, True)
    state = lax.fori_loop(0, lax.div(last, 2), pair, state)
    state = lax.cond(lax.rem(last, 2) == 1, lambda st: tile(0, st, 1, False), lambda st: st, state)
    _store_heads(o_ref, [st[1] for st in state])


def sb_attention(a):
    b, s, _ = a.shape
    kb, qb = KEY_TILE, QUERY_BLOCK
    tri = jnp.asarray(np.concatenate([np.triu(np.ones((kb, kb), np.float32), 1), np.ones((BF16_SUBLANES, kb), np.float32)]),
                      BF16)
    return pl.pallas_call(
        functools.partial(_sb_kernel, kt=min(SB_KEY_TILE, s)),
        grid=(b, s // qb),
        in_specs=_qkv_specs(s, qb, COL_SB) + [_resident(tri.shape)],
        out_specs=pl.BlockSpec((1, qb, BRANCH_W), lambda bi, qi: (bi, qi, 0)),
        out_shape=jax.ShapeDtypeStruct((b, s, BRANCH_W), BF16),
        scratch_shapes=[pltpu.VMEM((BRANCH_W, s), BF16), pltpu.VMEM((2, N_HEADS, min(SB_KEY_TILE, s), qb), F32)],
        compiler_params=_params(("parallel", "arbitrary")),
        name="sb_attention",
    )(a, a, a, tri)


def _alibi_slopes(n):
    return np.power(2.0, -8.0 * np.arange(1, n + 1, dtype=np.float64) / n).astype(np.float32)


def _select_top(score, ids, count, floor):
    def step(_, c):
        score, sel = c
        mx = jnp.max(score, axis=0, keepdims=True)
        idx = jnp.min(jnp.where(score == mx, ids, 1e9), axis=0, keepdims=True)
        pick = ids == idx
        sel = jnp.where(jnp.logical_and(pick, mx > floor), 1.0, sel)
        score = jnp.where(pick, -jnp.inf, score)
        return score, sel
    return lax.fori_loop(0, count, step, (score, jnp.zeros_like(score)))[1]


def _moba_kernel(slope_ref, q_ref, k_ref, v_ref, o_ref, vt_ref, km_ref, sel_ref, raw_ref, *, kt, topk):
    qi = pl.program_id(1)
    kb = q_ref.shape[1]
    nblk = km_ref.shape[0]
    per_tile = kt // kb

    @pl.when(qi == 0)
    def _():
        def key_mean(c, _):
            km_ref[pl.ds(c, 1), :] = jnp.mean(k_ref[0, pl.ds(pl.multiple_of(c * kb, kb), kb), :].astype(F32),
                                              axis=0, keepdims=True)
            return 0
        _fill_value_slots(v_ref, vt_ref, kb, [h * HEAD_DIM for h in range(N_HEADS)])
        lax.fori_loop(0, nblk, key_mean, 0)

    qt_all = _transposed(q_ref[0])
    qts = [_pair_rows(qt_all, h) for h in range(N_HEADS)]
    ids = lax.broadcasted_iota(jnp.int32, (nblk, kb), 0)
    row = lax.broadcasted_iota(jnp.int32, qt_all.shape, 0)
    km = km_ref[...]
    for h in range(N_HEADS):
        in_head = jnp.logical_and(row >= h * HEAD_DIM, row < (h + 1) * HEAD_DIM)
        gscore = _dot_exact(km, jnp.where(in_head, qt_all, 0.0))
        gscore = jnp.where(ids < qi, gscore, NEG)
        sel_ref[h] = jnp.where(_select_top(gscore, ids.astype(F32), topk, 0.5 * NEG) > 0.5, 0.0, NEG)

    rk = lax.broadcasted_iota(jnp.int32, (kt, kb), 0)
    rel = rk - lax.broadcasted_iota(jnp.int32, (kt, kb), 1)
    bias = [slope_ref[h] * rel.astype(F32) for h in range(N_HEADS)]

    def scores(j):
        k0 = pl.multiple_of(j * kt, kt)
        return tuple(_dot(k_ref[0, pl.ds(k0, kt), _pair_cols(h)], qts[h]) for h in range(N_HEADS))

    def values(j, h):
        return vt_ref[_value_slot(h), pl.ds(pl.multiple_of(j * kt, kt), kt)]

    def unselected(j, h):
        rows = [jnp.broadcast_to(sel_ref[h, pl.ds(j * per_tile + r, 1), :], (kb, kb)) for r in range(per_tile)]
        return jnp.concatenate(rows, axis=0)

    def finish(j, raw):
        off = (qi * kb - j * kt).astype(F32)
        return tuple((raw[h] + bias[h] + unselected(j, h), slope_ref[h] * off) for h in range(N_HEADS))

    def finish_last(j, raw):
        d0 = qi * kb - j * kt
        own_causal = jnp.logical_and(rk >= d0, rel <= d0)
        out = []
        for h in range(N_HEADS):
            s = raw[h] + bias[h]
            out.append((jnp.where(own_causal, s, s + unselected(j, h)), slope_ref[h] * d0.astype(F32)))
        return tuple(out)

    _store_heads(o_ref, _causal_softmax(qi, kb, kt, N_HEADS, scores, values, finish, finish_last, raw_ref))


def moba_attention(a):
    b, s, _ = a.shape
    kb = MOBA_BLOCK
    nblk = s // kb
    topk = min(MOBA_TOPK, nblk - 1)
    slopes = jnp.asarray(_alibi_slopes(N_HEADS) * np.float32(LOG2E))
    c = COL_MOBA // BRANCH_W
    return pl.pallas_call(
        functools.partial(_moba_kernel, kt=min(SOFTMAX_KEY_TILE, s), topk=topk),
        grid_spec=pltpu.PrefetchScalarGridSpec(
            num_scalar_prefetch=1,
            grid=(b, nblk),
            in_specs=[pl.BlockSpec((1, kb, BRANCH_W), lambda bi, qi, sl: (bi, qi, c)),
                      pl.BlockSpec((1, s, BRANCH_W), lambda bi, qi, sl: (bi, 0, c + 1)),
                      pl.BlockSpec((1, s, BRANCH_W), lambda bi, qi, sl: (bi, 0, c + 2))],
            out_specs=pl.BlockSpec((1, kb, BRANCH_W), lambda bi, qi, sl: (bi, qi, 0)),
            scratch_shapes=[pltpu.VMEM((N_HEADS * VALUE_SLOT, s), BF16),
                            pltpu.VMEM((nblk, BRANCH_W), F32),
                            pltpu.VMEM((N_HEADS, nblk, kb), F32),
                            pltpu.VMEM((2, N_HEADS, min(SOFTMAX_KEY_TILE, s), kb), F32)]),
        out_shape=jax.ShapeDtypeStruct((b, s, BRANCH_W), BF16),
        compiler_params=_params(("parallel", "arbitrary")),
        name="moba_attention",
    )(slopes, a, a, a)


MLA_SLOT = 128


def _rms(x, g):
    return x * lax.rsqrt(jnp.mean(x * x, axis=-1, keepdims=True) + RMS_EPS) * g


def _mla_proj_kernel(x_ref, gq_ref, gkv_ref, wq_ref, wqr_ref, wkv_ref, cq_ref, sq_ref, ck_ref, sk_ref,
                     q_ref, kv_ref, kr_ref):
    x = x_ref[...]
    c_q = _rms(x[:, :MLA_Q_RANK], gq_ref[...]).astype(BF16)
    c_kv = _rms(x[:, MLA_Q_RANK:MLA_Q_RANK + MLA_KV_RANK], gkv_ref[...]).astype(BF16)
    q_ref[...] = (_dot(c_q, wq_ref[...]) * cq_ref[...] + _dot(c_q, wqr_ref[...]) * sq_ref[...]).astype(q_ref.dtype)
    kv_ref[...] = _dot(c_kv, wkv_ref[...]).astype(kv_ref.dtype)
    tail = x[:, MLA_Q_RANK + MLA_KV_RANK:]
    rot = pltpu.roll(tail, LANES - MLA_ROPE, axis=1)
    kr_ref[...] = (tail * ck_ref[...] + rot * sk_ref[...]).astype(kr_ref.dtype)


def _rope_tables(s_len):
    half = MLA_ROPE // 2
    freqs = jnp.power(ROPE_THETA, -jnp.arange(half, dtype=F32) / half)
    ang = jnp.arange(s_len).astype(F32)[:, None] * freqs
    cos = jnp.concatenate([jnp.cos(ang)] * 2, axis=-1)
    sin = jnp.concatenate([jnp.sin(ang)] * 2, axis=-1)
    zk = jnp.zeros((s_len, LANES - MLA_ROPE), F32)
    zq = jnp.zeros((s_len, MLA_SLOT - MLA_NOPE - MLA_ROPE), F32)
    cq = jnp.tile(jnp.concatenate([jnp.ones((s_len, MLA_NOPE), F32), cos, zq], axis=-1), (1, N_HEADS))
    sq = jnp.tile(jnp.concatenate([jnp.zeros((s_len, MLA_NOPE), F32), sin, zq], axis=-1), (1, N_HEADS))
    return cq, sq, jnp.concatenate([cos, zk], axis=-1), jnp.concatenate([sin, zk], axis=-1)


def _rotate_half_cols(w):
    half = w.shape[-1] // 2
    return jnp.concatenate([-w[..., half:], w[..., :half]], axis=-1)


def mla_project(f, g_cq, g_ckv, w_uq, w_ukv, s_len, bm):
    t = f.shape[0]
    wq = w_uq.reshape(MLA_Q_RANK, N_HEADS, MLA_NOPE + MLA_ROPE)
    pad = jnp.zeros((MLA_Q_RANK, N_HEADS, MLA_SLOT - MLA_NOPE - MLA_ROPE), wq.dtype)
    wq_s = jnp.concatenate([wq, pad], axis=-1).reshape(MLA_Q_RANK, -1)
    wqr_s = jnp.concatenate([jnp.zeros_like(wq[..., :MLA_NOPE]), _rotate_half_cols(wq[..., MLA_NOPE:]), pad],
                            axis=-1).reshape(MLA_Q_RANK, -1)
    cq, sq, ck, sk = _rope_tables(s_len)
    q_scale = (MLA_NOPE + MLA_ROPE) ** -0.5 * LOG2E
    nrow = s_len // bm
    row = lambda i: (i, 0)
    pos = lambda i: (i % nrow, 0)
    qw = N_HEADS * MLA_SLOT
    kvw = N_HEADS * (MLA_NOPE + MLA_V)
    return pl.pallas_call(
        _mla_proj_kernel,
        grid=(t // bm,),
        in_specs=[pl.BlockSpec((bm, 512), row),
                  _resident((1, MLA_Q_RANK)), _resident((1, MLA_KV_RANK)),
                  _resident((MLA_Q_RANK, qw)), _resident((MLA_Q_RANK, qw)), _resident((MLA_KV_RANK, kvw)),
                  pl.BlockSpec((bm, qw), pos), pl.BlockSpec((bm, qw), pos),
                  pl.BlockSpec((bm, LANES), pos), pl.BlockSpec((bm, LANES), pos)],
        out_specs=[pl.BlockSpec((bm, qw), row), pl.BlockSpec((bm, kvw), row), pl.BlockSpec((bm, LANES), row)],
        out_shape=[jax.ShapeDtypeStruct((t, qw), BF16), jax.ShapeDtypeStruct((t, kvw), BF16),
                   jax.ShapeDtypeStruct((t, LANES), BF16)],
        compiler_params=_params(("parallel",)),
        name="mla_project",
    )(f, g_cq.reshape(1, -1), g_ckv.reshape(1, -1), (wq_s * q_scale).astype(BF16), (wqr_s * q_scale).astype(BF16),
      w_ukv.astype(BF16), cq, sq, ck, sk)


def _mla_kernel(q_ref, kv_ref, kr_ref, o_ref, vt_ref, raw_ref, *, kt):
    qi = pl.program_id(1)
    qb = q_ref.shape[1]

    @pl.when(qi == 0)
    def _():
        _fill_value_slots(kv_ref, vt_ref, KEY_TILE, [h * LANES + MLA_NOPE for h in range(N_HEADS)])

    qt_all = _transposed(q_ref[0])
    qts = []
    for h in range(N_HEADS):
        g = qt_all[h * MLA_SLOT:(h + 1) * MLA_SLOT]
        row = lax.broadcasted_iota(jnp.int32, g.shape, 0)
        qts.append(jnp.concatenate([jnp.where(row < MLA_NOPE, g, 0.0), g[MLA_NOPE:], jnp.zeros_like(g[MLA_NOPE:])],
                                   axis=0).astype(BF16))
    rel = lax.broadcasted_iota(jnp.int32, (kt, qb), 0) - lax.broadcasted_iota(jnp.int32, (kt, qb), 1)

    def scores(j):
        k0 = pl.multiple_of(j * kt, kt)
        kr = kr_ref[0, pl.ds(k0, kt), :]
        return tuple(_dot(jnp.concatenate([kv_ref[0, pl.ds(k0, kt), h * LANES:(h + 1) * LANES], kr], axis=1), qts[h])
                     for h in range(N_HEADS))

    def values(j, h):
        return vt_ref[_value_slot(h), pl.ds(pl.multiple_of(j * kt, kt), kt)]

    def finish_last(j, raw):
        causal = rel <= qi * qb - j * kt
        return tuple((jnp.where(causal, s, NEG), None) for s in raw)

    outs = _causal_softmax(qi, qb, kt, N_HEADS, scores, values,
                           lambda j, raw: tuple((s, None) for s in raw), finish_last, raw_ref)
    _store_heads(o_ref, outs)


def mla_attention(q, kv, kr):
    b, s, qw = q.shape
    kb = QUERY_BLOCK
    return pl.pallas_call(
        functools.partial(_mla_kernel, kt=min(SOFTMAX_KEY_TILE, s)),
        grid=(b, s // kb),
        in_specs=[pl.BlockSpec((1, kb, qw), lambda bi, qi: (bi, qi, 0)),
                  pl.BlockSpec((1, s, kv.shape[2]), lambda bi, qi: (bi, 0, 0)),
                  pl.BlockSpec((1, s, LANES), lambda bi, qi: (bi, 0, 0))],
        out_specs=pl.BlockSpec((1, kb, BRANCH_W), lambda bi, qi: (bi, qi, 0)),
        out_shape=jax.ShapeDtypeStruct((b, s, BRANCH_W), BF16),
        scratch_shapes=[pltpu.VMEM((N_HEADS * VALUE_SLOT, s), BF16),
                        pltpu.VMEM((2, N_HEADS, min(SOFTMAX_KEY_TILE, s), kb), F32)],
        compiler_params=_params(("parallel", "arbitrary")),
        name="mla_attention",
    )(q, kv, kr)


def _cross_kernel(q_ref, kv_ref, o_ref):
    qt_all = _transposed(q_ref[0])
    kv = kv_ref[0]
    vt = _transposed(kv[:, BRANCH_W:]).astype(BF16)
    outs = []
    for h in range(N_HEADS):
        s = _dot(kv[:, _pair_cols(h)], _pair_rows(qt_all, h))
        p = jnp.exp2(s - jnp.max(s, axis=0, keepdims=True))
        outs.append(_dot(vt[_head_rows(h)], p.astype(BF16)) / jnp.sum(p, axis=0, keepdims=True))
    _store_heads(o_ref, outs)


def cross_attention(a, mkv, qb):
    b, s, _ = a.shape
    n = mkv.shape[1]
    c = COL_MEM_Q // BRANCH_W
    return pl.pallas_call(
        _cross_kernel,
        grid=(b, s // qb),
        in_specs=[pl.BlockSpec((1, qb, BRANCH_W), lambda bi, qi: (bi, qi, c)),
                  pl.BlockSpec((1, n, 2 * BRANCH_W), lambda bi, qi: (bi, 0, 0))],
        out_specs=pl.BlockSpec((1, qb, BRANCH_W), lambda bi, qi: (bi, qi, 0)),
        out_shape=jax.ShapeDtypeStruct((b, s, BRANCH_W), BF16),
        compiler_params=_params(("parallel", "parallel")),
        name="cross_attention",
    )(a, mkv)


def _compress_kernel(x_ref, pe_ref, w1_ref, w2_ref, kc_ref, vct_ref):
    n = kc_ref.shape[1]
    first = jnp.zeros((n, 2 * NSA_PHI_HIDDEN), F32)
    second = jnp.zeros((n, 2 * NSA_PHI_HIDDEN), F32)
    for r in range(NSA_CMP_STRIDE):
        x = x_ref[0, pl.ds(r, n, stride=NSA_CMP_STRIDE), :]
        first = first + _dot((x + pe_ref[r:r + 1]).astype(BF16), w1_ref[r])
        second = second + _dot((x + pe_ref[NSA_CMP_STRIDE + r:NSA_CMP_STRIDE + r + 1]).astype(BF16),
                               w1_ref[NSA_CMP_STRIDE + r])
    hidden = jax.nn.gelu(first + pltpu.roll(second, n - 1, axis=0))
    out = _dot(hidden.astype(BF16), w2_ref[...])
    kc_ref[0] = out.astype(BF16)
    vct_ref[0] = out.T[HEAD_DIM:].astype(BF16)


def _pair_diag(wk, wv):
    z = jnp.zeros_like(wk)
    return jnp.concatenate([jnp.concatenate([wk, z], axis=-1), jnp.concatenate([z, wv], axis=-1)], axis=-2)


def nsa_compress(f, nsa_pe, w_k1, w_k2, w_v1, w_v2):
    b, s, _ = f.shape
    n = s // NSA_CMP_STRIDE
    hd = HEAD_DIM
    w1 = _pair_diag(w_k1.reshape(NSA_CMP_LEN, hd, -1), w_v1.reshape(NSA_CMP_LEN, hd, -1)).astype(BF16)
    w2 = _pair_diag(w_k2, w_v2).astype(BF16)
    pe = jnp.concatenate([nsa_pe, nsa_pe], axis=-1)
    c = (COL_NSA_CMP - COL_F32) // LANES
    return pl.pallas_call(
        _compress_kernel,
        grid=(b,),
        in_specs=[pl.BlockSpec((1, s, LANES), lambda bi: (bi, 0, c)),
                  _resident(pe.shape), _resident(w1.shape), _resident(w2.shape)],
        out_specs=[pl.BlockSpec((1, n, LANES), lambda bi: (bi, 0, 0)),
                   pl.BlockSpec((1, hd, n), lambda bi: (bi, 0, 0))],
        out_shape=[jax.ShapeDtypeStruct((b, n, LANES), BF16), jax.ShapeDtypeStruct((b, hd, n), BF16)],
        compiler_params=_params(("parallel",)),
        name="nsa_compress",
    )(f, pe, w1, w2)


def _nsa_kernel(q_ref, g_ref, slope_ref, kc_ref, vct_ref, ovt_ref, slc_ref, win_ref, o_ref,
                vst_ref, vwt_ref, sel_ref, bias_ref, cbias_ref, m_ref, acc_ref, wanted_ref, *, topn, n_cmp):
    qi = pl.program_id(1)
    qn = q_ref.shape[1]
    kb = KEY_TILE
    lanes = N_HEADS * qn
    dv = HEAD_DIM
    q0 = qi * qn

    slope = slope_ref[...]
    ql = jnp.bitwise_and(lax.broadcasted_iota(jnp.int32, (1, lanes), 1), qn - 1)
    qpos = q0 + ql
    kid = lax.broadcasted_iota(jnp.int32, (kb, 1), 0)
    ncp = kc_ref.shape[1]
    cid = lax.broadcasted_iota(jnp.int32, (ncp, 1), 0)
    cmp_last = cid * NSA_CMP_STRIDE + (NSA_CMP_LEN - 1)
    cmp_end = jnp.where(cid < n_cmp, cmp_last, 1 << 30)

    @pl.when(qi == 0)
    def _():
        _fill_value_slots(slc_ref, vst_ref, kb, [HEAD_DIM])
        _fill_value_slots(win_ref, vwt_ref, kb, [HEAD_DIM])
        bias_ref[...] = slope * (ql - kid).astype(F32)
        cbias_ref[...] = slope * (ql - cmp_last).astype(F32)

    qt_all = _transposed(q_ref[0])
    qt = jnp.concatenate([qt_all[_head_rows(h)] for h in range(N_HEADS)], axis=1)
    qt = jnp.concatenate([qt, jnp.zeros_like(qt)], axis=0).astype(BF16)
    jd = lax.div(q0, kb)
    win_tiles = [jnp.maximum(jd - back, 0) for back in range(NSA_WINDOW // kb + 1)]

    def raw_scores(kv_ref, j):
        return _dot(kv_ref[0, pl.ds(pl.multiple_of(j * kb, kb), kb), :], qt)

    raw_c = _dot(kc_ref[0], qt)
    raw_d = raw_scores(slc_ref, jd)
    raw_w = [raw_scores(win_ref, j) for j in win_tiles]
    bias = bias_ref[...]
    causal = ql - kid >= 0

    valid = cmp_end - ql <= q0
    s = jnp.where(valid, raw_c - cbias_ref[...], NEG)
    e = jnp.where(valid, jnp.exp2(s - jnp.max(s, axis=0, keepdims=True)), 0.0)
    den = jnp.sum(e, axis=0, keepdims=True)
    p_c = e / jnp.where(den > 0.0, den, 1.0)
    o_c = _dot(vct_ref[0], p_c.astype(BF16))

    p_sum = p_c[:, 0:qn]
    for hh in range(1, N_HEADS):
        p_sum = p_sum + p_c[:, hh * qn:(hh + 1) * qn]
    imp = _dot_exact(ovt_ref[...], p_sum)
    nsel = imp.shape[0]
    sid = lax.broadcasted_iota(jnp.int32, (nsel, qn), 0)
    cur = jnp.right_shift(qpos[:, 0:qn], NSA_SEL_LEN.bit_length() - 1)
    forced = jnp.logical_or(sid == 0, sid == cur)
    score = jnp.where(forced, BIG, jnp.where(sid < cur, imp, NEG))
    sel = _select_top(score, sid.astype(F32), topn, 0.5 * NEG)
    sel_ref[...] = jnp.concatenate([jnp.where(sel > 0.5, 0.0, NEG)] * N_HEADS, axis=1)
    per_tile = kb // NSA_SEL_LEN
    for j in range(nsel // per_tile):
        wanted_ref[j] = (jnp.max(sel[j * per_tile:(j + 1) * per_tile]) > 0.5).astype(jnp.int32)

    def sel_tile(j, c, raw, own):
        rows = [jnp.broadcast_to(sel_ref[pl.ds(j * per_tile + r, 1), :], (NSA_SEL_LEN, lanes))
                for r in range(per_tile)]
        s = raw - bias + jnp.concatenate(rows, axis=0)
        if own:
            s = jnp.where(causal, s, NEG)
        shift = slope * ((jd - j) * kb).astype(F32)
        return _softmax_tile(s, vst_ref[:, pl.ds(pl.multiple_of(j * kb, kb), kb)], *c, shift=shift)

    m_ref[...], acc_ref[...] = sel_tile(jd, _softmax_init(lanes), raw_d, True)

    def maybe_sel_tile(j, _):
        @pl.when(wanted_ref[j] > 0)
        def _():
            m_ref[...], acc_ref[...] = sel_tile(j, (m_ref[...], acc_ref[...]), raw_scores(slc_ref, j), False)
        return 0

    lax.fori_loop(0, jd, maybe_sel_tile, 0)
    o_s = _softmax_result(acc_ref[...])

    tops, masked = [], []
    for back, raw in enumerate(raw_w):
        s = raw - bias
        if back == 0:
            s = jnp.where(causal, s, NEG)
        else:
            inside = NSA_WINDOW - back * kb if back * kb + kb > NSA_WINDOW else 2 * kb
            s = jnp.where(ql - kid < jnp.where(jd - back >= 0, inside, -2 * kb), s, NEG)
        masked.append(s)
        tops.append(jnp.max(s, axis=0, keepdims=True) - slope * float(back * kb))
    m_w = functools.reduce(jnp.maximum, tops)
    acc_w = jnp.zeros((VALUE_SLOT, lanes), F32)
    for back, s in enumerate(masked):
        p = jnp.exp2(s - (m_w + slope * float(back * kb)))
        acc_w = acc_w + _dot(vwt_ref[:, pl.ds(pl.multiple_of(win_tiles[back] * kb, kb), kb)], p.astype(BF16))
    o_w = _softmax_result(acc_w)

    gt = _sigmoid(g_ref[0]).T
    def gate(ci):
        return jnp.concatenate([gt[h * 3 + ci:h * 3 + ci + 1] for h in range(N_HEADS)], axis=1)
    out = gate(0) * o_c + gate(1) * o_s + gate(2) * o_w
    _store_heads(o_ref, [out[:, h * qn:(h + 1) * qn] for h in range(N_HEADS)])


def nsa_attention(a, f, kc, vct):
    b, s, _ = a.shape
    ncp = kc.shape[1]
    n_cmp = ncp - NSA_CMP_LEN // NSA_CMP_STRIDE + 1
    nsel = s // NSA_SEL_LEN
    qn = QUERY_BLOCK
    lanes = N_HEADS * qn
    cs = np.arange(ncp) * NSA_CMP_STRIDE
    ss = np.arange(nsel) * NSA_SEL_LEN
    ov = ((cs[:, None] < ss[None, :] + NSA_SEL_LEN) & (cs[:, None] + NSA_CMP_LEN > ss[None, :])
          & (np.arange(ncp)[:, None] < n_cmp)).astype(np.float32)
    slopes = jnp.asarray(np.repeat(_alibi_slopes(N_HEADS) * np.float32(LOG2E), qn)[None, :])
    per_b = lambda bi, qi: (bi, 0, 0)
    return pl.pallas_call(
        functools.partial(_nsa_kernel, topn=min(NSA_TOPN, nsel), n_cmp=n_cmp),
        grid=(b, s // qn),
        in_specs=[pl.BlockSpec((1, qn, BRANCH_W), lambda bi, qi: (bi, qi, COL_NSA_Q // BRANCH_W)),
                  pl.BlockSpec((1, qn, LANES), lambda bi, qi: (bi, qi, (COL_NSA_G - COL_F32) // LANES)),
                  _resident((1, lanes)),
                  pl.BlockSpec((1, ncp, LANES), per_b),
                  pl.BlockSpec((1, HEAD_DIM, ncp), per_b),
                  _resident((nsel, ncp)),
                  pl.BlockSpec((1, s, LANES), lambda bi, qi: (bi, 0, COL_NSA_SLC // LANES)),
                  pl.BlockSpec((1, s, LANES), lambda bi, qi: (bi, 0, COL_NSA_WIN // LANES))],
        out_specs=pl.BlockSpec((1, qn, BRANCH_W), lambda bi, qi: (bi, qi, 0)),
        out_shape=jax.ShapeDtypeStruct((b, s, BRANCH_W), BF16),
        scratch_shapes=[pltpu.VMEM((VALUE_SLOT, s), BF16), pltpu.VMEM((VALUE_SLOT, s), BF16),
                        pltpu.VMEM((nsel, lanes), F32), pltpu.VMEM((KEY_TILE, lanes), F32),
                        pltpu.VMEM((ncp, lanes), F32), pltpu.VMEM((1, lanes), F32),
                        pltpu.VMEM((VALUE_SLOT, lanes), F32), pltpu.SMEM((s // KEY_TILE,), jnp.int32)],
        compiler_params=_params(("parallel", "arbitrary")),
        name="nsa_attention",
    )(a, f, slopes, kc, vct, jnp.asarray(ov.T), a, a)


def _layer_norm(r, g, b):
    mu = jnp.mean(r, axis=-1, keepdims=True)
    c = r - mu
    var = jnp.mean(c * c, axis=-1, keepdims=True)
    return c * lax.rsqrt(var + LN_EPS) * g + b


def _merge_kernel(h_ref, o0_ref, o1_ref, o2_ref, o3_ref, o4_ref, wg_ref, bg_ref, wbr_ref, wout_ref, g_ref, b_ref,
                  out_ref):
    h = h_ref[...]
    hb = h.astype(BF16)
    merged = jnp.zeros(h.shape, F32)
    for i, o_ref in enumerate((o0_ref, o1_ref, o2_ref, o3_ref, o4_ref)):
        gate = _sigmoid(_dot(hb, wg_ref[i]) + bg_ref[i])
        merged = merged + gate * _dot(o_ref[...], wbr_ref[i])
    y = _dot(merged.astype(BF16), wout_ref[...])
    out_ref[...] = _layer_norm(DEEPNORM_ALPHA * h + y, g_ref[...], b_ref[...])


def gated_merge(h, branches, w_gate, b_gate, w_br, w_out, ln_g, ln_b, bm):
    t, d = h.shape
    nb, bw = len(branches), branches[0].shape[1]
    row = lambda i: (i, 0)
    return pl.pallas_call(
        _merge_kernel,
        grid=(t // bm,),
        in_specs=[pl.BlockSpec((bm, d), row)] + [pl.BlockSpec((bm, bw), row)] * nb
                 + [_resident((nb, d, d)), _resident((nb, 1, d)), _resident((nb, bw, d)), _resident((d, d)),
                    _resident((1, d)), _resident((1, d))],
        out_specs=pl.BlockSpec((bm, d), row),
        out_shape=jax.ShapeDtypeStruct((t, d), F32),
        compiler_params=_params(("parallel",)),
        name="gated_merge",
    )(h, *branches, w_gate.astype(BF16), b_gate.reshape(nb, 1, d), w_br.astype(BF16), w_out.astype(BF16),
      ln_g.reshape(1, d), ln_b.reshape(1, d))


ROUTER_LANES = 128


def _moe_kernel(h_ref, wr_ref, br_ref, wup_ref, wdn_ref, g_ref, b_ref, out_ref, hid_ref):
    h = h_ref[...]
    hb = h.astype(BF16)
    bm = h.shape[0]
    logits = _dot_exact(h, wr_ref[...]) + br_ref[...]
    lane = lax.broadcasted_iota(jnp.int32, (bm, ROUTER_LANES), 1)
    lane_f = lane.astype(F32)
    is_g = lane < N_GROUPS
    glog = jnp.where(is_g, logits, NEG)
    gmax = jnp.max(glog, axis=-1, keepdims=True)
    g_sel = jnp.min(jnp.where(glog == gmax, lane_f, 1e9), axis=-1, keepdims=True)
    pg_sel = 1.0 / jnp.sum(jnp.where(is_g, jnp.exp(glog - gmax), 0.0), axis=-1, keepdims=True)
    lo = N_GROUPS + g_sel * EXPERTS_PER_GROUP
    in_grp = jnp.logical_and(lane_f >= lo, lane_f < lo + EXPERTS_PER_GROUP)
    elog = jnp.where(in_grp, logits, NEG)
    emax = jnp.max(elog, axis=-1, keepdims=True)
    ee = jnp.where(in_grp, jnp.exp(elog - emax), 0.0)
    pe = ee / jnp.sum(ee, axis=-1, keepdims=True)
    pe_m = jnp.where(in_grp, pe, -1.0)
    v1 = jnp.max(pe_m, axis=-1, keepdims=True)
    i1 = jnp.min(jnp.where(pe_m == v1, lane_f, 1e9), axis=-1, keepdims=True)
    pe_m2 = jnp.where(lane_f == i1, -1.0, pe_m)
    v2 = jnp.max(pe_m2, axis=-1, keepdims=True)
    i2 = jnp.min(jnp.where(pe_m2 == v2, lane_f, 1e9), axis=-1, keepdims=True)
    norm = pg_sel / (v1 + v2)
    gate = jnp.where(lane_f == i1, v1 * norm, jnp.where(lane_f == i2, v2 * norm, 0.0))
    for e in range(N_EXPERTS):
        au = _dot(hb, wup_ref[e])
        a, u = au[:, :D_EXPERT], au[:, D_EXPERT:]
        w_e = jnp.sum(jnp.where(lane == N_GROUPS + e, gate, 0.0), axis=-1, keepdims=True)
        hid_ref[:, e * D_EXPERT:(e + 1) * D_EXPERT] = (w_e * (a * _sigmoid(a) * u)).astype(BF16)
    y = _dot(hid_ref[...], wdn_ref[...])
    out_ref[...] = _layer_norm(DEEPNORM_ALPHA * h + y, g_ref[...], b_ref[...])


def hierarchical_moe(h, w_rg, b_rg, w_re, b_re, w_up, w_down, ln_g, ln_b, bm):
    t, d = h.shape
    ne = N_EXPERTS
    w_r = jnp.concatenate([w_rg, w_re.transpose(1, 0, 2).reshape(d, ne)], axis=1)
    w_r = jnp.pad(w_r, ((0, 0), (0, ROUTER_LANES - w_r.shape[1])))
    b_r = jnp.pad(jnp.concatenate([b_rg, b_re.reshape(ne)]), (0, ROUTER_LANES - N_GROUPS - ne)).reshape(1, -1)
    row = lambda i: (i, 0)
    return pl.pallas_call(
        _moe_kernel,
        grid=(t // bm,),
        in_specs=[pl.BlockSpec((bm, d), row), _resident((d, ROUTER_LANES)), _resident((1, ROUTER_LANES)),
                  _resident((ne, d, 2 * D_EXPERT)), _resident((ne * D_EXPERT, d)), _resident((1, d)),
                  _resident((1, d))],
        out_specs=pl.BlockSpec((bm, d), row),
        out_shape=jax.ShapeDtypeStruct((t, d), F32),
        scratch_shapes=[pltpu.VMEM((bm, ne * D_EXPERT), BF16)],
        compiler_params=_params(("parallel",)),
        name="hierarchical_moe",
    )(h, w_r, b_r, w_up.astype(BF16), w_down.reshape(ne * D_EXPERT, d).astype(BF16),
      ln_g.reshape(1, d), ln_b.reshape(1, d))


def _pad_in_weight(w_in):
    d = w_in.shape[0]
    sizes = (768, 768, MLA_Q_RANK, MLA_KV_RANK, MLA_ROPE, 256, 384, 12, 256)
    offs = np.concatenate([[0], np.cumsum(sizes)])
    sb, moba, c_q, c_kv, k_rope, nsa_q, nsa_kv, nsa_g, mem_q = (w_in[:, offs[i]:offs[i + 1]] for i in range(len(sizes)))
    z = lambda n: jnp.zeros((d, n), w_in.dtype)
    q_scale = HEAD_DIM ** -0.5 * LOG2E
    scale_q = lambda qkv: jnp.concatenate([qkv[:, :BRANCH_W] * q_scale, qkv[:, BRANCH_W:]], axis=1)
    return jnp.concatenate([scale_q(sb), scale_q(moba), c_q, c_kv, k_rope, _rotate_half_cols(k_rope), z(64),
                            nsa_kv[:, :128], nsa_g, z(116), nsa_kv[:, 128:], nsa_q * q_scale, mem_q * q_scale], axis=1)


def kernel(x, mem, w_in, g_cq, g_ckv, w_uq, w_ukv, nsa_pe, w_phi_k1, w_phi_k2, w_phi_v1, w_phi_v2, w_mem_kv, w_br,
           w_gate, b_gate, w_out, ln1_g, ln1_b, w_rg, b_rg, w_re, b_re, w_up, w_down, ln2_g, ln2_b):
    b, s_len, d = x.shape
    s = -(-s_len // MOBA_BLOCK) * MOBA_BLOCK
    t = b * s
    n_mem = mem.shape[1]
    h = jnp.pad(x, ((0, 0), (0, s - s_len), (0, 0))).reshape(t, d)
    for l in range(w_in.shape[0]):
        a, f = in_projection(h, _pad_in_weight(w_in[l]).astype(BF16), bm=512)
        a3, f3 = a.reshape(b, s, IN_PAD), f.reshape(b, s, F32_W)
        q, kv, kr = mla_project(f, g_cq[l], g_ckv[l], w_uq[l], w_ukv[l], s, bm=512)
        kc, vct = nsa_compress(f3, nsa_pe[l], w_phi_k1[l], w_phi_k2[l], w_phi_v1[l], w_phi_v2[l])
        mkv = matmul(mem.reshape(b * n_mem, d), w_mem_kv[l].astype(BF16), bm=n_mem, out_dtype=BF16)
        branches = [sb_attention(a3),
                    moba_attention(a3),
                    mla_attention(q.reshape(b, s, -1), kv.reshape(b, s, -1), kr.reshape(b, s, -1)),
                    nsa_attention(a3, f3, kc, vct),
                    cross_attention(a3, mkv.reshape(b, n_mem, -1), qb=512)]
        h = gated_merge(h, [o.reshape(t, BRANCH_W) for o in branches], w_gate[l], b_gate[l], w_br[l], w_out[l],
                        ln1_g[l], ln1_b[l], bm=256)
        h = hierarchical_moe(h, w_rg[l], b_rg[l], w_re[l], b_re[l], w_up[l], w_down[l], ln2_g[l], ln2_b[l], bm=256)
    return h.reshape(b, s, d)[:, :s_len]
```

```python
import functools

import jax
import jax.numpy as jnp
import numpy as np
from jax import lax
from jax.experimental import pallas as pl
from jax.experimental.pallas import tpu as pltpu

DEPTH = 4
HEAD_DIM = 64
N_HEADS = 4
BRANCH_W = N_HEADS * HEAD_DIM
N_BRANCHES = 5
MOBA_BLOCK = 256
MOBA_TOPK = 3
MLA_Q_RANK = 256
MLA_KV_RANK = 128
MLA_NOPE = 64
MLA_ROPE = 32
MLA_V = 64
ROPE_THETA = 10000.0
NSA_CMP_LEN = 32
NSA_CMP_STRIDE = 16
NSA_SEL_LEN = 64
NSA_TOPN = 16
NSA_WINDOW = 512
NSA_PHI_HIDDEN = 128
N_GROUPS = 4
EXPERTS_PER_GROUP = 4
N_EXPERTS = N_GROUPS * EXPERTS_PER_GROUP
D_EXPERT = 256
DEEPNORM_ALPHA = (2.0 * DEPTH) ** 0.25
LN_EPS = 1e-5
RMS_EPS = 1e-6
NEG = -1e30
BIG = 1e30

LANES = 128
BF16_SUBLANES = 16
VALUE_SLOT = HEAD_DIM + BF16_SUBLANES
QUERY_BLOCK = 256
KEY_TILE = 256
SOFTMAX_KEY_TILE = 512
SB_KEY_TILE = 512
LOG2E = 1.4426950408889634
VMEM_LIMIT_BYTES = 56 * 1024 * 1024

F32 = jnp.float32
BF16 = jnp.bfloat16
HIGHEST = lax.Precision.HIGHEST

COL_SB = 0
COL_MOBA = 768
COL_MLA = 1536
COL_NSA_CMP = 2048
COL_NSA_G = 2176
COL_NSA_SLC = 2304
COL_NSA_WIN = 2432
COL_NSA_Q = 2560
COL_MEM_Q = 2816
IN_PAD = 3072
COL_F32 = COL_MLA
F32_W = COL_NSA_SLC - COL_MLA


def _params(semantics):
    return pltpu.CompilerParams(dimension_semantics=semantics, vmem_limit_bytes=VMEM_LIMIT_BYTES)


def _dot(a, b):
    return jnp.dot(a, b, preferred_element_type=F32)


def _dot_exact(a, b):
    return jnp.dot(a, b, preferred_element_type=F32, precision=HIGHEST)


def _sigmoid(x):
    return 1.0 / (1.0 + jnp.exp(-x))


def _resident(shape):
    zeros = (0,) * len(shape)
    return pl.BlockSpec(shape, lambda *_: zeros, pipeline_mode=pl.Buffered(1))


def _in_proj_kernel(x_ref, w_ref, o_ref, f_ref):
    y = _dot(x_ref[...].astype(BF16), w_ref[...])
    o_ref[...] = y.astype(BF16)
    f_ref[...] = y[:, COL_F32:COL_F32 + F32_W]


def in_projection(h, w, bm):
    t, d = h.shape
    n = w.shape[1]
    return pl.pallas_call(
        _in_proj_kernel,
        grid=(t // bm,),
        in_specs=[pl.BlockSpec((bm, d), lambda i: (i, 0)), _resident((d, n))],
        out_specs=[pl.BlockSpec((bm, n), lambda i: (i, 0)), pl.BlockSpec((bm, F32_W), lambda i: (i, 0))],
        out_shape=[jax.ShapeDtypeStruct((t, n), BF16), jax.ShapeDtypeStruct((t, F32_W), F32)],
        compiler_params=_params(("parallel",)),
        name="in_projection",
    )(h, w)


def _mm_kernel(x_ref, w_ref, o_ref):
    o_ref[...] = _dot(x_ref[...].astype(BF16), w_ref[...]).astype(o_ref.dtype)


def matmul(x, w, *, bm, out_dtype=F32):
    m, k = x.shape
    n = w.shape[1]
    return pl.pallas_call(
        _mm_kernel,
        grid=(m // bm,),
        in_specs=[pl.BlockSpec((bm, k), lambda i: (i, 0)), _resident((k, n))],
        out_specs=pl.BlockSpec((bm, n), lambda i: (i, 0)),
        out_shape=jax.ShapeDtypeStruct((m, n), out_dtype),
        compiler_params=_params(("parallel",)),
        name="matmul",
    )(x, w)


def _softmax_tile(s, v_slot, m, acc, shift=None):
    top = jnp.max(s, axis=0, keepdims=True)
    m_new = jnp.maximum(m, top if shift is None else top - shift)
    p = jnp.exp2(s - (m_new if shift is None else m_new + shift))
    return m_new, jnp.exp2(m - m_new) * acc + _dot(v_slot, p.astype(BF16))


def _softmax_init(q):
    return (jnp.full((1, q), NEG, F32), jnp.zeros((VALUE_SLOT, q), F32))


def _softmax_result(acc):
    return acc[:HEAD_DIM] / acc[HEAD_DIM:HEAD_DIM + 1]


def _causal_softmax(qi, qb, kt, heads, scores, values, finish, finish_last, raw_ref):
    def tile(j, state, fin):
        return tuple(_softmax_tile(fin[h][0], values(j, h), *state[h], shift=fin[h][1]) for h in range(heads))

    def prefetch(j, slot):
        for h, raw in enumerate(scores(j)):
            raw_ref[slot, h] = raw

    def held(slot):
        return tuple(raw_ref[slot, h] for h in range(heads))

    def pair(i, state):
        prefetch(2 * i + 1, 1)
        state = tile(2 * i, state, finish(2 * i, held(0)))
        prefetch(2 * i + 2, 0)
        return tile(2 * i + 1, state, finish(2 * i + 1, held(1)))

    def odd_tail(state):
        prefetch(last, 1)
        state = tile(last - 1, state, finish(last - 1, held(0)))
        return tile(last, state, finish_last(last, held(1)))

    def even_tail(state):
        return tile(last, state, finish_last(last, held(0)))

    last = lax.div(qi * qb, kt)
    prefetch(0, 0)
    state = tuple(_softmax_init(qb) for _ in range(heads))
    state = lax.fori_loop(0, lax.div(last, 2), pair, state)
    state = lax.cond(lax.rem(last, 2) == 1, odd_tail, even_tail, state)
    return [_softmax_result(acc) for _, acc in state]


def _transposed(x):
    return x.astype(F32).T


def _pair_rows(qt_all, h):
    g = qt_all[(h // 2) * LANES:(h // 2 + 1) * LANES]
    row = lax.broadcasted_iota(jnp.int32, g.shape, 0)
    keep = (row >= HEAD_DIM) if h % 2 else (row < HEAD_DIM)
    return jnp.where(keep, g, 0.0).astype(BF16)


def _pair_cols(h):
    return slice((h // 2) * LANES, (h // 2 + 1) * LANES)


def _head_rows(h):
    return slice(h * HEAD_DIM, (h + 1) * HEAD_DIM)


def _fill_transposed(src_ref, dst_ref, kb, extra=None):
    def chunk(c, _):
        r0 = pl.multiple_of(c * kb, kb)
        x = src_ref[0, pl.ds(r0, kb), :].astype(F32)
        dst_ref[:, pl.ds(r0, kb)] = x.T.astype(BF16)
        if extra is not None:
            extra(c, x)
        return 0
    lax.fori_loop(0, src_ref.shape[1] // kb, chunk, 0)


def _fill_value_slots(src_ref, dst_ref, kb, first_rows):
    ones = jnp.ones((BF16_SUBLANES, kb), BF16)

    def chunk(c, _):
        r0 = pl.multiple_of(c * kb, kb)
        xt = src_ref[0, pl.ds(r0, kb), :].astype(F32).T
        for h, r in enumerate(first_rows):
            dst_ref[h * VALUE_SLOT:h * VALUE_SLOT + HEAD_DIM, pl.ds(r0, kb)] = xt[r:r + HEAD_DIM].astype(BF16)
            dst_ref[h * VALUE_SLOT + HEAD_DIM:(h + 1) * VALUE_SLOT, pl.ds(r0, kb)] = ones
        return 0
    lax.fori_loop(0, src_ref.shape[1] // kb, chunk, 0)


def _value_slot(h):
    return slice(h * VALUE_SLOT, (h + 1) * VALUE_SLOT)


def _store_heads(o_ref, outs):
    o_ref[0] = jnp.concatenate(outs, axis=0).T.astype(o_ref.dtype)


def _qkv_specs(s, kb, col):
    c = col // BRANCH_W
    return [pl.BlockSpec((1, kb, BRANCH_W), lambda bi, qi: (bi, qi, c)),
            pl.BlockSpec((1, s, BRANCH_W), lambda bi, qi: (bi, 0, c + 1)),
            pl.BlockSpec((1, s, BRANCH_W), lambda bi, qi: (bi, 0, c + 2))]


def _sb_kernel(q_ref, k_ref, v_ref, tri_ref, o_ref, vt_ref, raw_ref, *, kt):
    qi = pl.program_id(1)
    qb = q_ref.shape[1]
    kb = KEY_TILE
    pieces = kt // kb

    @pl.when(qi == 0)
    def _():
        _fill_transposed(v_ref, vt_ref, kb)

    qt_all = _transposed(q_ref[0])
    qts = [_pair_rows(qt_all, h) for h in range(N_HEADS)]
    tri = tri_ref[...]
    rel = lax.broadcasted_iota(jnp.int32, (kb, qb), 0) - lax.broadcasted_iota(jnp.int32, (kb, qb), 1)
    order = [(u, h) for u in reversed(range(pieces)) for h in range(N_HEADS)]

    def prefetch(j, slot):
        for u, h in order:
            raw_ref[slot, h, u * kb:(u + 1) * kb] = _dot(
                k_ref[0, pl.ds(pl.multiple_of(j * kt + u * kb, kb), kb), _pair_cols(h)], qts[h])

    def tile(j, state, slot, last):
        k0 = {u: pl.multiple_of(j * kt + u * kb, kb) for u in range(pieces)}
        log_beta, later, col_sum, past = {}, {}, {}, {}
        for u, h in order:
            z = raw_ref[slot, h, u * kb:(u + 1) * kb]
            nz = -z
            soft = jnp.log(1.0 + jnp.exp2(jnp.minimum(z, nz))) * LOG2E
            log_keep = jnp.minimum(nz, 0.0) - soft
            log_beta[u, h] = log_keep + z
            if last:
                past[u] = rel < qi * qb - k0[u]
                log_keep = jnp.where(past[u], log_keep, 0.0)
            sums = _dot(tri, log_keep.astype(BF16))
            later[u, h] = sums[:kb]
            col_sum[u, h] = sums[kb:kb + 1]
        state = list(state)
        for u, h in order:
            carry, acc = state[h]
            a = jnp.exp2(log_beta[u, h] + later[u, h])
            if last:
                a = jnp.where(past[u], a, 0.0)
            part = _dot(vt_ref[_head_rows(h), pl.ds(k0[u], kb)], a.astype(BF16))
            state[h] = (carry + col_sum[u, h], acc + jnp.exp2(carry) * part)
        return tuple(state)

    def pair(i, state):
        first = last - 1 - 2 * i
        prefetch(first - 1, 0)
        state = tile(first, state, 1, False)
        prefetch(jnp.maximum(first - 2, 0), 1)
        return tile(first - 1, state, 0, False)

    last = lax.div(qi * qb, kt)
    prefetch(last, 0)
    prefetch(jnp.maximum(last - 1, 0), 1)
    state = tuple((jnp.zeros((1, qb), F32), jnp.zeros((HEAD_DIM, qb), F32)) for _ in range(N_HEADS))
    state = tile(last, state, 0, True)
    state = lax.fori_loop(0, lax.div(last, 2), pair, state)
    state = lax.cond(lax.rem(last, 2) == 1, lambda st: tile(0, st, 1, False), lambda st: st, state)
    _store_heads(o_ref, [st[1] for st in state])


def sb_attention(a):
    b, s, _ = a.shape
    kb, qb = KEY_TILE, QUERY_BLOCK
    tri = jnp.asarray(np.concatenate([np.triu(np.ones((kb, kb), np.float32), 1), np.ones((BF16_SUBLANES, kb), np.float32)]),
                      BF16)
    return pl.pallas_call(
        functools.partial(_sb_kernel, kt=min(SB_KEY_TILE, s)),
        grid=(b, s // qb),
        in_specs=_qkv_specs(s, qb, COL_SB) + [_resident(tri.shape)],
        out_specs=pl.BlockSpec((1, qb, BRANCH_W), lambda bi, qi: (bi, qi, 0)),
        out_shape=jax.ShapeDtypeStruct((b, s, BRANCH_W), BF16),
        scratch_shapes=[pltpu.VMEM((BRANCH_W, s), BF16), pltpu.VMEM((2, N_HEADS, min(SB_KEY_TILE, s), qb), F32)],
        compiler_params=_params(("parallel", "arbitrary")),
        name="sb_attention",
    )(a, a, a, tri)


def _alibi_slopes(n):
    return np.power(2.0, -8.0 * np.arange(1, n + 1, dtype=np.float64) / n).astype(np.float32)


def _select_top(score, ids, count, floor):
    def step(_, c):
        score, sel = c
        mx = jnp.max(score, axis=0, keepdims=True)
        idx = jnp.min(jnp.where(score == mx, ids, 1e9), axis=0, keepdims=True)
        pick = ids == idx
        sel = jnp.where(jnp.logical_and(pick, mx > floor), 1.0, sel)
        score = jnp.where(pick, -jnp.inf, score)
        return score, sel
    return lax.fori_loop(0, count, step, (score, jnp.zeros_like(score)))[1]


def _moba_kernel(slope_ref, q_ref, k_ref, v_ref, o_ref, vt_ref, km_ref, sel_ref, raw_ref, *, kt, topk):
    qi = pl.program_id(1)
    kb = q_ref.shape[1]
    nblk = km_ref.shape[0]
    per_tile = kt // kb

    @pl.when(qi == 0)
    def _():
        def key_mean(c, _):
            km_ref[pl.ds(c, 1), :] = jnp.mean(k_ref[0, pl.ds(pl.multiple_of(c * kb, kb), kb), :].astype(F32),
                                              axis=0, keepdims=True)
            return 0
        _fill_value_slots(v_ref, vt_ref, kb, [h * HEAD_DIM for h in range(N_HEADS)])
        lax.fori_loop(0, nblk, key_mean, 0)

    qt_all = _transposed(q_ref[0])
    qts = [_pair_rows(qt_all, h) for h in range(N_HEADS)]
    ids = lax.broadcasted_iota(jnp.int32, (nblk, kb), 0)
    row = lax.broadcasted_iota(jnp.int32, qt_all.shape, 0)
    km = km_ref[...]
    for h in range(N_HEADS):
        in_head = jnp.logical_and(row >= h * HEAD_DIM, row < (h + 1) * HEAD_DIM)
        gscore = _dot_exact(km, jnp.where(in_head, qt_all, 0.0))
        gscore = jnp.where(ids < qi, gscore, NEG)
        sel_ref[h] = jnp.where(_select_top(gscore, ids.astype(F32), topk, 0.5 * NEG) > 0.5, 0.0, NEG)

    rk = lax.broadcasted_iota(jnp.int32, (kt, kb), 0)
    rel = rk - lax.broadcasted_iota(jnp.int32, (kt, kb), 1)
    bias = [slope_ref[h] * rel.astype(F32) for h in range(N_HEADS)]

    def scores(j):
        k0 = pl.multiple_of(j * kt, kt)
        return tuple(_dot(k_ref[0, pl.ds(k0, kt), _pair_cols(h)], qts[h]) for h in range(N_HEADS))

    def values(j, h):
        return vt_ref[_value_slot(h), pl.ds(pl.multiple_of(j * kt, kt), kt)]

    def unselected(j, h):
        rows = [jnp.broadcast_to(sel_ref[h, pl.ds(j * per_tile + r, 1), :], (kb, kb)) for r in range(per_tile)]
        return jnp.concatenate(rows, axis=0)

    def finish(j, raw):
        off = (qi * kb - j * kt).astype(F32)
        return tuple((raw[h] + bias[h] + unselected(j, h), slope_ref[h] * off) for h in range(N_HEADS))

    def finish_last(j, raw):
        d0 = qi * kb - j * kt
        own_causal = jnp.logical_and(rk >= d0, rel <= d0)
        out = []
        for h in range(N_HEADS):
            s = raw[h] + bias[h]
            out.append((jnp.where(own_causal, s, s + unselected(j, h)), slope_ref[h] * d0.astype(F32)))
        return tuple(out)

    _store_heads(o_ref, _causal_softmax(qi, kb, kt, N_HEADS, scores, values, finish, finish_last, raw_ref))


def moba_attention(a):
    b, s, _ = a.shape
    kb = MOBA_BLOCK
    nblk = s // kb
    topk = min(MOBA_TOPK, nblk - 1)
    slopes = jnp.asarray(_alibi_slopes(N_HEADS) * np.float32(LOG2E))
    c = COL_MOBA // BRANCH_W
    return pl.pallas_call(
        functools.partial(_moba_kernel, kt=min(SOFTMAX_KEY_TILE, s), topk=topk),
        grid_spec=pltpu.PrefetchScalarGridSpec(
            num_scalar_prefetch=1,
            grid=(b, nblk),
            in_specs=[pl.BlockSpec((1, kb, BRANCH_W), lambda bi, qi, sl: (bi, qi, c)),
                      pl.BlockSpec((1, s, BRANCH_W), lambda bi, qi, sl: (bi, 0, c + 1)),
                      pl.BlockSpec((1, s, BRANCH_W), lambda bi, qi, sl: (bi, 0, c + 2))],
            out_specs=pl.BlockSpec((1, kb, BRANCH_W), lambda bi, qi, sl: (bi, qi, 0)),
            scratch_shapes=[pltpu.VMEM((N_HEADS * VALUE_SLOT, s), BF16),
                            pltpu.VMEM((nblk, BRANCH_W), F32),
                            pltpu.VMEM((N_HEADS, nblk, kb), F32),
                            pltpu.VMEM((2, N_HEADS, min(SOFTMAX_KEY_TILE, s), kb), F32)]),
        out_shape=jax.ShapeDtypeStruct((b, s, BRANCH_W), BF16),
        compiler_params=_params(("parallel", "arbitrary")),
        name="moba_attention",
    )(slopes, a, a, a)


MLA_SLOT = 128


def _rms(x, g):
    return x * lax.rsqrt(jnp.mean(x * x, axis=-1, keepdims=True) + RMS_EPS) * g


def _mla_proj_kernel(x_ref, gq_ref, gkv_ref, wq_ref, wqr_ref, wkv_ref, cq_ref, sq_ref, ck_ref, sk_ref,
                     q_ref, kv_ref, kr_ref):
    x = x_ref[...]
    c_q = _rms(x[:, :MLA_Q_RANK], gq_ref[...]).astype(BF16)
    c_kv = _rms(x[:, MLA_Q_RANK:MLA_Q_RANK + MLA_KV_RANK], gkv_ref[...]).astype(BF16)
    q_ref[...] = (_dot(c_q, wq_ref[...]) * cq_ref[...] + _dot(c_q, wqr_ref[...]) * sq_ref[...]).astype(q_ref.dtype)
    kv_ref[...] = _dot(c_kv, wkv_ref[...]).astype(kv_ref.dtype)
    tail = x[:, MLA_Q_RANK + MLA_KV_RANK:]
    rot = pltpu.roll(tail, LANES - MLA_ROPE, axis=1)
    kr_ref[...] = (tail * ck_ref[...] + rot * sk_ref[...]).astype(kr_ref.dtype)


def _rope_tables(s_len):
    half = MLA_ROPE // 2
    freqs = jnp.power(ROPE_THETA, -jnp.arange(half, dtype=F32) / half)
    ang = jnp.arange(s_len).astype(F32)[:, None] * freqs
    cos = jnp.concatenate([jnp.cos(ang)] * 2, axis=-1)
    sin = jnp.concatenate([jnp.sin(ang)] * 2, axis=-1)
    zk = jnp.zeros((s_len, LANES - MLA_ROPE), F32)
    zq = jnp.zeros((s_len, MLA_SLOT - MLA_NOPE - MLA_ROPE), F32)
    cq = jnp.tile(jnp.concatenate([jnp.ones((s_len, MLA_NOPE), F32), cos, zq], axis=-1), (1, N_HEADS))
    sq = jnp.tile(jnp.concatenate([jnp.zeros((s_len, MLA_NOPE), F32), sin, zq], axis=-1), (1, N_HEADS))
    return cq, sq, jnp.concatenate([cos, zk], axis=-1), jnp.concatenate([sin, zk], axis=-1)


def _rotate_half_cols(w):
    half = w.shape[-1] // 2
    return jnp.concatenate([-w[..., half:], w[..., :half]], axis=-1)


def mla_project(f, g_cq, g_ckv, w_uq, w_ukv, s_len, bm):
    t = f.shape[0]
    wq = w_uq.reshape(MLA_Q_RANK, N_HEADS, MLA_NOPE + MLA_ROPE)
    pad = jnp.zeros((MLA_Q_RANK, N_HEADS, MLA_SLOT - MLA_NOPE - MLA_ROPE), wq.dtype)
    wq_s = jnp.concatenate([wq, pad], axis=-1).reshape(MLA_Q_RANK, -1)
    wqr_s = jnp.concatenate([jnp.zeros_like(wq[..., :MLA_NOPE]), _rotate_half_cols(wq[..., MLA_NOPE:]), pad],
                            axis=-1).reshape(MLA_Q_RANK, -1)
    cq, sq, ck, sk = _rope_tables(s_len)
    q_scale = (MLA_NOPE + MLA_ROPE) ** -0.5 * LOG2E
    nrow = s_len // bm
    row = lambda i: (i, 0)
    pos = lambda i: (i % nrow, 0)
    qw = N_HEADS * MLA_SLOT
    kvw = N_HEADS * (MLA_NOPE + MLA_V)
    return pl.pallas_call(
        _mla_proj_kernel,
        grid=(t // bm,),
        in_specs=[pl.BlockSpec((bm, 512), row),
                  _resident((1, MLA_Q_RANK)), _resident((1, MLA_KV_RANK)),
                  _resident((MLA_Q_RANK, qw)), _resident((MLA_Q_RANK, qw)), _resident((MLA_KV_RANK, kvw)),
                  pl.BlockSpec((bm, qw), pos), pl.BlockSpec((bm, qw), pos),
                  pl.BlockSpec((bm, LANES), pos), pl.BlockSpec((bm, LANES), pos)],
        out_specs=[pl.BlockSpec((bm, qw), row), pl.BlockSpec((bm, kvw), row), pl.BlockSpec((bm, LANES), row)],
        out_shape=[jax.ShapeDtypeStruct((t, qw), BF16), jax.ShapeDtypeStruct((t, kvw), BF16),
                   jax.ShapeDtypeStruct((t, LANES), BF16)],
        compiler_params=_params(("parallel",)),
        name="mla_project",
    )(f, g_cq.reshape(1, -1), g_ckv.reshape(1, -1), (wq_s * q_scale).astype(BF16), (wqr_s * q_scale).astype(BF16),
      w_ukv.astype(BF16), cq, sq, ck, sk)


def _mla_kernel(q_ref, kv_ref, kr_ref, o_ref, vt_ref, raw_ref, *, kt):
    qi = pl.program_id(1)
    qb = q_ref.shape[1]

    @pl.when(qi == 0)
    def _():
        _fill_value_slots(kv_ref, vt_ref, KEY_TILE, [h * LANES + MLA_NOPE for h in range(N_HEADS)])

    qt_all = _transposed(q_ref[0])
    qts = []
    for h in range(N_HEADS):
        g = qt_all[h * MLA_SLOT:(h + 1) * MLA_SLOT]
        row = lax.broadcasted_iota(jnp.int32, g.shape, 0)
        qts.append(jnp.concatenate([jnp.where(row < MLA_NOPE, g, 0.0), g[MLA_NOPE:], jnp.zeros_like(g[MLA_NOPE:])],
                                   axis=0).astype(BF16))
    rel = lax.broadcasted_iota(jnp.int32, (kt, qb), 0) - lax.broadcasted_iota(jnp.int32, (kt, qb), 1)

    def scores(j):
        k0 = pl.multiple_of(j * kt, kt)
        kr = kr_ref[0, pl.ds(k0, kt), :]
        return tuple(_dot(jnp.concatenate([kv_ref[0, pl.ds(k0, kt), h * LANES:(h + 1) * LANES], kr], axis=1), qts[h])
                     for h in range(N_HEADS))

    def values(j, h):
        return vt_ref[_value_slot(h), pl.ds(pl.multiple_of(j * kt, kt), kt)]

    def finish_last(j, raw):
        causal = rel <= qi * qb - j * kt
        return tuple((jnp.where(causal, s, NEG), None) for s in raw)

    outs = _causal_softmax(qi, qb, kt, N_HEADS, scores, values,
                           lambda j, raw: tuple((s, None) for s in raw), finish_last, raw_ref)
    _store_heads(o_ref, outs)


def mla_attention(q, kv, kr):
    b, s, qw = q.shape
    kb = QUERY_BLOCK
    return pl.pallas_call(
        functools.partial(_mla_kernel, kt=min(SOFTMAX_KEY_TILE, s)),
        grid=(b, s // kb),
        in_specs=[pl.BlockSpec((1, kb, qw), lambda bi, qi: (bi, qi, 0)),
                  pl.BlockSpec((1, s, kv.shape[2]), lambda bi, qi: (bi, 0, 0)),
                  pl.BlockSpec((1, s, LANES), lambda bi, qi: (bi, 0, 0))],
        out_specs=pl.BlockSpec((1, kb, BRANCH_W), lambda bi, qi: (bi, qi, 0)),
        out_shape=jax.ShapeDtypeStruct((b, s, BRANCH_W), BF16),
        scratch_shapes=[pltpu.VMEM((N_HEADS * VALUE_SLOT, s), BF16),
                        pltpu.VMEM((2, N_HEADS, min(SOFTMAX_KEY_TILE, s), kb), F32)],
        compiler_params=_params(("parallel", "arbitrary")),
        name="mla_attention",
    )(q, kv, kr)


def _cross_kernel(q_ref, kv_ref, o_ref):
    qt_all = _transposed(q_ref[0])
    kv = kv_ref[0]
    vt = _transposed(kv[:, BRANCH_W:]).astype(BF16)
    outs = []
    for h in range(N_HEADS):
        s = _dot(kv[:, _pair_cols(h)], _pair_rows(qt_all, h))
        p = jnp.exp2(s - jnp.max(s, axis=0, keepdims=True))
        outs.append(_dot(vt[_head_rows(h)], p.astype(BF16)) / jnp.sum(p, axis=0, keepdims=True))
    _store_heads(o_ref, outs)


def cross_attention(a, mkv, qb):
    b, s, _ = a.shape
    n = mkv.shape[1]
    c = COL_MEM_Q // BRANCH_W
    return pl.pallas_call(
        _cross_kernel,
        grid=(b, s // qb),
        in_specs=[pl.BlockSpec((1, qb, BRANCH_W), lambda bi, qi: (bi, qi, c)),
                  pl.BlockSpec((1, n, 2 * BRANCH_W), lambda bi, qi: (bi, 0, 0))],
        out_specs=pl.BlockSpec((1, qb, BRANCH_W), lambda bi, qi: (bi, qi, 0)),
        out_shape=jax.ShapeDtypeStruct((b, s, BRANCH_W), BF16),
        compiler_params=_params(("parallel", "parallel")),
        name="cross_attention",
    )(a, mkv)


def _compress_kernel(x_ref, pe_ref, w1_ref, w2_ref, kc_ref, vct_ref):
    n = kc_ref.shape[1]
    first = jnp.zeros((n, 2 * NSA_PHI_HIDDEN), F32)
    second = jnp.zeros((n, 2 * NSA_PHI_HIDDEN), F32)
    for r in range(NSA_CMP_STRIDE):
        x = x_ref[0, pl.ds(r, n, stride=NSA_CMP_STRIDE), :]
        first = first + _dot((x + pe_ref[r:r + 1]).astype(BF16), w1_ref[r])
        second = second + _dot((x + pe_ref[NSA_CMP_STRIDE + r:NSA_CMP_STRIDE + r + 1]).astype(BF16),
                               w1_ref[NSA_CMP_STRIDE + r])
    hidden = jax.nn.gelu(first + pltpu.roll(second, n - 1, axis=0))
    out = _dot(hidden.astype(BF16), w2_ref[...])
    kc_ref[0] = out.astype(BF16)
    vct_ref[0] = out.T[HEAD_DIM:].astype(BF16)


def _pair_diag(wk, wv):
    z = jnp.zeros_like(wk)
    return jnp.concatenate([jnp.concatenate([wk, z], axis=-1), jnp.concatenate([z, wv], axis=-1)], axis=-2)


def nsa_compress(f, nsa_pe, w_k1, w_k2, w_v1, w_v2):
    b, s, _ = f.shape
    n = s // NSA_CMP_STRIDE
    hd = HEAD_DIM
    w1 = _pair_diag(w_k1.reshape(NSA_CMP_LEN, hd, -1), w_v1.reshape(NSA_CMP_LEN, hd, -1)).astype(BF16)
    w2 = _pair_diag(w_k2, w_v2).astype(BF16)
    pe = jnp.concatenate([nsa_pe, nsa_pe], axis=-1)
    c = (COL_NSA_CMP - COL_F32) // LANES
    return pl.pallas_call(
        _compress_kernel,
        grid=(b,),
        in_specs=[pl.BlockSpec((1, s, LANES), lambda bi: (bi, 0, c)),
                  _resident(pe.shape), _resident(w1.shape), _resident(w2.shape)],
        out_specs=[pl.BlockSpec((1, n, LANES), lambda bi: (bi, 0, 0)),
                   pl.BlockSpec((1, hd, n), lambda bi: (bi, 0, 0))],
        out_shape=[jax.ShapeDtypeStruct((b, n, LANES), BF16), jax.ShapeDtypeStruct((b, hd, n), BF16)],
        compiler_params=_params(("parallel",)),
        name="nsa_compress",
    )(f, pe, w1, w2)


def _nsa_kernel(q_ref, g_ref, slope_ref, kc_ref, vct_ref, ovt_ref, slc_ref, win_ref, o_ref,
                vst_ref, vwt_ref, sel_ref, bias_ref, cbias_ref, m_ref, acc_ref, wanted_ref, *, topn, n_cmp):
    qi = pl.program_id(1)
    qn = q_ref.shape[1]
    kb = KEY_TILE
    lanes = N_HEADS * qn
    dv = HEAD_DIM
    q0 = qi * qn

    slope = slope_ref[...]
    ql = jnp.bitwise_and(lax.broadcasted_iota(jnp.int32, (1, lanes), 1), qn - 1)
    qpos = q0 + ql
    kid = lax.broadcasted_iota(jnp.int32, (kb, 1), 0)
    ncp = kc_ref.shape[1]
    cid = lax.broadcasted_iota(jnp.int32, (ncp, 1), 0)
    cmp_last = cid * NSA_CMP_STRIDE + (NSA_CMP_LEN - 1)
    cmp_end = jnp.where(cid < n_cmp, cmp_last, 1 << 30)

    @pl.when(qi == 0)
    def _():
        _fill_value_slots(slc_ref, vst_ref, kb, [HEAD_DIM])
        _fill_value_slots(win_ref, vwt_ref, kb, [HEAD_DIM])
        bias_ref[...] = slope * (ql - kid).astype(F32)
        cbias_ref[...] = slope * (ql - cmp_last).astype(F32)

    qt_all = _transposed(q_ref[0])
    qt = jnp.concatenate([qt_all[_head_rows(h)] for h in range(N_HEADS)], axis=1)
    qt = jnp.concatenate([qt, jnp.zeros_like(qt)], axis=0).astype(BF16)
    jd = lax.div(q0, kb)
    win_tiles = [jnp.maximum(jd - back, 0) for back in range(NSA_WINDOW // kb + 1)]

    def raw_scores(kv_ref, j):
        return _dot(kv_ref[0, pl.ds(pl.multiple_of(j * kb, kb), kb), :], qt)

    raw_c = _dot(kc_ref[0], qt)
    raw_d = raw_scores(slc_ref, jd)
    raw_w = [raw_scores(win_ref, j) for j in win_tiles]
    bias = bias_ref[...]
    causal = ql - kid >= 0

    valid = cmp_end - ql <= q0
    s = jnp.where(valid, raw_c - cbias_ref[...], NEG)
    e = jnp.where(valid, jnp.exp2(s - jnp.max(s, axis=0, keepdims=True)), 0.0)
    den = jnp.sum(e, axis=0, keepdims=True)
    p_c = e / jnp.where(den > 0.0, den, 1.0)
    o_c = _dot(vct_ref[0], p_c.astype(BF16))

    p_sum = p_c[:, 0:qn]
    for hh in range(1, N_HEADS):
        p_sum = p_sum + p_c[:, hh * qn:(hh + 1) * qn]
    imp = _dot_exact(ovt_ref[...], p_sum)
    nsel = imp.shape[0]
    sid = lax.broadcasted_iota(jnp.int32, (nsel, qn), 0)
    cur = jnp.right_shift(qpos[:, 0:qn], NSA_SEL_LEN.bit_length() - 1)
    forced = jnp.logical_or(sid == 0, sid == cur)
    score = jnp.where(forced, BIG, jnp.where(sid < cur, imp, NEG))
    sel = _select_top(score, sid.astype(F32), topn, 0.5 * NEG)
    sel_ref[...] = jnp.concatenate([jnp.where(sel > 0.5, 0.0, NEG)] * N_HEADS, axis=1)
    per_tile = kb // NSA_SEL_LEN
    for j in range(nsel // per_tile):
        wanted_ref[j] = (jnp.max(sel[j * per_tile:(j + 1) * per_tile]) > 0.5).astype(jnp.int32)

    def sel_tile(j, c, raw, own):
        rows = [jnp.broadcast_to(sel_ref[pl.ds(j * per_tile + r, 1), :], (NSA_SEL_LEN, lanes))
                for r in range(per_tile)]
        s = raw - bias + jnp.concatenate(rows, axis=0)
        if own:
            s = jnp.where(causal, s, NEG)
        shift = slope * ((jd - j) * kb).astype(F32)
        return _softmax_tile(s, vst_ref[:, pl.ds(pl.multiple_of(j * kb, kb), kb)], *c, shift=shift)

    m_ref[...], acc_ref[...] = sel_tile(jd, _softmax_init(lanes), raw_d, True)

    def maybe_sel_tile(j, _):
        @pl.when(wanted_ref[j] > 0)
        def _():
            m_ref[...], acc_ref[...] = sel_tile(j, (m_ref[...], acc_ref[...]), raw_scores(slc_ref, j), False)
        return 0

    lax.fori_loop(0, jd, maybe_sel_tile, 0)
    o_s = _softmax_result(acc_ref[...])

    tops, masked = [], []
    for back, raw in enumerate(raw_w):
        s = raw - bias
        if back == 0:
            s = jnp.where(causal, s, NEG)
        else:
            inside = NSA_WINDOW - back * kb if back * kb + kb > NSA_WINDOW else 2 * kb
            s = jnp.where(ql - kid < jnp.where(jd - back >= 0, inside, -2 * kb), s, NEG)
        masked.append(s)
        tops.append(jnp.max(s, axis=0, keepdims=True) - slope * float(back * kb))
    m_w = functools.reduce(jnp.maximum, tops)
    acc_w = jnp.zeros((VALUE_SLOT, lanes), F32)
    for back, s in enumerate(masked):
        p = jnp.exp2(s - (m_w + slope * float(back * kb)))
        acc_w = acc_w + _dot(vwt_ref[:, pl.ds(pl.multiple_of(win_tiles[back] * kb, kb), kb)], p.astype(BF16))
    o_w = _softmax_result(acc_w)

    gt = _sigmoid(g_ref[0]).T
    def gate(ci):
        return jnp.concatenate([gt[h * 3 + ci:h * 3 + ci + 1] for h in range(N_HEADS)], axis=1)
    out = gate(0) * o_c + gate(1) * o_s + gate(2) * o_w
    _store_heads(o_ref, [out[:, h * qn:(h + 1) * qn] for h in range(N_HEADS)])


def nsa_attention(a, f, kc, vct):
    b, s, _ = a.shape
    ncp = kc.shape[1]
    n_cmp = ncp - NSA_CMP_LEN // NSA_CMP_STRIDE + 1
    nsel = s // NSA_SEL_LEN
    qn = QUERY_BLOCK
    lanes = N_HEADS * qn
    cs = np.arange(ncp) * NSA_CMP_STRIDE
    ss = np.arange(nsel) * NSA_SEL_LEN
    ov = ((cs[:, None] < ss[None, :] + NSA_SEL_LEN) & (cs[:, None] + NSA_CMP_LEN > ss[None, :])
          & (np.arange(ncp)[:, None] < n_cmp)).astype(np.float32)
    slopes = jnp.asarray(np.repeat(_alibi_slopes(N_HEADS) * np.float32(LOG2E), qn)[None, :])
    per_b = lambda bi, qi: (bi, 0, 0)
    return pl.pallas_call(
        functools.partial(_nsa_kernel, topn=min(NSA_TOPN, nsel), n_cmp=n_cmp),
        grid=(b, s // qn),
        in_specs=[pl.BlockSpec((1, qn, BRANCH_W), lambda bi, qi: (bi, qi, COL_NSA_Q // BRANCH_W)),
                  pl.BlockSpec((1, qn, LANES), lambda bi, qi: (bi, qi, (COL_NSA_G - COL_F32) // LANES)),
                  _resident((1, lanes)),
                  pl.BlockSpec((1, ncp, LANES), per_b),
                  pl.BlockSpec((1, HEAD_DIM, ncp), per_b),
                  _resident((nsel, ncp)),
                  pl.BlockSpec((1, s, LANES), lambda bi, qi: (bi, 0, COL_NSA_SLC // LANES)),
                  pl.BlockSpec((1, s, LANES), lambda bi, qi: (bi, 0, COL_NSA_WIN // LANES))],
        out_specs=pl.BlockSpec((1, qn, BRANCH_W), lambda bi, qi: (bi, qi, 0)),
        out_shape=jax.ShapeDtypeStruct((b, s, BRANCH_W), BF16),
        scratch_shapes=[pltpu.VMEM((VALUE_SLOT, s), BF16), pltpu.VMEM((VALUE_SLOT, s), BF16),
                        pltpu.VMEM((nsel, lanes), F32), pltpu.VMEM((KEY_TILE, lanes), F32),
                        pltpu.VMEM((ncp, lanes), F32), pltpu.VMEM((1, lanes), F32),
                        pltpu.VMEM((VALUE_SLOT, lanes), F32), pltpu.SMEM((s // KEY_TILE,), jnp.int32)],
        compiler_params=_params(("parallel", "arbitrary")),
        name="nsa_attention",
    )(a, f, slopes, kc, vct, jnp.asarray(ov.T), a, a)


def _layer_norm(r, g, b):
    mu = jnp.mean(r, axis=-1, keepdims=True)
    c = r - mu
    var = jnp.mean(c * c, axis=-1, keepdims=True)
    return c * lax.rsqrt(var + LN_EPS) * g + b


def _merge_kernel(h_ref, o0_ref, o1_ref, o2_ref, o3_ref, o4_ref, wg_ref, bg_ref, wbr_ref, wout_ref, g_ref, b_ref,
                  out_ref):
    h = h_ref[...]
    hb = h.astype(BF16)
    merged = jnp.zeros(h.shape, F32)
    for i, o_ref in enumerate((o0_ref, o1_ref, o2_ref, o3_ref, o4_ref)):
        gate = _sigmoid(_dot(hb, wg_ref[i]) + bg_ref[i])
        merged = merged + gate * _dot(o_ref[...], wbr_ref[i])
    y = _dot(merged.astype(BF16), wout_ref[...])
    out_ref[...] = _layer_norm(DEEPNORM_ALPHA * h + y, g_ref[...], b_ref[...])


def gated_merge(h, branches, w_gate, b_gate, w_br, w_out, ln_g, ln_b, bm):
    t, d = h.shape
    nb, bw = len(branches), branches[0].shape[1]
    row = lambda i: (i, 0)
    return pl.pallas_call(
        _merge_kernel,
        grid=(t // bm,),
        in_specs=[pl.BlockSpec((bm, d), row)] + [pl.BlockSpec((bm, bw), row)] * nb
                 + [_resident((nb, d, d)), _resident((nb, 1, d)), _resident((nb, bw, d)), _resident((d, d)),
                    _resident((1, d)), _resident((1, d))],
        out_specs=pl.BlockSpec((bm, d), row),
        out_shape=jax.ShapeDtypeStruct((t, d), F32),
        compiler_params=_params(("parallel",)),
        name="gated_merge",
    )(h, *branches, w_gate.astype(BF16), b_gate.reshape(nb, 1, d), w_br.astype(BF16), w_out.astype(BF16),
      ln_g.reshape(1, d), ln_b.reshape(1, d))


ROUTER_LANES = 128


def _moe_kernel(h_ref, wr_ref, br_ref, wup_ref, wdn_ref, g_ref, b_ref, out_ref, hid_ref):
    h = h_ref[...]
    hb = h.astype(BF16)
    bm = h.shape[0]
    logits = _dot_exact(h, wr_ref[...]) + br_ref[...]
    lane = lax.broadcasted_iota(jnp.int32, (bm, ROUTER_LANES), 1)
    lane_f = lane.astype(F32)
    is_g = lane < N_GROUPS
    glog = jnp.where(is_g, logits, NEG)
    gmax = jnp.max(glog, axis=-1, keepdims=True)
    g_sel = jnp.min(jnp.where(glog == gmax, lane_f, 1e9), axis=-1, keepdims=True)
    pg_sel = 1.0 / jnp.sum(jnp.where(is_g, jnp.exp(glog - gmax), 0.0), axis=-1, keepdims=True)
    lo = N_GROUPS + g_sel * EXPERTS_PER_GROUP
    in_grp = jnp.logical_and(lane_f >= lo, lane_f < lo + EXPERTS_PER_GROUP)
    elog = jnp.where(in_grp, logits, NEG)
    emax = jnp.max(elog, axis=-1, keepdims=True)
    ee = jnp.where(in_grp, jnp.exp(elog - emax), 0.0)
    pe = ee / jnp.sum(ee, axis=-1, keepdims=True)
    pe_m = jnp.where(in_grp, pe, -1.0)
    v1 = jnp.max(pe_m, axis=-1, keepdims=True)
    i1 = jnp.min(jnp.where(pe_m == v1, lane_f, 1e9), axis=-1, keepdims=True)
    pe_m2 = jnp.where(lane_f == i1, -1.0, pe_m)
    v2 = jnp.max(pe_m2, axis=-1, keepdims=True)
    i2 = jnp.min(jnp.where(pe_m2 == v2, lane_f, 1e9), axis=-1, keepdims=True)
    norm = pg_sel / (v1 + v2)
    gate = jnp.where(lane_f == i1, v1 * norm, jnp.where(lane_f == i2, v2 * norm, 0.0))
    for e in range(N_EXPERTS):
        au = _dot(hb, wup_ref[e])
        a, u = au[:, :D_EXPERT], au[:, D_EXPERT:]
        w_e = jnp.sum(jnp.where(lane == N_GROUPS + e, gate, 0.0), axis=-1, keepdims=True)
        hid_ref[:, e * D_EXPERT:(e + 1) * D_EXPERT] = (w_e * (a * _sigmoid(a) * u)).astype(BF16)
    y = _dot(hid_ref[...], wdn_ref[...])
    out_ref[...] = _layer_norm(DEEPNORM_ALPHA * h + y, g_ref[...], b_ref[...])


def hierarchical_moe(h, w_rg, b_rg, w_re, b_re, w_up, w_down, ln_g, ln_b, bm):
    t, d = h.shape
    ne = N_EXPERTS
    w_r = jnp.concatenate([w_rg, w_re.transpose(1, 0, 2).reshape(d, ne)], axis=1)
    w_r = jnp.pad(w_r, ((0, 0), (0, ROUTER_LANES - w_r.shape[1])))
    b_r = jnp.pad(jnp.concatenate([b_rg, b_re.reshape(ne)]), (0, ROUTER_LANES - N_GROUPS - ne)).reshape(1, -1)
    row = lambda i: (i, 0)
    return pl.pallas_call(
        _moe_kernel,
        grid=(t // bm,),
        in_specs=[pl.BlockSpec((bm, d), row), _resident((d, ROUTER_LANES)), _resident((1, ROUTER_LANES)),
                  _resident((ne, d, 2 * D_EXPERT)), _resident((ne * D_EXPERT, d)), _resident((1, d)),
                  _resident((1, d))],
        out_specs=pl.BlockSpec((bm, d), row),
        out_shape=jax.ShapeDtypeStruct((t, d), F32),
        scratch_shapes=[pltpu.VMEM((bm, ne * D_EXPERT), BF16)],
        compiler_params=_params(("parallel",)),
        name="hierarchical_moe",
    )(h, w_r, b_r, w_up.astype(BF16), w_down.reshape(ne * D_EXPERT, d).astype(BF16),
      ln_g.reshape(1, d), ln_b.reshape(1, d))


def _pad_in_weight(w_in):
    d = w_in.shape[0]
    sizes = (768, 768, MLA_Q_RANK, MLA_KV_RANK, MLA_ROPE, 256, 384, 12, 256)
    offs = np.concatenate([[0], np.cumsum(sizes)])
    sb, moba, c_q, c_kv, k_rope, nsa_q, nsa_kv, nsa_g, mem_q = (w_in[:, offs[i]:offs[i + 1]] for i in range(len(sizes)))
    z = lambda n: jnp.zeros((d, n), w_in.dtype)
    q_scale = HEAD_DIM ** -0.5 * LOG2E
    scale_q = lambda qkv: jnp.concatenate([qkv[:, :BRANCH_W] * q_scale, qkv[:, BRANCH_W:]], axis=1)
    return jnp.concatenate([scale_q(sb), scale_q(moba), c_q, c_kv, k_rope, _rotate_half_cols(k_rope), z(64),
                            nsa_kv[:, :128], nsa_g, z(116), nsa_kv[:, 128:], nsa_q * q_scale, mem_q * q_scale], axis=1)


def kernel(x, mem, w_in, g_cq, g_ckv, w_uq, w_ukv, nsa_pe, w_phi_k1, w_phi_k2, w_phi_v1, w_phi_v2, w_mem_kv, w_br,
           w_gate, b_gate, w_out, ln1_g, ln1_b, w_rg, b_rg, w_re, b_re, w_up, w_down, ln2_g, ln2_b):
    b, s_len, d = x.shape
    s = -(-s_len // MOBA_BLOCK) * MOBA_BLOCK
    t = b * s
    n_mem = mem.shape[1]
    h = jnp.pad(x, ((0, 0), (0, s - s_len), (0, 0))).reshape(t, d)
    for l in range(w_in.shape[0]):
        a, f = in_projection(h, _pad_in_weight(w_in[l]).astype(BF16), bm=512)
        a3, f3 = a.reshape(b, s, IN_PAD), f.reshape(b, s, F32_W)
        q, kv, kr = mla_project(f, g_cq[l], g_ckv[l], w_uq[l], w_ukv[l], s, bm=512)
        kc, vct = nsa_compress(f3, nsa_pe[l], w_phi_k1[l], w_phi_k2[l], w_phi_v1[l], w_phi_v2[l])
        mkv = matmul(mem.reshape(b * n_mem, d), w_mem_kv[l].astype(BF16), bm=n_mem, out_dtype=BF16)
        branches = [sb_attention(a3),
                    moba_attention(a3),
                    mla_attention(q.reshape(b, s, -1), kv.reshape(b, s, -1), kr.reshape(b, s, -1)),
                    nsa_attention(a3, f3, kc, vct),
                    cross_attention(a3, mkv.reshape(b, n_mem, -1), qb=512)]
        h = gated_merge(h, [o.reshape(t, BRANCH_W) for o in branches], w_gate[l], b_gate[l], w_br[l], w_out[l],
                        ln1_g[l], ln1_b[l], bm=256)
        h = hierarchical_moe(h, w_rg[l], b_rg[l], w_re[l], b_re[l], w_up[l], w_down[l], ln2_g[l], ln2_b[l], bm=256)
    return h.reshape(b, s, d)[:, :s_len]
```

```python
import functools

import jax
import jax.numpy as jnp
import numpy as np
from jax import lax
from jax.experimental import pallas as pl
from jax.experimental.pallas import tpu as pltpu

DEPTH = 4
HEAD_DIM = 64
N_HEADS = 4
BRANCH_W = N_HEADS * HEAD_DIM
N_BRANCHES = 5
MOBA_BLOCK = 256
MOBA_TOPK = 3
MLA_Q_RANK = 256
MLA_KV_RANK = 128
MLA_NOPE = 64
MLA_ROPE = 32
MLA_V = 64
ROPE_THETA = 10000.0
NSA_CMP_LEN = 32
NSA_CMP_STRIDE = 16
NSA_SEL_LEN = 64
NSA_TOPN = 16
NSA_WINDOW = 512
NSA_PHI_HIDDEN = 128
N_GROUPS = 4
EXPERTS_PER_GROUP = 4
N_EXPERTS = N_GROUPS * EXPERTS_PER_GROUP
D_EXPERT = 256
DEEPNORM_ALPHA = (2.0 * DEPTH) ** 0.25
LN_EPS = 1e-5
RMS_EPS = 1e-6
NEG = -1e30
BIG = 1e30

LANES = 128
BF16_SUBLANES = 16
VALUE_SLOT = HEAD_DIM + BF16_SUBLANES
QUERY_BLOCK = 256
KEY_TILE = 256
SOFTMAX_KEY_TILE = 512
SB_KEY_TILE = 512
LOG2E = 1.4426950408889634
VMEM_LIMIT_BYTES = 56 * 1024 * 1024

F32 = jnp.float32
BF16 = jnp.bfloat16
HIGHEST = lax.Precision.HIGHEST

COL_SB = 0
COL_MOBA = 768
COL_MLA = 1536
COL_NSA_CMP = 2048
COL_NSA_G = 2176
COL_NSA_SLC = 2304
COL_NSA_WIN = 2432
COL_NSA_Q = 2560
COL_MEM_Q = 2816
IN_PAD = 3072
COL_F32 = COL_MLA
F32_W = COL_NSA_SLC - COL_MLA


def _params(semantics):
    return pltpu.CompilerParams(dimension_semantics=semantics, vmem_limit_bytes=VMEM_LIMIT_BYTES)


def _dot(a, b):
    return jnp.dot(a, b, preferred_element_type=F32)


def _dot_exact(a, b):
    return jnp.dot(a, b, preferred_element_type=F32, precision=HIGHEST)


def _sigmoid(x):
    return 1.0 / (1.0 + jnp.exp(-x))


def _resident(shape):
    zeros = (0,) * len(shape)
    return pl.BlockSpec(shape, lambda *_: zeros, pipeline_mode=pl.Buffered(1))


def _in_proj_kernel(x_ref, w_ref, o_ref, f_ref):
    y = _dot(x_ref[...].astype(BF16), w_ref[...])
    o_ref[...] = y.astype(BF16)
    f_ref[...] = y[:, COL_F32:COL_F32 + F32_W]


def in_projection(h, w, bm):
    t, d = h.shape
    n = w.shape[1]
    return pl.pallas_call(
        _in_proj_kernel,
        grid=(t // bm,),
        in_specs=[pl.BlockSpec((bm, d), lambda i: (i, 0)), _resident((d, n))],
        out_specs=[pl.BlockSpec((bm, n), lambda i: (i, 0)), pl.BlockSpec((bm, F32_W), lambda i: (i, 0))],
        out_shape=[jax.ShapeDtypeStruct((t, n), BF16), jax.ShapeDtypeStruct((t, F32_W), F32)],
        compiler_params=_params(("parallel",)),
        name="in_projection",
    )(h, w)


def _mm_kernel(x_ref, w_ref, o_ref):
    o_ref[...] = _dot(x_ref[...].astype(BF16), w_ref[...]).astype(o_ref.dtype)


def matmul(x, w, *, bm, out_dtype=F32):
    m, k = x.shape
    n = w.shape[1]
    return pl.pallas_call(
        _mm_kernel,
        grid=(m // bm,),
        in_specs=[pl.BlockSpec((bm, k), lambda i: (i, 0)), _resident((k, n))],
        out_specs=pl.BlockSpec((bm, n), lambda i: (i, 0)),
        out_shape=jax.ShapeDtypeStruct((m, n), out_dtype),
        compiler_params=_params(("parallel",)),
        name="matmul",
    )(x, w)


def _softmax_tile(s, v_slot, m, acc, shift=None):
    top = jnp.max(s, axis=0, keepdims=True)
    m_new = jnp.maximum(m, top if shift is None else top - shift)
    p = jnp.exp2(s - (m_new if shift is None else m_new + shift))
    return m_new, jnp.exp2(m - m_new) * acc + _dot(v_slot, p.astype(BF16))


def _softmax_init(q):
    return (jnp.full((1, q), NEG, F32), jnp.zeros((VALUE_SLOT, q), F32))


def _softmax_result(acc):
    return acc[:HEAD_DIM] / acc[HEAD_DIM:HEAD_DIM + 1]


def _causal_softmax(qi, qb, kt, heads, scores, values, finish, finish_last, raw_ref):
    def tile(j, state, fin):
        return tuple(_softmax_tile(fin[h][0], values(j, h), *state[h], shift=fin[h][1]) for h in range(heads))

    def prefetch(j, slot):
        for h, raw in enumerate(scores(j)):
            raw_ref[slot, h] = raw

    def held(slot):
        return tuple(raw_ref[slot, h] for h in range(heads))

    def pair(i, state):
        prefetch(2 * i + 1, 1)
        state = tile(2 * i, state, finish(2 * i, held(0)))
        prefetch(2 * i + 2, 0)
        return tile(2 * i + 1, state, finish(2 * i + 1, held(1)))

    def odd_tail(state):
        prefetch(last, 1)
        state = tile(last - 1, state, finish(last - 1, held(0)))
        return tile(last, state, finish_last(last, held(1)))

    def even_tail(state):
        return tile(last, state, finish_last(last, held(0)))

    last = lax.div(qi * qb, kt)
    prefetch(0, 0)
    state = tuple(_softmax_init(qb) for _ in range(heads))
    state = lax.fori_loop(0, lax.div(last, 2), pair, state)
    state = lax.cond(lax.rem(last, 2) == 1, odd_tail, even_tail, state)
    return [_softmax_result(acc) for _, acc in state]


def _transposed(x):
    return x.astype(F32).T


def _pair_rows(qt_all, h):
    g = qt_all[(h // 2) * LANES:(h // 2 + 1) * LANES]
    row = lax.broadcasted_iota(jnp.int32, g.shape, 0)
    keep = (row >= HEAD_DIM) if h % 2 else (row < HEAD_DIM)
    return jnp.where(keep, g, 0.0).astype(BF16)


def _pair_cols(h):
    return slice((h // 2) * LANES, (h // 2 + 1) * LANES)


def _head_rows(h):
    return slice(h * HEAD_DIM, (h + 1) * HEAD_DIM)


def _fill_transposed(src_ref, dst_ref, kb, extra=None):
    def chunk(c, _):
        r0 = pl.multiple_of(c * kb, kb)
        x = src_ref[0, pl.ds(r0, kb), :].astype(F32)
        dst_ref[:, pl.ds(r0, kb)] = x.T.astype(BF16)
        if extra is not None:
            extra(c, x)
        return 0
    lax.fori_loop(0, src_ref.shape[1] // kb, chunk, 0)


def _fill_value_slots(src_ref, dst_ref, kb, first_rows):
    ones = jnp.ones((BF16_SUBLANES, kb), BF16)

    def chunk(c, _):
        r0 = pl.multiple_of(c * kb, kb)
        xt = src_ref[0, pl.ds(r0, kb), :].astype(F32).T
        for h, r in enumerate(first_rows):
            dst_ref[h * VALUE_SLOT:h * VALUE_SLOT + HEAD_DIM, pl.ds(r0, kb)] = xt[r:r + HEAD_DIM].astype(BF16)
            dst_ref[h * VALUE_SLOT + HEAD_DIM:(h + 1) * VALUE_SLOT, pl.ds(r0, kb)] = ones
        return 0
    lax.fori_loop(0, src_ref.shape[1] // kb, chunk, 0)


def _value_slot(h):
    return slice(h * VALUE_SLOT, (h + 1) * VALUE_SLOT)


def _store_heads(o_ref, outs):
    o_ref[0] = jnp.concatenate(outs, axis=0).T.astype(o_ref.dtype)


def _qkv_specs(s, kb, col):
    c = col // BRANCH_W
    return [pl.BlockSpec((1, kb, BRANCH_W), lambda bi, qi: (bi, qi, c)),
            pl.BlockSpec((1, s, BRANCH_W), lambda bi, qi: (bi, 0, c + 1)),
            pl.BlockSpec((1, s, BRANCH_W), lambda bi, qi: (bi, 0, c + 2))]


def _sb_kernel(q_ref, k_ref, v_ref, tri_ref, o_ref, vt_ref, raw_ref, *, kt):
    qi = pl.program_id(1)
    qb = q_ref.shape[1]
    kb = KEY_TILE
    pieces = kt // kb

    @pl.when(qi == 0)
    def _():
        _fill_transposed(v_ref, vt_ref, kb)

    qt_all = _transposed(q_ref[0])
    qts = [_pair_rows(qt_all, h) for h in range(N_HEADS)]
    tri = tri_ref[...]
    rel = lax.broadcasted_iota(jnp.int32, (kb, qb), 0) - lax.broadcasted_iota(jnp.int32, (kb, qb), 1)
    order = [(u, h) for u in reversed(range(pieces)) for h in range(N_HEADS)]

    def prefetch(j, slot):
        for u, h in order:
            raw_ref[slot, h, u * kb:(u + 1) * kb] = _dot(
                k_ref[0, pl.ds(pl.multiple_of(j * kt + u * kb, kb), kb), _pair_cols(h)], qts[h])

    def tile(j, state, slot, last):
        k0 = {u: pl.multiple_of(j * kt + u * kb, kb) for u in range(pieces)}
        log_beta, later, col_sum, past = {}, {}, {}, {}
        for u, h in order:
            z = raw_ref[slot, h, u * kb:(u + 1) * kb]
            nz = -z
            soft = jnp.log(1.0 + jnp.exp2(jnp.minimum(z, nz))) * LOG2E
            log_keep = jnp.minimum(nz, 0.0) - soft
            log_beta[u, h] = log_keep + z
            if last:
                past[u] = rel < qi * qb - k0[u]
                log_keep = jnp.where(past[u], log_keep, 0.0)
            sums = _dot(tri, log_keep.astype(BF16))
            later[u, h] = sums[:kb]
            col_sum[u, h] = sums[kb:kb + 1]
        state = list(state)
        for u, h in order:
            carry, acc = state[h]
            a = jnp.exp2(log_beta[u, h] + later[u, h])
            if last:
                a = jnp.where(past[u], a, 0.0)
            part = _dot(vt_ref[_head_rows(h), pl.ds(k0[u], kb)], a.astype(BF16))
            state[h] = (carry + col_sum[u, h], acc + jnp.exp2(carry) * part)
        return tuple(state)

    def pair(i, state):
        first = last - 1 - 2 * i
        prefetch(first - 1, 0)
        state = tile(first, state, 1, False)
        prefetch(jnp.maximum(first - 2, 0), 1)
        return tile(first - 1, state, 0, False)

    last = lax.div(qi * qb, kt)
    prefetch(last, 0)
    prefetch(jnp.maximum(last - 1, 0), 1)
    state = tuple((jnp.zeros((1, qb), F32), jnp.zeros((HEAD_DIM, qb), F32)) for _ in range(N_HEADS))
    state = tile(last, state, 0, True)
    state = lax.fori_loop(0, lax.div(last, 2), pair, state)
    state = lax.cond(lax.rem(last, 2) == 1, lambda st: tile(0, st, 1, False), lambda st: st, state)
    _store_heads(o_ref, [st[1] for st in state])


def sb_attention(a):
    b, s, _ = a.shape
    kb, qb = KEY_TILE, QUERY_BLOCK
    tri = jnp.asarray(np.concatenate([np.triu(np.ones((kb, kb), np.float32), 1), np.ones((BF16_SUBLANES, kb), np.float32)]),
                      BF16)
    return pl.pallas_call(
        functools.partial(_sb_kernel, kt=min(SB_KEY_TILE, s)),
        grid=(b, s // qb),
        in_specs=_qkv_specs(s, qb, COL_SB) + [_resident(tri.shape)],
        out_specs=pl.BlockSpec((1, qb, BRANCH_W), lambda bi, qi: (bi, qi, 0)),
        out_shape=jax.ShapeDtypeStruct((b, s, BRANCH_W), BF16),
        scratch_shapes=[pltpu.VMEM((BRANCH_W, s), BF16), pltpu.VMEM((2, N_HEADS, min(SB_KEY_TILE, s), qb), F32)],
        compiler_params=_params(("parallel", "arbitrary")),
        name="sb_attention",
    )(a, a, a, tri)


def _alibi_slopes(n):
    return np.power(2.0, -8.0 * np.arange(1, n + 1, dtype=np.float64) / n).astype(np.float32)


def _select_top(scores, ids, count, floor):
    def step(_, c):
        out = []
        for score, sel in c:
            mx = jnp.max(score, axis=0, keepdims=True)
            idx = jnp.min(jnp.where(score == mx, ids, 1e9), axis=0, keepdims=True)
            pick = ids == idx
            out.append((jnp.where(pick, -jnp.inf, score), jnp.where(jnp.logical_and(pick, mx > floor), 1.0, sel)))
        return tuple(out)
    done = lax.fori_loop(0, count, step, tuple((s, jnp.zeros_like(s)) for s in scores))
    return [sel for _, sel in done]


def _moba_kernel(slope_ref, q_ref, k_ref, v_ref, o_ref, vt_ref, km_ref, sel_ref, raw_ref, bias_ref, *, kt, topk):
    qi = pl.program_id(1)
    kb = q_ref.shape[1]
    nblk = km_ref.shape[0]
    per_tile = kt // kb
    rk = lax.broadcasted_iota(jnp.int32, (kt, kb), 0)
    rel = rk - lax.broadcasted_iota(jnp.int32, (kt, kb), 1)

    @pl.when(qi == 0)
    def _():
        def key_mean(c, _):
            km_ref[pl.ds(c, 1), :] = jnp.mean(k_ref[0, pl.ds(pl.multiple_of(c * kb, kb), kb), :].astype(F32),
                                              axis=0, keepdims=True)
            return 0
        _fill_value_slots(v_ref, vt_ref, kb, [h * HEAD_DIM for h in range(N_HEADS)])
        lax.fori_loop(0, nblk, key_mean, 0)
        for h in range(N_HEADS):
            bias_ref[h] = slope_ref[h] * rel.astype(F32)

    qt_all = _transposed(q_ref[0])
    qts = [_pair_rows(qt_all, h) for h in range(N_HEADS)]
    ids = lax.broadcasted_iota(jnp.int32, (nblk, kb), 0)
    row = lax.broadcasted_iota(jnp.int32, qt_all.shape, 0)
    km = km_ref[...]
    gscores = []
    for h in range(N_HEADS):
        in_head = jnp.logical_and(row >= h * HEAD_DIM, row < (h + 1) * HEAD_DIM)
        gscore = _dot_exact(km, jnp.where(in_head, qt_all, 0.0))
        gscores.append(jnp.where(ids < qi, gscore, NEG))
    for h, sel in enumerate(_select_top(gscores, ids.astype(F32), topk, 0.5 * NEG)):
        sel_ref[h] = jnp.where(sel > 0.5, 0.0, NEG)


    def scores(j):
        k0 = pl.multiple_of(j * kt, kt)
        return tuple(_dot(k_ref[0, pl.ds(k0, kt), _pair_cols(h)], qts[h]) for h in range(N_HEADS))

    def values(j, h):
        return vt_ref[_value_slot(h), pl.ds(pl.multiple_of(j * kt, kt), kt)]

    def unselected(j, h):
        rows = [jnp.broadcast_to(sel_ref[h, pl.ds(j * per_tile + r, 1), :], (kb, kb)) for r in range(per_tile)]
        return jnp.concatenate(rows, axis=0)

    def finish(j, raw):
        off = (qi * kb - j * kt).astype(F32)
        return tuple((raw[h] + bias_ref[h] + unselected(j, h), slope_ref[h] * off) for h in range(N_HEADS))

    def finish_last(j, raw):
        d0 = qi * kb - j * kt
        own_causal = jnp.logical_and(rk >= d0, rel <= d0)
        out = []
        for h in range(N_HEADS):
            s = raw[h] + bias_ref[h]
            out.append((jnp.where(own_causal, s, s + unselected(j, h)), slope_ref[h] * d0.astype(F32)))
        return tuple(out)

    _store_heads(o_ref, _causal_softmax(qi, kb, kt, N_HEADS, scores, values, finish, finish_last, raw_ref))


def moba_attention(a):
    b, s, _ = a.shape
    kb = MOBA_BLOCK
    nblk = s // kb
    topk = min(MOBA_TOPK, nblk - 1)
    slopes = jnp.asarray(_alibi_slopes(N_HEADS) * np.float32(LOG2E))
    c = COL_MOBA // BRANCH_W
    return pl.pallas_call(
        functools.partial(_moba_kernel, kt=min(SOFTMAX_KEY_TILE, s), topk=topk),
        grid_spec=pltpu.PrefetchScalarGridSpec(
            num_scalar_prefetch=1,
            grid=(b, nblk),
            in_specs=[pl.BlockSpec((1, kb, BRANCH_W), lambda bi, qi, sl: (bi, qi, c)),
                      pl.BlockSpec((1, s, BRANCH_W), lambda bi, qi, sl: (bi, 0, c + 1)),
                      pl.BlockSpec((1, s, BRANCH_W), lambda bi, qi, sl: (bi, 0, c + 2))],
            out_specs=pl.BlockSpec((1, kb, BRANCH_W), lambda bi, qi, sl: (bi, qi, 0)),
            scratch_shapes=[pltpu.VMEM((N_HEADS * VALUE_SLOT, s), BF16),
                            pltpu.VMEM((nblk, BRANCH_W), F32),
                            pltpu.VMEM((N_HEADS, nblk, kb), F32),
                            pltpu.VMEM((2, N_HEADS, min(SOFTMAX_KEY_TILE, s), kb), F32),
                            pltpu.VMEM((N_HEADS, min(SOFTMAX_KEY_TILE, s), kb), F32)]),
        out_shape=jax.ShapeDtypeStruct((b, s, BRANCH_W), BF16),
        compiler_params=_params(("parallel", "arbitrary")),
        name="moba_attention",
    )(slopes, a, a, a)


MLA_SLOT = 128


def _rms(x, g):
    return x * lax.rsqrt(jnp.mean(x * x, axis=-1, keepdims=True) + RMS_EPS) * g


def _mla_proj_kernel(x_ref, gq_ref, gkv_ref, wq_ref, wqr_ref, wkv_ref, cq_ref, sq_ref, ck_ref, sk_ref,
                     q_ref, kv_ref, kr_ref):
    x = x_ref[...]
    c_q = _rms(x[:, :MLA_Q_RANK], gq_ref[...]).astype(BF16)
    c_kv = _rms(x[:, MLA_Q_RANK:MLA_Q_RANK + MLA_KV_RANK], gkv_ref[...]).astype(BF16)
    q_ref[...] = (_dot(c_q, wq_ref[...]) * cq_ref[...] + _dot(c_q, wqr_ref[...]) * sq_ref[...]).astype(q_ref.dtype)
    kv_ref[...] = _dot(c_kv, wkv_ref[...]).astype(kv_ref.dtype)
    tail = x[:, MLA_Q_RANK + MLA_KV_RANK:]
    rot = pltpu.roll(tail, LANES - MLA_ROPE, axis=1)
    kr_ref[...] = (tail * ck_ref[...] + rot * sk_ref[...]).astype(kr_ref.dtype)


def _rope_tables(s_len):
    half = MLA_ROPE // 2
    freqs = jnp.power(ROPE_THETA, -jnp.arange(half, dtype=F32) / half)
    ang = jnp.arange(s_len).astype(F32)[:, None] * freqs
    cos = jnp.concatenate([jnp.cos(ang)] * 2, axis=-1)
    sin = jnp.concatenate([jnp.sin(ang)] * 2, axis=-1)
    zk = jnp.zeros((s_len, LANES - MLA_ROPE), F32)
    zq = jnp.zeros((s_len, MLA_SLOT - MLA_NOPE - MLA_ROPE), F32)
    cq = jnp.tile(jnp.concatenate([jnp.ones((s_len, MLA_NOPE), F32), cos, zq], axis=-1), (1, N_HEADS))
    sq = jnp.tile(jnp.concatenate([jnp.zeros((s_len, MLA_NOPE), F32), sin, zq], axis=-1), (1, N_HEADS))
    return cq, sq, jnp.concatenate([cos, zk], axis=-1), jnp.concatenate([sin, zk], axis=-1)


def _rotate_half_cols(w):
    half = w.shape[-1] // 2
    return jnp.concatenate([-w[..., half:], w[..., :half]], axis=-1)


def mla_project(f, g_cq, g_ckv, w_uq, w_ukv, s_len, bm):
    t = f.shape[0]
    wq = w_uq.reshape(MLA_Q_RANK, N_HEADS, MLA_NOPE + MLA_ROPE)
    pad = jnp.zeros((MLA_Q_RANK, N_HEADS, MLA_SLOT - MLA_NOPE - MLA_ROPE), wq.dtype)
    wq_s = jnp.concatenate([wq, pad], axis=-1).reshape(MLA_Q_RANK, -1)
    wqr_s = jnp.concatenate([jnp.zeros_like(wq[..., :MLA_NOPE]), _rotate_half_cols(wq[..., MLA_NOPE:]), pad],
                            axis=-1).reshape(MLA_Q_RANK, -1)
    cq, sq, ck, sk = _rope_tables(s_len)
    q_scale = (MLA_NOPE + MLA_ROPE) ** -0.5 * LOG2E
    nrow = s_len // bm
    row = lambda i: (i, 0)
    pos = lambda i: (i % nrow, 0)
    qw = N_HEADS * MLA_SLOT
    kvw = N_HEADS * (MLA_NOPE + MLA_V)
    return pl.pallas_call(
        _mla_proj_kernel,
        grid=(t // bm,),
        in_specs=[pl.BlockSpec((bm, 512), row),
                  _resident((1, MLA_Q_RANK)), _resident((1, MLA_KV_RANK)),
                  _resident((MLA_Q_RANK, qw)), _resident((MLA_Q_RANK, qw)), _resident((MLA_KV_RANK, kvw)),
                  pl.BlockSpec((bm, qw), pos), pl.BlockSpec((bm, qw), pos),
                  pl.BlockSpec((bm, LANES), pos), pl.BlockSpec((bm, LANES), pos)],
        out_specs=[pl.BlockSpec((bm, qw), row), pl.BlockSpec((bm, kvw), row), pl.BlockSpec((bm, LANES), row)],
        out_shape=[jax.ShapeDtypeStruct((t, qw), BF16), jax.ShapeDtypeStruct((t, kvw), BF16),
                   jax.ShapeDtypeStruct((t, LANES), BF16)],
        compiler_params=_params(("parallel",)),
        name="mla_project",
    )(f, g_cq.reshape(1, -1), g_ckv.reshape(1, -1), (wq_s * q_scale).astype(BF16), (wqr_s * q_scale).astype(BF16),
      w_ukv.astype(BF16), cq, sq, ck, sk)


def _mla_kernel(q_ref, kv_ref, kr_ref, o_ref, vt_ref, raw_ref, *, kt):
    qi = pl.program_id(1)
    qb = q_ref.shape[1]

    @pl.when(qi == 0)
    def _():
        _fill_value_slots(kv_ref, vt_ref, KEY_TILE, [h * LANES + MLA_NOPE for h in range(N_HEADS)])

    qt_all = _transposed(q_ref[0])
    qts = []
    for h in range(N_HEADS):
        g = qt_all[h * MLA_SLOT:(h + 1) * MLA_SLOT]
        row = lax.broadcasted_iota(jnp.int32, g.shape, 0)
        qts.append(jnp.concatenate([jnp.where(row < MLA_NOPE, g, 0.0), g[MLA_NOPE:], jnp.zeros_like(g[MLA_NOPE:])],
                                   axis=0).astype(BF16))
    rel = lax.broadcasted_iota(jnp.int32, (kt, qb), 0) - lax.broadcasted_iota(jnp.int32, (kt, qb), 1)

    def scores(j):
        k0 = pl.multiple_of(j * kt, kt)
        kr = kr_ref[0, pl.ds(k0, kt), :]
        return tuple(_dot(jnp.concatenate([kv_ref[0, pl.ds(k0, kt), h * LANES:(h + 1) * LANES], kr], axis=1), qts[h])
                     for h in range(N_HEADS))

    def values(j, h):
        return vt_ref[_value_slot(h), pl.ds(pl.multiple_of(j * kt, kt), kt)]

    def finish_last(j, raw):
        causal = rel <= qi * qb - j * kt
        return tuple((jnp.where(causal, s, NEG), None) for s in raw)

    outs = _causal_softmax(qi, qb, kt, N_HEADS, scores, values,
                           lambda j, raw: tuple((s, None) for s in raw), finish_last, raw_ref)
    _store_heads(o_ref, outs)


def mla_attention(q, kv, kr):
    b, s, qw = q.shape
    kb = QUERY_BLOCK
    return pl.pallas_call(
        functools.partial(_mla_kernel, kt=min(SOFTMAX_KEY_TILE, s)),
        grid=(b, s // kb),
        in_specs=[pl.BlockSpec((1, kb, qw), lambda bi, qi: (bi, qi, 0)),
                  pl.BlockSpec((1, s, kv.shape[2]), lambda bi, qi: (bi, 0, 0)),
                  pl.BlockSpec((1, s, LANES), lambda bi, qi: (bi, 0, 0))],
        out_specs=pl.BlockSpec((1, kb, BRANCH_W), lambda bi, qi: (bi, qi, 0)),
        out_shape=jax.ShapeDtypeStruct((b, s, BRANCH_W), BF16),
        scratch_shapes=[pltpu.VMEM((N_HEADS * VALUE_SLOT, s), BF16),
                        pltpu.VMEM((2, N_HEADS, min(SOFTMAX_KEY_TILE, s), kb), F32)],
        compiler_params=_params(("parallel", "arbitrary")),
        name="mla_attention",
    )(q, kv, kr)


def _cross_kernel(q_ref, kv_ref, o_ref):
    qt_all = _transposed(q_ref[0])
    kv = kv_ref[0]
    vt = _transposed(kv[:, BRANCH_W:]).astype(BF16)
    outs = []
    for h in range(N_HEADS):
        s = _dot(kv[:, _pair_cols(h)], _pair_rows(qt_all, h))
        p = jnp.exp2(s - jnp.max(s, axis=0, keepdims=True))
        outs.append(_dot(vt[_head_rows(h)], p.astype(BF16)) / jnp.sum(p, axis=0, keepdims=True))
    _store_heads(o_ref, outs)


def cross_attention(a, mkv, qb):
    b, s, _ = a.shape
    n = mkv.shape[1]
    c = COL_MEM_Q // BRANCH_W
    return pl.pallas_call(
        _cross_kernel,
        grid=(b, s // qb),
        in_specs=[pl.BlockSpec((1, qb, BRANCH_W), lambda bi, qi: (bi, qi, c)),
                  pl.BlockSpec((1, n, 2 * BRANCH_W), lambda bi, qi: (bi, 0, 0))],
        out_specs=pl.BlockSpec((1, qb, BRANCH_W), lambda bi, qi: (bi, qi, 0)),
        out_shape=jax.ShapeDtypeStruct((b, s, BRANCH_W), BF16),
        compiler_params=_params(("parallel", "parallel")),
        name="cross_attention",
    )(a, mkv)


def _compress_kernel(x_ref, pe_ref, w1_ref, w2_ref, kc_ref, vct_ref):
    n = kc_ref.shape[1]
    first = jnp.zeros((n, 2 * NSA_PHI_HIDDEN), F32)
    second = jnp.zeros((n, 2 * NSA_PHI_HIDDEN), F32)
    for r in range(NSA_CMP_STRIDE):
        x = x_ref[0, pl.ds(r, n, stride=NSA_CMP_STRIDE), :]
        first = first + _dot((x + pe_ref[r:r + 1]).astype(BF16), w1_ref[r])
        second = second + _dot((x + pe_ref[NSA_CMP_STRIDE + r:NSA_CMP_STRIDE + r + 1]).astype(BF16),
                               w1_ref[NSA_CMP_STRIDE + r])
    hidden = jax.nn.gelu(first + pltpu.roll(second, n - 1, axis=0))
    out = _dot(hidden.astype(BF16), w2_ref[...])
    kc_ref[0] = out.astype(BF16)
    vct_ref[0] = out.T[HEAD_DIM:].astype(BF16)


def _pair_diag(wk, wv):
    z = jnp.zeros_like(wk)
    return jnp.concatenate([jnp.concatenate([wk, z], axis=-1), jnp.concatenate([z, wv], axis=-1)], axis=-2)


def nsa_compress(f, nsa_pe, w_k1, w_k2, w_v1, w_v2):
    b, s, _ = f.shape
    n = s // NSA_CMP_STRIDE
    hd = HEAD_DIM
    w1 = _pair_diag(w_k1.reshape(NSA_CMP_LEN, hd, -1), w_v1.reshape(NSA_CMP_LEN, hd, -1)).astype(BF16)
    w2 = _pair_diag(w_k2, w_v2).astype(BF16)
    pe = jnp.concatenate([nsa_pe, nsa_pe], axis=-1)
    c = (COL_NSA_CMP - COL_F32) // LANES
    return pl.pallas_call(
        _compress_kernel,
        grid=(b,),
        in_specs=[pl.BlockSpec((1, s, LANES), lambda bi: (bi, 0, c)),
                  _resident(pe.shape), _resident(w1.shape), _resident(w2.shape)],
        out_specs=[pl.BlockSpec((1, n, LANES), lambda bi: (bi, 0, 0)),
                   pl.BlockSpec((1, hd, n), lambda bi: (bi, 0, 0))],
        out_shape=[jax.ShapeDtypeStruct((b, n, LANES), BF16), jax.ShapeDtypeStruct((b, hd, n), BF16)],
        compiler_params=_params(("parallel",)),
        name="nsa_compress",
    )(f, pe, w1, w2)


def _nsa_kernel(q_ref, g_ref, slope_ref, kc_ref, vct_ref, ovt_ref, slc_ref, win_ref, o_ref,
                vst_ref, vwt_ref, sel_ref, bias_ref, cbias_ref, raw_ref, order_ref, *, topn, n_cmp):
    qi = pl.program_id(1)
    qn = q_ref.shape[1]
    kb = KEY_TILE
    lanes = N_HEADS * qn
    dv = HEAD_DIM
    q0 = qi * qn

    slope = slope_ref[...]
    ql = jnp.bitwise_and(lax.broadcasted_iota(jnp.int32, (1, lanes), 1), qn - 1)
    qpos = q0 + ql
    kid = lax.broadcasted_iota(jnp.int32, (kb, 1), 0)
    ncp = kc_ref.shape[1]
    cid = lax.broadcasted_iota(jnp.int32, (ncp, 1), 0)
    cmp_last = cid * NSA_CMP_STRIDE + (NSA_CMP_LEN - 1)
    cmp_end = jnp.where(cid < n_cmp, cmp_last, 1 << 30)

    @pl.when(qi == 0)
    def _():
        _fill_value_slots(slc_ref, vst_ref, kb, [HEAD_DIM])
        _fill_value_slots(win_ref, vwt_ref, kb, [HEAD_DIM])
        bias_ref[...] = slope * (ql - kid).astype(F32)
        cbias_ref[...] = slope * (ql - cmp_last).astype(F32)

    qt_all = _transposed(q_ref[0])
    qt = jnp.concatenate([qt_all[_head_rows(h)] for h in range(N_HEADS)], axis=1)
    qt = jnp.concatenate([qt, jnp.zeros_like(qt)], axis=0).astype(BF16)
    jd = lax.div(q0, kb)
    win_tiles = [jnp.maximum(jd - back, 0) for back in range(NSA_WINDOW // kb + 1)]

    def raw_scores(kv_ref, j):
        return _dot(kv_ref[0, pl.ds(pl.multiple_of(j * kb, kb), kb), :], qt)

    raw_c = _dot(kc_ref[0], qt)
    raw_d = raw_scores(slc_ref, jd)
    raw_w = [raw_scores(win_ref, j) for j in win_tiles]
    bias = bias_ref[...]
    causal = ql - kid >= 0

    valid = cmp_end - ql <= q0
    s = jnp.where(valid, raw_c - cbias_ref[...], NEG)
    top = jnp.max(s, axis=0, keepdims=True)
    e = jnp.exp2(s - top)
    p_c = e * jnp.where(top > 0.5 * NEG, 1.0 / jnp.sum(e, axis=0, keepdims=True), 0.0)
    o_c = _dot(vct_ref[0], p_c.astype(BF16))

    p_sum = p_c[:, 0:qn]
    for hh in range(1, N_HEADS):
        p_sum = p_sum + p_c[:, hh * qn:(hh + 1) * qn]
    imp = _dot_exact(ovt_ref[...], p_sum)
    nsel = imp.shape[0]
    sid = lax.broadcasted_iota(jnp.int32, (nsel, qn), 0)
    cur = jnp.right_shift(qpos[:, 0:qn], NSA_SEL_LEN.bit_length() - 1)
    forced = jnp.logical_or(sid == 0, sid == cur)
    score = jnp.where(forced, BIG, jnp.where(sid < cur, imp, NEG))
    sel, = _select_top([score], sid.astype(F32), topn, 0.5 * NEG)
    sel_ref[...] = jnp.concatenate([jnp.where(sel > 0.5, 0.0, NEG)] * N_HEADS, axis=1)
    per_tile = kb // NSA_SEL_LEN
    n = jnp.int32(0)
    for j in range(nsel // per_tile):
        order_ref[n] = j
        wanted = jnp.max(sel[j * per_tile:(j + 1) * per_tile]) > 0.5
        n = n + jnp.logical_and(wanted, j < jd).astype(jnp.int32)

    def sel_tile(j, c, raw, own):
        rows = [jnp.broadcast_to(sel_ref[pl.ds(j * per_tile + r, 1), :], (NSA_SEL_LEN, lanes))
                for r in range(per_tile)]
        s = raw - bias + jnp.concatenate(rows, axis=0)
        if own:
            s = jnp.where(causal, s, NEG)
        shift = slope * ((jd - j) * kb).astype(F32)
        return _softmax_tile(s, vst_ref[:, pl.ds(pl.multiple_of(j * kb, kb), kb)], *c, shift=shift)

    def fetch(k, slot):
        raw_ref[slot] = raw_scores(slc_ref, order_ref[k])

    def work(k, slot, c):
        return sel_tile(order_ref[k], c, raw_ref[slot], False)

    def pair(i, c):
        fetch(2 * i + 1, 1)
        c = work(2 * i, 0, c)
        fetch(jnp.minimum(2 * i + 2, n), 0)
        return work(2 * i + 1, 1, c)

    fetch(0, 0)
    c = sel_tile(jd, _softmax_init(lanes), raw_d, True)
    c = lax.fori_loop(0, lax.div(n, 2), pair, c)
    c = lax.cond(lax.rem(n, 2) == 1, lambda c: work(n - 1, 0, c), lambda c: c, c)
    o_s = _softmax_result(c[1])

    tops, masked = [], []
    for back, raw in enumerate(raw_w):
        s = raw - bias
        if back == 0:
            s = jnp.where(causal, s, NEG)
        else:
            inside = NSA_WINDOW - back * kb if back * kb + kb > NSA_WINDOW else 2 * kb
            s = jnp.where(ql - kid < jnp.where(jd - back >= 0, inside, -2 * kb), s, NEG)
        masked.append(s)
        tops.append(jnp.max(s, axis=0, keepdims=True) - slope * float(back * kb))
    m_w = functools.reduce(jnp.maximum, tops)
    acc_w = jnp.zeros((VALUE_SLOT, lanes), F32)
    for back, s in enumerate(masked):
        p = jnp.exp2(s - (m_w + slope * float(back * kb)))
        acc_w = acc_w + _dot(vwt_ref[:, pl.ds(pl.multiple_of(win_tiles[back] * kb, kb), kb)], p.astype(BF16))
    o_w = _softmax_result(acc_w)

    gt = _sigmoid(g_ref[0]).T
    def gate(ci):
        return jnp.concatenate([gt[h * 3 + ci:h * 3 + ci + 1] for h in range(N_HEADS)], axis=1)
    out = gate(0) * o_c + gate(1) * o_s + gate(2) * o_w
    _store_heads(o_ref, [out[:, h * qn:(h + 1) * qn] for h in range(N_HEADS)])


def nsa_attention(a, f, kc, vct):
    b, s, _ = a.shape
    ncp = kc.shape[1]
    n_cmp = ncp - NSA_CMP_LEN // NSA_CMP_STRIDE + 1
    nsel = s // NSA_SEL_LEN
    qn = QUERY_BLOCK
    lanes = N_HEADS * qn
    cs = np.arange(ncp) * NSA_CMP_STRIDE
    ss = np.arange(nsel) * NSA_SEL_LEN
    ov = ((cs[:, None] < ss[None, :] + NSA_SEL_LEN) & (cs[:, None] + NSA_CMP_LEN > ss[None, :])
          & (np.arange(ncp)[:, None] < n_cmp)).astype(np.float32)
    slopes = jnp.asarray(np.repeat(_alibi_slopes(N_HEADS) * np.float32(LOG2E), qn)[None, :])
    per_b = lambda bi, qi: (bi, 0, 0)
    return pl.pallas_call(
        functools.partial(_nsa_kernel, topn=min(NSA_TOPN, nsel), n_cmp=n_cmp),
        grid=(b, s // qn),
        in_specs=[pl.BlockSpec((1, qn, BRANCH_W), lambda bi, qi: (bi, qi, COL_NSA_Q // BRANCH_W)),
                  pl.BlockSpec((1, qn, LANES), lambda bi, qi: (bi, qi, (COL_NSA_G - COL_F32) // LANES)),
                  _resident((1, lanes)),
                  pl.BlockSpec((1, ncp, LANES), per_b),
                  pl.BlockSpec((1, HEAD_DIM, ncp), per_b),
                  _resident((nsel, ncp)),
                  pl.BlockSpec((1, s, LANES), lambda bi, qi: (bi, 0, COL_NSA_SLC // LANES)),
                  pl.BlockSpec((1, s, LANES), lambda bi, qi: (bi, 0, COL_NSA_WIN // LANES))],
        out_specs=pl.BlockSpec((1, qn, BRANCH_W), lambda bi, qi: (bi, qi, 0)),
        out_shape=jax.ShapeDtypeStruct((b, s, BRANCH_W), BF16),
        scratch_shapes=[pltpu.VMEM((VALUE_SLOT, s), BF16), pltpu.VMEM((VALUE_SLOT, s), BF16),
                        pltpu.VMEM((nsel, lanes), F32), pltpu.VMEM((KEY_TILE, lanes), F32),
                        pltpu.VMEM((ncp, lanes), F32), pltpu.VMEM((2, KEY_TILE, lanes), F32),
                        pltpu.SMEM((s // KEY_TILE + 1,), jnp.int32)],
        compiler_params=_params(("parallel", "arbitrary")),
        name="nsa_attention",
    )(a, f, slopes, kc, vct, jnp.asarray(ov.T), a, a)


def _layer_norm(r, g, b):
    mu = jnp.mean(r, axis=-1, keepdims=True)
    c = r - mu
    var = jnp.mean(c * c, axis=-1, keepdims=True)
    return c * lax.rsqrt(var + LN_EPS) * g + b


def _merge_kernel(h_ref, o0_ref, o1_ref, o2_ref, o3_ref, o4_ref, wg_ref, bg_ref, wbr_ref, wout_ref, g_ref, b_ref,
                  out_ref):
    h = h_ref[...]
    hb = h.astype(BF16)
    merged = jnp.zeros(h.shape, F32)
    for i, o_ref in enumerate((o0_ref, o1_ref, o2_ref, o3_ref, o4_ref)):
        gate = _sigmoid(_dot(hb, wg_ref[i]) + bg_ref[i])
        merged = merged + gate * _dot(o_ref[...], wbr_ref[i])
    y = _dot(merged.astype(BF16), wout_ref[...])
    out_ref[...] = _layer_norm(DEEPNORM_ALPHA * h + y, g_ref[...], b_ref[...])


def gated_merge(h, branches, w_gate, b_gate, w_br, w_out, ln_g, ln_b, bm):
    t, d = h.shape
    nb, bw = len(branches), branches[0].shape[1]
    row = lambda i: (i, 0)
    return pl.pallas_call(
        _merge_kernel,
        grid=(t // bm,),
        in_specs=[pl.BlockSpec((bm, d), row)] + [pl.BlockSpec((bm, bw), row)] * nb
                 + [_resident((nb, d, d)), _resident((nb, 1, d)), _resident((nb, bw, d)), _resident((d, d)),
                    _resident((1, d)), _resident((1, d))],
        out_specs=pl.BlockSpec((bm, d), row),
        out_shape=jax.ShapeDtypeStruct((t, d), F32),
        compiler_params=_params(("parallel",)),
        name="gated_merge",
    )(h, *branches, w_gate.astype(BF16), b_gate.reshape(nb, 1, d), w_br.astype(BF16), w_out.astype(BF16),
      ln_g.reshape(1, d), ln_b.reshape(1, d))


ROUTER_LANES = 128


def _moe_kernel(h_ref, wr_ref, br_ref, wup_ref, wdn_ref, g_ref, b_ref, out_ref, hid_ref):
    h = h_ref[...]
    hb = h.astype(BF16)
    bm = h.shape[0]
    logits = _dot_exact(h, wr_ref[...]) + br_ref[...]
    lane = lax.broadcasted_iota(jnp.int32, (bm, ROUTER_LANES), 1)
    lane_f = lane.astype(F32)
    is_g = lane < N_GROUPS
    glog = jnp.where(is_g, logits, NEG)
    gmax = jnp.max(glog, axis=-1, keepdims=True)
    g_sel = jnp.min(jnp.where(glog == gmax, lane_f, 1e9), axis=-1, keepdims=True)
    pg_sel = 1.0 / jnp.sum(jnp.where(is_g, jnp.exp(glog - gmax), 0.0), axis=-1, keepdims=True)
    lo = N_GROUPS + g_sel * EXPERTS_PER_GROUP
    in_grp = jnp.logical_and(lane_f >= lo, lane_f < lo + EXPERTS_PER_GROUP)
    elog = jnp.where(in_grp, logits, NEG)
    emax = jnp.max(elog, axis=-1, keepdims=True)
    ee = jnp.where(in_grp, jnp.exp(elog - emax), 0.0)
    pe = ee / jnp.sum(ee, axis=-1, keepdims=True)
    pe_m = jnp.where(in_grp, pe, -1.0)
    v1 = jnp.max(pe_m, axis=-1, keepdims=True)
    i1 = jnp.min(jnp.where(pe_m == v1, lane_f, 1e9), axis=-1, keepdims=True)
    pe_m2 = jnp.where(lane_f == i1, -1.0, pe_m)
    v2 = jnp.max(pe_m2, axis=-1, keepdims=True)
    i2 = jnp.min(jnp.where(pe_m2 == v2, lane_f, 1e9), axis=-1, keepdims=True)
    norm = pg_sel / (v1 + v2)
    gate = jnp.where(lane_f == i1, v1 * norm, jnp.where(lane_f == i2, v2 * norm, 0.0))
    for e in range(N_EXPERTS):
        au = _dot(hb, wup_ref[e])
        a, u = au[:, :D_EXPERT], au[:, D_EXPERT:]
        w_e = jnp.sum(jnp.where(lane == N_GROUPS + e, gate, 0.0), axis=-1, keepdims=True)
        hid_ref[:, e * D_EXPERT:(e + 1) * D_EXPERT] = (w_e * (a * _sigmoid(a) * u)).astype(BF16)
    y = _dot(hid_ref[...], wdn_ref[...])
    out_ref[...] = _layer_norm(DEEPNORM_ALPHA * h + y, g_ref[...], b_ref[...])


def hierarchical_moe(h, w_rg, b_rg, w_re, b_re, w_up, w_down, ln_g, ln_b, bm):
    t, d = h.shape
    ne = N_EXPERTS
    w_r = jnp.concatenate([w_rg, w_re.transpose(1, 0, 2).reshape(d, ne)], axis=1)
    w_r = jnp.pad(w_r, ((0, 0), (0, ROUTER_LANES - w_r.shape[1])))
    b_r = jnp.pad(jnp.concatenate([b_rg, b_re.reshape(ne)]), (0, ROUTER_LANES - N_GROUPS - ne)).reshape(1, -1)
    row = lambda i: (i, 0)
    return pl.pallas_call(
        _moe_kernel,
        grid=(t // bm,),
        in_specs=[pl.BlockSpec((bm, d), row), _resident((d, ROUTER_LANES)), _resident((1, ROUTER_LANES)),
                  _resident((ne, d, 2 * D_EXPERT)), _resident((ne * D_EXPERT, d)), _resident((1, d)),
                  _resident((1, d))],
        out_specs=pl.BlockSpec((bm, d), row),
        out_shape=jax.ShapeDtypeStruct((t, d), F32),
        scratch_shapes=[pltpu.VMEM((bm, ne * D_EXPERT), BF16)],
        compiler_params=_params(("parallel",)),
        name="hierarchical_moe",
    )(h, w_r, b_r, w_up.astype(BF16), w_down.reshape(ne * D_EXPERT, d).astype(BF16),
      ln_g.reshape(1, d), ln_b.reshape(1, d))


def _pad_in_weight(w_in):
    d = w_in.shape[0]
    sizes = (768, 768, MLA_Q_RANK, MLA_KV_RANK, MLA_ROPE, 256, 384, 12, 256)
    offs = np.concatenate([[0], np.cumsum(sizes)])
    sb, moba, c_q, c_kv, k_rope, nsa_q, nsa_kv, nsa_g, mem_q = (w_in[:, offs[i]:offs[i + 1]] for i in range(len(sizes)))
    z = lambda n: jnp.zeros((d, n), w_in.dtype)
    q_scale = HEAD_DIM ** -0.5 * LOG2E
    scale_q = lambda qkv: jnp.concatenate([qkv[:, :BRANCH_W] * q_scale, qkv[:, BRANCH_W:]], axis=1)
    return jnp.concatenate([scale_q(sb), scale_q(moba), c_q, c_kv, k_rope, _rotate_half_cols(k_rope), z(64),
                            nsa_kv[:, :128], nsa_g, z(116), nsa_kv[:, 128:], nsa_q * q_scale, mem_q * q_scale], axis=1)


def kernel(x, mem, w_in, g_cq, g_ckv, w_uq, w_ukv, nsa_pe, w_phi_k1, w_phi_k2, w_phi_v1, w_phi_v2, w_mem_kv, w_br,
           w_gate, b_gate, w_out, ln1_g, ln1_b, w_rg, b_rg, w_re, b_re, w_up, w_down, ln2_g, ln2_b):
    b, s_len, d = x.shape
    s = -(-s_len // MOBA_BLOCK) * MOBA_BLOCK
    t = b * s
    n_mem = mem.shape[1]
    h = jnp.pad(x, ((0, 0), (0, s - s_len), (0, 0))).reshape(t, d)
    for l in range(w_in.shape[0]):
        a, f = in_projection(h, _pad_in_weight(w_in[l]).astype(BF16), bm=512)
        a3, f3 = a.reshape(b, s, IN_PAD), f.reshape(b, s, F32_W)
        q, kv, kr = mla_project(f, g_cq[l], g_ckv[l], w_uq[l], w_ukv[l], s, bm=512)
        kc, vct = nsa_compress(f3, nsa_pe[l], w_phi_k1[l], w_phi_k2[l], w_phi_v1[l], w_phi_v2[l])
        mkv = matmul(mem.reshape(b * n_mem, d), w_mem_kv[l].astype(BF16), bm=n_mem, out_dtype=BF16)
        branches = [sb_attention(a3),
                    moba_attention(a3),
                    mla_attention(q.reshape(b, s, -1), kv.reshape(b, s, -1), kr.reshape(b, s, -1)),
                    nsa_attention(a3, f3, kc, vct),
                    cross_attention(a3, mkv.reshape(b, n_mem, -1), qb=512)]
        h = gated_merge(h, [o.reshape(t, BRANCH_W) for o in branches], w_gate[l], b_gate[l], w_br[l], w_out[l],
                        ln1_g[l], ln1_b[l], bm=256)
        h = hierarchical_moe(h, w_rg[l], b_rg[l], w_re[l], b_re[l], w_up[l], w_down[l], ln2_g[l], ln2_b[l], bm=256)
    return h.reshape(b, s, d)[:, :s_len]
```

```python
import functools

import jax
import jax.numpy as jnp
import numpy as np
from jax import lax
from jax.experimental import pallas as pl
from jax.experimental.pallas import tpu as pltpu

DEPTH = 4
HEAD_DIM = 64
N_HEADS = 4
BRANCH_W = N_HEADS * HEAD_DIM
N_BRANCHES = 5
MOBA_BLOCK = 256
MOBA_TOPK = 3
MLA_Q_RANK = 256
MLA_KV_RANK = 128
MLA_NOPE = 64
MLA_ROPE = 32
MLA_V = 64
ROPE_THETA = 10000.0
NSA_CMP_LEN = 32
NSA_CMP_STRIDE = 16
NSA_SEL_LEN = 64
NSA_TOPN = 16
NSA_WINDOW = 512
NSA_PHI_HIDDEN = 128
N_GROUPS = 4
EXPERTS_PER_GROUP = 4
N_EXPERTS = N_GROUPS * EXPERTS_PER_GROUP
D_EXPERT = 256
DEEPNORM_ALPHA = (2.0 * DEPTH) ** 0.25
LN_EPS = 1e-5
RMS_EPS = 1e-6
NEG = -1e30
BIG = 1e30
NO_INDEX = 1e9

LANES = 128
BF16_SUBLANES = 16
VALUE_SLOT = HEAD_DIM + BF16_SUBLANES
QUERY_BLOCK = 256
KEY_TILE = 256
SOFTMAX_KEY_TILE = 512
SB_KEY_TILE = 512
LOG2E = 1.4426950408889634
VMEM_LIMIT_BYTES = 56 * 1024 * 1024

F32 = jnp.float32
BF16 = jnp.bfloat16
HIGHEST = lax.Precision.HIGHEST

COL_SB = 0
COL_MOBA = 768
COL_MLA = 1536
COL_NSA_CMP = 2048
COL_NSA_G = 2176
COL_NSA_SLC = 2304
COL_NSA_WIN = 2432
COL_NSA_Q = 2560
COL_MEM_Q = 2816
IN_PAD = 3072
COL_F32 = COL_MLA
F32_W = COL_NSA_SLC - COL_MLA


def _params(semantics):
    return pltpu.CompilerParams(dimension_semantics=semantics, vmem_limit_bytes=VMEM_LIMIT_BYTES)


def _dot(a, b):
    return jnp.dot(a, b, preferred_element_type=F32)


def _dot_exact(a, b):
    return jnp.dot(a, b, preferred_element_type=F32, precision=HIGHEST)


def _sigmoid(x):
    return 1.0 / (1.0 + jnp.exp(-x))


def _resident(shape):
    zeros = (0,) * len(shape)
    return pl.BlockSpec(shape, lambda *_: zeros, pipeline_mode=pl.Buffered(1))


def _in_proj_kernel(x_ref, w_ref, o_ref, f_ref):
    y = _dot(x_ref[...].astype(BF16), w_ref[...])
    o_ref[...] = y.astype(BF16)
    f_ref[...] = y[:, COL_F32:COL_F32 + F32_W]


def in_projection(h, w, bm):
    t, d = h.shape
    n = w.shape[1]
    return pl.pallas_call(
        _in_proj_kernel,
        grid=(t // bm,),
        in_specs=[pl.BlockSpec((bm, d), lambda i: (i, 0)), _resident((d, n))],
        out_specs=[pl.BlockSpec((bm, n), lambda i: (i, 0)), pl.BlockSpec((bm, F32_W), lambda i: (i, 0))],
        out_shape=[jax.ShapeDtypeStruct((t, n), BF16), jax.ShapeDtypeStruct((t, F32_W), F32)],
        compiler_params=_params(("parallel",)),
        name="in_projection",
    )(h, w)


def _mm_kernel(x_ref, w_ref, o_ref):
    o_ref[...] = _dot(x_ref[...].astype(BF16), w_ref[...]).astype(o_ref.dtype)


def matmul(x, w, *, bm, out_dtype=F32):
    m, k = x.shape
    n = w.shape[1]
    return pl.pallas_call(
        _mm_kernel,
        grid=(m // bm,),
        in_specs=[pl.BlockSpec((bm, k), lambda i: (i, 0)), _resident((k, n))],
        out_specs=pl.BlockSpec((bm, n), lambda i: (i, 0)),
        out_shape=jax.ShapeDtypeStruct((m, n), out_dtype),
        compiler_params=_params(("parallel",)),
        name="matmul",
    )(x, w)


def _softmax_tile(s, v_slot, m, acc, shift=None):
    top = jnp.max(s, axis=0, keepdims=True)
    m_new = jnp.maximum(m, top if shift is None else top - shift)
    p = jnp.exp2(s - (m_new if shift is None else m_new + shift))
    return m_new, jnp.exp2(m - m_new) * acc + _dot(v_slot, p.astype(BF16))


def _softmax_init(q):
    return (jnp.full((1, q), NEG, F32), jnp.zeros((VALUE_SLOT, q), F32))


def _softmax_result(acc):
    return acc[:HEAD_DIM] / acc[HEAD_DIM:HEAD_DIM + 1]


def _causal_softmax(qi, qb, kt, heads, scores, values, finish, finish_last, raw_ref):
    def tile(j, state, fin):
        return tuple(_softmax_tile(fin[h][0], values(j, h), *state[h], shift=fin[h][1]) for h in range(heads))

    def prefetch(j, slot):
        for h, raw in enumerate(scores(j)):
            raw_ref[slot, h] = raw

    def held(slot):
        return tuple(raw_ref[slot, h] for h in range(heads))

    def pair(i, state):
        prefetch(2 * i + 1, 1)
        state = tile(2 * i, state, finish(2 * i, held(0)))
        prefetch(2 * i + 2, 0)
        return tile(2 * i + 1, state, finish(2 * i + 1, held(1)))

    def odd_tail(state):
        prefetch(last, 1)
        state = tile(last - 1, state, finish(last - 1, held(0)))
        return tile(last, state, finish_last(last, held(1)))

    def even_tail(state):
        return tile(last, state, finish_last(last, held(0)))

    last = lax.div(qi * qb, kt)
    prefetch(0, 0)
    state = tuple(_softmax_init(qb) for _ in range(heads))
    state = lax.fori_loop(0, lax.div(last, 2), pair, state)
    state = lax.cond(lax.rem(last, 2) == 1, odd_tail, even_tail, state)
    return [_softmax_result(acc) for _, acc in state]


def _transposed(x):
    return x.astype(F32).T


def _pair_rows(qt_all, h):
    g = qt_all[(h // 2) * LANES:(h // 2 + 1) * LANES]
    row = lax.broadcasted_iota(jnp.int32, g.shape, 0)
    keep = (row >= HEAD_DIM) if h % 2 else (row < HEAD_DIM)
    return jnp.where(keep, g, 0.0).astype(BF16)


def _pair_cols(h):
    return slice((h // 2) * LANES, (h // 2 + 1) * LANES)


def _head_rows(h):
    return slice(h * HEAD_DIM, (h + 1) * HEAD_DIM)


def _fill_transposed(src_ref, dst_ref, kb):
    def chunk(c, _):
        r0 = pl.multiple_of(c * kb, kb)
        dst_ref[:, pl.ds(r0, kb)] = src_ref[0, pl.ds(r0, kb), :].astype(F32).T.astype(BF16)
        return 0
    lax.fori_loop(0, src_ref.shape[1] // kb, chunk, 0)


def _fill_value_slots(src_ref, dst_ref, kb, first_rows):
    ones = jnp.ones((BF16_SUBLANES, kb), BF16)

    def chunk(c, _):
        r0 = pl.multiple_of(c * kb, kb)
        xt = src_ref[0, pl.ds(r0, kb), :].astype(F32).T
        for h, r in enumerate(first_rows):
            dst_ref[h * VALUE_SLOT:h * VALUE_SLOT + HEAD_DIM, pl.ds(r0, kb)] = xt[r:r + HEAD_DIM].astype(BF16)
            dst_ref[h * VALUE_SLOT + HEAD_DIM:(h + 1) * VALUE_SLOT, pl.ds(r0, kb)] = ones
        return 0
    lax.fori_loop(0, src_ref.shape[1] // kb, chunk, 0)


def _value_slot(h):
    return slice(h * VALUE_SLOT, (h + 1) * VALUE_SLOT)


def _store_heads(o_ref, outs):
    o_ref[0] = jnp.concatenate(outs, axis=0).T.astype(o_ref.dtype)


def _qkv_specs(s, kb, col):
    c = col // BRANCH_W
    return [pl.BlockSpec((1, kb, BRANCH_W), lambda bi, qi: (bi, qi, c)),
            pl.BlockSpec((1, s, BRANCH_W), lambda bi, qi: (bi, 0, c + 1)),
            pl.BlockSpec((1, s, BRANCH_W), lambda bi, qi: (bi, 0, c + 2))]


def _sb_kernel(q_ref, k_ref, v_ref, tri_ref, o_ref, vt_ref, raw_ref, *, kt):
    qi = pl.program_id(1)
    qb = q_ref.shape[1]
    kb = KEY_TILE
    pieces = kt // kb

    @pl.when(qi == 0)
    def _():
        _fill_transposed(v_ref, vt_ref, kb)

    qt_all = _transposed(q_ref[0])
    qts = [_pair_rows(qt_all, h) for h in range(N_HEADS)]
    tri = tri_ref[...]
    rel = lax.broadcasted_iota(jnp.int32, (kb, qb), 0) - lax.broadcasted_iota(jnp.int32, (kb, qb), 1)
    order = [(u, h) for u in reversed(range(pieces)) for h in range(N_HEADS)]

    def prefetch(j, slot):
        for u, h in order:
            raw_ref[slot, h, u * kb:(u + 1) * kb] = _dot(
                k_ref[0, pl.ds(pl.multiple_of(j * kt + u * kb, kb), kb), _pair_cols(h)], qts[h])

    def tile(j, state, slot, last):
        k0 = {u: pl.multiple_of(j * kt + u * kb, kb) for u in range(pieces)}
        log_beta, later, col_sum, past = {}, {}, {}, {}
        for u, h in order:
            z = raw_ref[slot, h, u * kb:(u + 1) * kb]
            nz = -z
            soft = jnp.log(1.0 + jnp.exp2(jnp.minimum(z, nz))) * LOG2E
            log_keep = jnp.minimum(nz, 0.0) - soft
            log_beta[u, h] = log_keep + z
            if last:
                past[u] = rel < qi * qb - k0[u]
                log_keep = jnp.where(past[u], log_keep, 0.0)
            sums = _dot(tri, log_keep.astype(BF16))
            later[u, h] = sums[:kb]
            col_sum[u, h] = sums[kb:kb + 1]
        state = list(state)
        for u, h in order:
            carry, acc = state[h]
            a = jnp.exp2(log_beta[u, h] + later[u, h])
            if last:
                a = jnp.where(past[u], a, 0.0)
            part = _dot(vt_ref[_head_rows(h), pl.ds(k0[u], kb)], a.astype(BF16))
            state[h] = (carry + col_sum[u, h], acc + jnp.exp2(carry) * part)
        return tuple(state)

    def pair(i, state):
        first = last - 1 - 2 * i
        prefetch(first - 1, 0)
        state = tile(first, state, 1, False)
        prefetch(jnp.maximum(first - 2, 0), 1)
        return tile(first - 1, state, 0, False)

    last = lax.div(qi * qb, kt)
    prefetch(last, 0)
    prefetch(jnp.maximum(last - 1, 0), 1)
    state = tuple((jnp.zeros((1, qb), F32), jnp.zeros((HEAD_DIM, qb), F32)) for _ in range(N_HEADS))
    state = tile(last, state, 0, True)
    state = lax.fori_loop(0, lax.div(last, 2), pair, state)
    state = lax.cond(lax.rem(last, 2) == 1, lambda st: tile(0, st, 1, False), lambda st: st, state)
    _store_heads(o_ref, [st[1] for st in state])


def sb_attention(a):
    b, s, _ = a.shape
    kb, qb = KEY_TILE, QUERY_BLOCK
    tri = jnp.asarray(np.concatenate([np.triu(np.ones((kb, kb), np.float32), 1), np.ones((BF16_SUBLANES, kb), np.float32)]),
                      BF16)
    return pl.pallas_call(
        functools.partial(_sb_kernel, kt=min(SB_KEY_TILE, s)),
        grid=(b, s // qb),
        in_specs=_qkv_specs(s, qb, COL_SB) + [_resident(tri.shape)],
        out_specs=pl.BlockSpec((1, qb, BRANCH_W), lambda bi, qi: (bi, qi, 0)),
        out_shape=jax.ShapeDtypeStruct((b, s, BRANCH_W), BF16),
        scratch_shapes=[pltpu.VMEM((BRANCH_W, s), BF16), pltpu.VMEM((2, N_HEADS, min(SB_KEY_TILE, s), qb), F32)],
        compiler_params=_params(("parallel", "arbitrary")),
        name="sb_attention",
    )(a, a, a, tri)


def _alibi_slopes(n):
    return np.power(2.0, -8.0 * np.arange(1, n + 1, dtype=np.float64) / n).astype(np.float32)


def _select_top(scores, ids, count, floor):
    def step(_, c):
        out = []
        for score, sel in c:
            mx = jnp.max(score, axis=0, keepdims=True)
            idx = jnp.min(jnp.where(score == mx, ids, NO_INDEX), axis=0, keepdims=True)
            pick = ids == idx
            out.append((jnp.where(pick, -jnp.inf, score), jnp.where(jnp.logical_and(pick, mx > floor), 1.0, sel)))
        return tuple(out)
    done = lax.fori_loop(0, count, step, tuple((s, jnp.zeros_like(s)) for s in scores))
    return [sel for _, sel in done]


def _moba_kernel(slope_ref, q_ref, k_ref, v_ref, o_ref, vt_ref, km_ref, sel_ref, raw_ref, bias_ref, *, kt, topk):
    qi = pl.program_id(1)
    kb = q_ref.shape[1]
    nblk = km_ref.shape[0]
    per_tile = kt // kb
    rk = lax.broadcasted_iota(jnp.int32, (kt, kb), 0)
    rel = rk - lax.broadcasted_iota(jnp.int32, (kt, kb), 1)

    @pl.when(qi == 0)
    def _():
        def key_mean(c, _):
            km_ref[pl.ds(c, 1), :] = jnp.mean(k_ref[0, pl.ds(pl.multiple_of(c * kb, kb), kb), :].astype(F32),
                                              axis=0, keepdims=True)
            return 0
        _fill_value_slots(v_ref, vt_ref, kb, [h * HEAD_DIM for h in range(N_HEADS)])
        lax.fori_loop(0, nblk, key_mean, 0)
        for h in range(N_HEADS):
            bias_ref[h] = slope_ref[h] * rel.astype(F32)

    qt_all = _transposed(q_ref[0])
    qts = [_pair_rows(qt_all, h) for h in range(N_HEADS)]
    ids = lax.broadcasted_iota(jnp.int32, (nblk, kb), 0)
    row = lax.broadcasted_iota(jnp.int32, qt_all.shape, 0)
    km = km_ref[...]
    gscores = []
    for h in range(N_HEADS):
        in_head = jnp.logical_and(row >= h * HEAD_DIM, row < (h + 1) * HEAD_DIM)
        gscore = _dot_exact(km, jnp.where(in_head, qt_all, 0.0))
        gscores.append(jnp.where(ids < qi, gscore, NEG))
    for h, sel in enumerate(_select_top(gscores, ids.astype(F32), topk, 0.5 * NEG)):
        sel_ref[h] = jnp.where(sel > 0.5, 0.0, NEG)


    def scores(j):
        k0 = pl.multiple_of(j * kt, kt)
        return tuple(_dot(k_ref[0, pl.ds(k0, kt), _pair_cols(h)], qts[h]) for h in range(N_HEADS))

    def values(j, h):
        return vt_ref[_value_slot(h), pl.ds(pl.multiple_of(j * kt, kt), kt)]

    def unselected(j, h):
        rows = [jnp.broadcast_to(sel_ref[h, pl.ds(j * per_tile + r, 1), :], (kb, kb)) for r in range(per_tile)]
        return jnp.concatenate(rows, axis=0)

    def finish(j, raw):
        off = (qi * kb - j * kt).astype(F32)
        return tuple((raw[h] + bias_ref[h] + unselected(j, h), slope_ref[h] * off) for h in range(N_HEADS))

    def finish_last(j, raw):
        d0 = qi * kb - j * kt
        own_causal = jnp.logical_and(rk >= d0, rel <= d0)
        out = []
        for h in range(N_HEADS):
            s = raw[h] + bias_ref[h]
            out.append((jnp.where(own_causal, s, s + unselected(j, h)), slope_ref[h] * d0.astype(F32)))
        return tuple(out)

    _store_heads(o_ref, _causal_softmax(qi, kb, kt, N_HEADS, scores, values, finish, finish_last, raw_ref))


def moba_attention(a):
    b, s, _ = a.shape
    kb = MOBA_BLOCK
    nblk = s // kb
    topk = min(MOBA_TOPK, nblk - 1)
    slopes = jnp.asarray(_alibi_slopes(N_HEADS) * np.float32(LOG2E))
    c = COL_MOBA // BRANCH_W
    return pl.pallas_call(
        functools.partial(_moba_kernel, kt=min(SOFTMAX_KEY_TILE, s), topk=topk),
        grid_spec=pltpu.PrefetchScalarGridSpec(
            num_scalar_prefetch=1,
            grid=(b, nblk),
            in_specs=[pl.BlockSpec((1, kb, BRANCH_W), lambda bi, qi, sl: (bi, qi, c)),
                      pl.BlockSpec((1, s, BRANCH_W), lambda bi, qi, sl: (bi, 0, c + 1)),
                      pl.BlockSpec((1, s, BRANCH_W), lambda bi, qi, sl: (bi, 0, c + 2))],
            out_specs=pl.BlockSpec((1, kb, BRANCH_W), lambda bi, qi, sl: (bi, qi, 0)),
            scratch_shapes=[pltpu.VMEM((N_HEADS * VALUE_SLOT, s), BF16),
                            pltpu.VMEM((nblk, BRANCH_W), F32),
                            pltpu.VMEM((N_HEADS, nblk, kb), F32),
                            pltpu.VMEM((2, N_HEADS, min(SOFTMAX_KEY_TILE, s), kb), F32),
                            pltpu.VMEM((N_HEADS, min(SOFTMAX_KEY_TILE, s), kb), F32)]),
        out_shape=jax.ShapeDtypeStruct((b, s, BRANCH_W), BF16),
        compiler_params=_params(("parallel", "arbitrary")),
        name="moba_attention",
    )(slopes, a, a, a)


MLA_SLOT = 128


def _rms(x, g):
    return x * lax.rsqrt(jnp.mean(x * x, axis=-1, keepdims=True) + RMS_EPS) * g


def _mla_proj_kernel(x_ref, gq_ref, gkv_ref, wq_ref, wqr_ref, wkv_ref, cq_ref, sq_ref, ck_ref, sk_ref,
                     q_ref, kv_ref, kr_ref):
    x = x_ref[...]
    c_q = _rms(x[:, :MLA_Q_RANK], gq_ref[...]).astype(BF16)
    c_kv = _rms(x[:, MLA_Q_RANK:MLA_Q_RANK + MLA_KV_RANK], gkv_ref[...]).astype(BF16)
    q_ref[...] = (_dot(c_q, wq_ref[...]) * cq_ref[...] + _dot(c_q, wqr_ref[...]) * sq_ref[...]).astype(q_ref.dtype)
    kv_ref[...] = _dot(c_kv, wkv_ref[...]).astype(kv_ref.dtype)
    tail = x[:, MLA_Q_RANK + MLA_KV_RANK:]
    rot = pltpu.roll(tail, LANES - MLA_ROPE, axis=1)
    kr_ref[...] = (tail * ck_ref[...] + rot * sk_ref[...]).astype(kr_ref.dtype)


def _rope_tables(s_len):
    half = MLA_ROPE // 2
    freqs = jnp.power(ROPE_THETA, -jnp.arange(half, dtype=F32) / half)
    ang = jnp.arange(s_len).astype(F32)[:, None] * freqs
    cos = jnp.concatenate([jnp.cos(ang)] * 2, axis=-1)
    sin = jnp.concatenate([jnp.sin(ang)] * 2, axis=-1)
    zk = jnp.zeros((s_len, LANES - MLA_ROPE), F32)
    zq = jnp.zeros((s_len, MLA_SLOT - MLA_NOPE - MLA_ROPE), F32)
    cq = jnp.tile(jnp.concatenate([jnp.ones((s_len, MLA_NOPE), F32), cos, zq], axis=-1), (1, N_HEADS))
    sq = jnp.tile(jnp.concatenate([jnp.zeros((s_len, MLA_NOPE), F32), sin, zq], axis=-1), (1, N_HEADS))
    return cq, sq, jnp.concatenate([cos, zk], axis=-1), jnp.concatenate([sin, zk], axis=-1)


def _rotate_half_cols(w):
    half = w.shape[-1] // 2
    return jnp.concatenate([-w[..., half:], w[..., :half]], axis=-1)


def mla_project(f, g_cq, g_ckv, w_uq, w_ukv, s_len, bm):
    t = f.shape[0]
    wq = w_uq.reshape(MLA_Q_RANK, N_HEADS, MLA_NOPE + MLA_ROPE)
    pad = jnp.zeros((MLA_Q_RANK, N_HEADS, MLA_SLOT - MLA_NOPE - MLA_ROPE), wq.dtype)
    wq_s = jnp.concatenate([wq, pad], axis=-1).reshape(MLA_Q_RANK, -1)
    wqr_s = jnp.concatenate([jnp.zeros_like(wq[..., :MLA_NOPE]), _rotate_half_cols(wq[..., MLA_NOPE:]), pad],
                            axis=-1).reshape(MLA_Q_RANK, -1)
    cq, sq, ck, sk = _rope_tables(s_len)
    q_scale = (MLA_NOPE + MLA_ROPE) ** -0.5 * LOG2E
    nrow = s_len // bm
    row = lambda i: (i, 0)
    pos = lambda i: (i % nrow, 0)
    qw = N_HEADS * MLA_SLOT
    kvw = N_HEADS * (MLA_NOPE + MLA_V)
    return pl.pallas_call(
        _mla_proj_kernel,
        grid=(t // bm,),
        in_specs=[pl.BlockSpec((bm, 512), row),
                  _resident((1, MLA_Q_RANK)), _resident((1, MLA_KV_RANK)),
                  _resident((MLA_Q_RANK, qw)), _resident((MLA_Q_RANK, qw)), _resident((MLA_KV_RANK, kvw)),
                  pl.BlockSpec((bm, qw), pos), pl.BlockSpec((bm, qw), pos),
                  pl.BlockSpec((bm, LANES), pos), pl.BlockSpec((bm, LANES), pos)],
        out_specs=[pl.BlockSpec((bm, qw), row), pl.BlockSpec((bm, kvw), row), pl.BlockSpec((bm, LANES), row)],
        out_shape=[jax.ShapeDtypeStruct((t, qw), BF16), jax.ShapeDtypeStruct((t, kvw), BF16),
                   jax.ShapeDtypeStruct((t, LANES), BF16)],
        compiler_params=_params(("parallel",)),
        name="mla_project",
    )(f, g_cq.reshape(1, -1), g_ckv.reshape(1, -1), (wq_s * q_scale).astype(BF16), (wqr_s * q_scale).astype(BF16),
      w_ukv.astype(BF16), cq, sq, ck, sk)


def _mla_kernel(q_ref, kv_ref, kr_ref, o_ref, vt_ref, raw_ref, *, kt):
    qi = pl.program_id(1)
    qb = q_ref.shape[1]

    @pl.when(qi == 0)
    def _():
        _fill_value_slots(kv_ref, vt_ref, KEY_TILE, [h * LANES + MLA_NOPE for h in range(N_HEADS)])

    qt_all = _transposed(q_ref[0])
    qts = []
    for h in range(N_HEADS):
        g = qt_all[h * MLA_SLOT:(h + 1) * MLA_SLOT]
        row = lax.broadcasted_iota(jnp.int32, g.shape, 0)
        qts.append(jnp.concatenate([jnp.where(row < MLA_NOPE, g, 0.0), g[MLA_NOPE:], jnp.zeros_like(g[MLA_NOPE:])],
                                   axis=0).astype(BF16))
    rel = lax.broadcasted_iota(jnp.int32, (kt, qb), 0) - lax.broadcasted_iota(jnp.int32, (kt, qb), 1)

    def scores(j):
        k0 = pl.multiple_of(j * kt, kt)
        kr = kr_ref[0, pl.ds(k0, kt), :]
        return tuple(_dot(jnp.concatenate([kv_ref[0, pl.ds(k0, kt), h * LANES:(h + 1) * LANES], kr], axis=1), qts[h])
                     for h in range(N_HEADS))

    def values(j, h):
        return vt_ref[_value_slot(h), pl.ds(pl.multiple_of(j * kt, kt), kt)]

    def finish_last(j, raw):
        causal = rel <= qi * qb - j * kt
        return tuple((jnp.where(causal, s, NEG), None) for s in raw)

    outs = _causal_softmax(qi, qb, kt, N_HEADS, scores, values,
                           lambda j, raw: tuple((s, None) for s in raw), finish_last, raw_ref)
    _store_heads(o_ref, outs)


def mla_attention(q, kv, kr):
    b, s, qw = q.shape
    kb = QUERY_BLOCK
    return pl.pallas_call(
        functools.partial(_mla_kernel, kt=min(SOFTMAX_KEY_TILE, s)),
        grid=(b, s // kb),
        in_specs=[pl.BlockSpec((1, kb, qw), lambda bi, qi: (bi, qi, 0)),
                  pl.BlockSpec((1, s, kv.shape[2]), lambda bi, qi: (bi, 0, 0)),
                  pl.BlockSpec((1, s, LANES), lambda bi, qi: (bi, 0, 0))],
        out_specs=pl.BlockSpec((1, kb, BRANCH_W), lambda bi, qi: (bi, qi, 0)),
        out_shape=jax.ShapeDtypeStruct((b, s, BRANCH_W), BF16),
        scratch_shapes=[pltpu.VMEM((N_HEADS * VALUE_SLOT, s), BF16),
                        pltpu.VMEM((2, N_HEADS, min(SOFTMAX_KEY_TILE, s), kb), F32)],
        compiler_params=_params(("parallel", "arbitrary")),
        name="mla_attention",
    )(q, kv, kr)


def _cross_kernel(q_ref, kv_ref, o_ref):
    qt_all = _transposed(q_ref[0])
    kv = kv_ref[0]
    vt = _transposed(kv[:, BRANCH_W:]).astype(BF16)
    outs = []
    for h in range(N_HEADS):
        s = _dot(kv[:, _pair_cols(h)], _pair_rows(qt_all, h))
        p = jnp.exp2(s - jnp.max(s, axis=0, keepdims=True))
        outs.append(_dot(vt[_head_rows(h)], p.astype(BF16)) / jnp.sum(p, axis=0, keepdims=True))
    _store_heads(o_ref, outs)


def cross_attention(a, mkv, qb):
    b, s, _ = a.shape
    n = mkv.shape[1]
    c = COL_MEM_Q // BRANCH_W
    return pl.pallas_call(
        _cross_kernel,
        grid=(b, s // qb),
        in_specs=[pl.BlockSpec((1, qb, BRANCH_W), lambda bi, qi: (bi, qi, c)),
                  pl.BlockSpec((1, n, 2 * BRANCH_W), lambda bi, qi: (bi, 0, 0))],
        out_specs=pl.BlockSpec((1, qb, BRANCH_W), lambda bi, qi: (bi, qi, 0)),
        out_shape=jax.ShapeDtypeStruct((b, s, BRANCH_W), BF16),
        compiler_params=_params(("parallel", "parallel")),
        name="cross_attention",
    )(a, mkv)


def _compress_kernel(x_ref, pe_ref, w1_ref, w2_ref, kc_ref, vct_ref):
    n = kc_ref.shape[1]
    first = jnp.zeros((n, 2 * NSA_PHI_HIDDEN), F32)
    second = jnp.zeros((n, 2 * NSA_PHI_HIDDEN), F32)
    for r in range(NSA_CMP_STRIDE):
        x = x_ref[0, pl.ds(r, n, stride=NSA_CMP_STRIDE), :]
        first = first + _dot((x + pe_ref[r:r + 1]).astype(BF16), w1_ref[r])
        second = second + _dot((x + pe_ref[NSA_CMP_STRIDE + r:NSA_CMP_STRIDE + r + 1]).astype(BF16),
                               w1_ref[NSA_CMP_STRIDE + r])
    hidden = jax.nn.gelu(first + pltpu.roll(second, n - 1, axis=0))
    out = _dot(hidden.astype(BF16), w2_ref[...])
    kc_ref[0] = out.astype(BF16)
    vct_ref[0] = out.T[HEAD_DIM:].astype(BF16)


def _pair_diag(wk, wv):
    z = jnp.zeros_like(wk)
    return jnp.concatenate([jnp.concatenate([wk, z], axis=-1), jnp.concatenate([z, wv], axis=-1)], axis=-2)


def nsa_compress(f, nsa_pe, w_k1, w_k2, w_v1, w_v2):
    b, s, _ = f.shape
    n = s // NSA_CMP_STRIDE
    hd = HEAD_DIM
    w1 = _pair_diag(w_k1.reshape(NSA_CMP_LEN, hd, -1), w_v1.reshape(NSA_CMP_LEN, hd, -1)).astype(BF16)
    w2 = _pair_diag(w_k2, w_v2).astype(BF16)
    pe = jnp.concatenate([nsa_pe, nsa_pe], axis=-1)
    c = (COL_NSA_CMP - COL_F32) // LANES
    return pl.pallas_call(
        _compress_kernel,
        grid=(b,),
        in_specs=[pl.BlockSpec((1, s, LANES), lambda bi: (bi, 0, c)),
                  _resident(pe.shape), _resident(w1.shape), _resident(w2.shape)],
        out_specs=[pl.BlockSpec((1, n, LANES), lambda bi: (bi, 0, 0)),
                   pl.BlockSpec((1, hd, n), lambda bi: (bi, 0, 0))],
        out_shape=[jax.ShapeDtypeStruct((b, n, LANES), BF16), jax.ShapeDtypeStruct((b, hd, n), BF16)],
        compiler_params=_params(("parallel",)),
        name="nsa_compress",
    )(f, pe, w1, w2)


def _nsa_kernel(q_ref, g_ref, slope_ref, kc_ref, vct_ref, ovt_ref, slc_ref, win_ref, o_ref,
                vst_ref, vwt_ref, sel_ref, bias_ref, cbias_ref, raw_ref, order_ref, *, topn, n_cmp):
    qi = pl.program_id(1)
    qn = q_ref.shape[1]
    kb = KEY_TILE
    lanes = N_HEADS * qn
    dv = HEAD_DIM
    q0 = qi * qn

    slope = slope_ref[...]
    ql = jnp.bitwise_and(lax.broadcasted_iota(jnp.int32, (1, lanes), 1), qn - 1)
    qpos = q0 + ql
    kid = lax.broadcasted_iota(jnp.int32, (kb, 1), 0)
    ncp = kc_ref.shape[1]
    cid = lax.broadcasted_iota(jnp.int32, (ncp, 1), 0)
    cmp_last = cid * NSA_CMP_STRIDE + (NSA_CMP_LEN - 1)
    cmp_end = jnp.where(cid < n_cmp, cmp_last, 1 << 30)

    @pl.when(qi == 0)
    def _():
        _fill_value_slots(slc_ref, vst_ref, kb, [HEAD_DIM])
        _fill_value_slots(win_ref, vwt_ref, kb, [HEAD_DIM])
        bias_ref[...] = slope * (ql - kid).astype(F32)
        cbias_ref[...] = slope * (ql - cmp_last).astype(F32)

    qt_all = _transposed(q_ref[0])
    qt = jnp.concatenate([qt_all[_head_rows(h)] for h in range(N_HEADS)], axis=1)
    qt = jnp.concatenate([qt, jnp.zeros_like(qt)], axis=0).astype(BF16)
    jd = lax.div(q0, kb)
    win_tiles = [jnp.maximum(jd - back, 0) for back in range(NSA_WINDOW // kb + 1)]

    def raw_scores(kv_ref, j):
        return _dot(kv_ref[0, pl.ds(pl.multiple_of(j * kb, kb), kb), :], qt)

    raw_c = _dot(kc_ref[0], qt)
    raw_d = raw_scores(slc_ref, jd)
    raw_w = [raw_scores(win_ref, j) for j in win_tiles]
    bias = bias_ref[...]
    causal = ql - kid >= 0

    valid = cmp_end - ql <= q0
    s = jnp.where(valid, raw_c - cbias_ref[...], NEG)
    top = jnp.max(s, axis=0, keepdims=True)
    e = jnp.exp2(s - top)
    p_c = e * jnp.where(top > 0.5 * NEG, 1.0 / jnp.sum(e, axis=0, keepdims=True), 0.0)
    o_c = _dot(vct_ref[0], p_c.astype(BF16))

    p_sum = p_c[:, 0:qn]
    for hh in range(1, N_HEADS):
        p_sum = p_sum + p_c[:, hh * qn:(hh + 1) * qn]
    imp = _dot_exact(ovt_ref[...], p_sum)
    nsel = imp.shape[0]
    sid = lax.broadcasted_iota(jnp.int32, (nsel, qn), 0)
    cur = jnp.right_shift(qpos[:, 0:qn], NSA_SEL_LEN.bit_length() - 1)
    forced = jnp.logical_or(sid == 0, sid == cur)
    score = jnp.where(forced, BIG, jnp.where(sid < cur, imp, NEG))
    sel, = _select_top([score], sid.astype(F32), topn, 0.5 * NEG)
    sel_ref[...] = jnp.concatenate([jnp.where(sel > 0.5, 0.0, NEG)] * N_HEADS, axis=1)
    per_tile = kb // NSA_SEL_LEN
    n = jnp.int32(0)
    for j in range(nsel // per_tile):
        order_ref[n] = j
        wanted = jnp.max(sel[j * per_tile:(j + 1) * per_tile]) > 0.5
        n = n + jnp.logical_and(wanted, j < jd).astype(jnp.int32)

    def sel_tile(j, c, raw, own):
        rows = [jnp.broadcast_to(sel_ref[pl.ds(j * per_tile + r, 1), :], (NSA_SEL_LEN, lanes))
                for r in range(per_tile)]
        s = raw - bias + jnp.concatenate(rows, axis=0)
        if own:
            s = jnp.where(causal, s, NEG)
        shift = slope * ((jd - j) * kb).astype(F32)
        return _softmax_tile(s, vst_ref[:, pl.ds(pl.multiple_of(j * kb, kb), kb)], *c, shift=shift)

    def fetch(k, slot):
        raw_ref[slot] = raw_scores(slc_ref, order_ref[k])

    def work(k, slot, c):
        return sel_tile(order_ref[k], c, raw_ref[slot], False)

    def pair(i, c):
        fetch(2 * i + 1, 1)
        c = work(2 * i, 0, c)
        fetch(jnp.minimum(2 * i + 2, n), 0)
        return work(2 * i + 1, 1, c)

    fetch(0, 0)
    c = sel_tile(jd, _softmax_init(lanes), raw_d, True)
    c = lax.fori_loop(0, lax.div(n, 2), pair, c)
    c = lax.cond(lax.rem(n, 2) == 1, lambda c: work(n - 1, 0, c), lambda c: c, c)
    o_s = _softmax_result(c[1])

    tops, masked = [], []
    for back, raw in enumerate(raw_w):
        s = raw - bias
        if back == 0:
            s = jnp.where(causal, s, NEG)
        else:
            inside = NSA_WINDOW - back * kb if back * kb + kb > NSA_WINDOW else 2 * kb
            s = jnp.where(ql - kid < jnp.where(jd - back >= 0, inside, -2 * kb), s, NEG)
        masked.append(s)
        tops.append(jnp.max(s, axis=0, keepdims=True) - slope * float(back * kb))
    m_w = functools.reduce(jnp.maximum, tops)
    acc_w = jnp.zeros((VALUE_SLOT, lanes), F32)
    for back, s in enumerate(masked):
        p = jnp.exp2(s - (m_w + slope * float(back * kb)))
        acc_w = acc_w + _dot(vwt_ref[:, pl.ds(pl.multiple_of(win_tiles[back] * kb, kb), kb)], p.astype(BF16))
    o_w = _softmax_result(acc_w)

    gt = _sigmoid(g_ref[0]).T
    def gate(ci):
        return jnp.concatenate([gt[h * 3 + ci:h * 3 + ci + 1] for h in range(N_HEADS)], axis=1)
    out = gate(0) * o_c + gate(1) * o_s + gate(2) * o_w
    _store_heads(o_ref, [out[:, h * qn:(h + 1) * qn] for h in range(N_HEADS)])


def nsa_attention(a, f, kc, vct):
    b, s, _ = a.shape
    ncp = kc.shape[1]
    n_cmp = ncp - NSA_CMP_LEN // NSA_CMP_STRIDE + 1
    nsel = s // NSA_SEL_LEN
    qn = QUERY_BLOCK
    lanes = N_HEADS * qn
    cs = np.arange(ncp) * NSA_CMP_STRIDE
    ss = np.arange(nsel) * NSA_SEL_LEN
    ov = ((cs[:, None] < ss[None, :] + NSA_SEL_LEN) & (cs[:, None] + NSA_CMP_LEN > ss[None, :])
          & (np.arange(ncp)[:, None] < n_cmp)).astype(np.float32)
    slopes = jnp.asarray(np.repeat(_alibi_slopes(N_HEADS) * np.float32(LOG2E), qn)[None, :])
    per_b = lambda bi, qi: (bi, 0, 0)
    return pl.pallas_call(
        functools.partial(_nsa_kernel, topn=min(NSA_TOPN, nsel), n_cmp=n_cmp),
        grid=(b, s // qn),
        in_specs=[pl.BlockSpec((1, qn, BRANCH_W), lambda bi, qi: (bi, qi, COL_NSA_Q // BRANCH_W)),
                  pl.BlockSpec((1, qn, LANES), lambda bi, qi: (bi, qi, (COL_NSA_G - COL_F32) // LANES)),
                  _resident((1, lanes)),
                  pl.BlockSpec((1, ncp, LANES), per_b),
                  pl.BlockSpec((1, HEAD_DIM, ncp), per_b),
                  _resident((nsel, ncp)),
                  pl.BlockSpec((1, s, LANES), lambda bi, qi: (bi, 0, COL_NSA_SLC // LANES)),
                  pl.BlockSpec((1, s, LANES), lambda bi, qi: (bi, 0, COL_NSA_WIN // LANES))],
        out_specs=pl.BlockSpec((1, qn, BRANCH_W), lambda bi, qi: (bi, qi, 0)),
        out_shape=jax.ShapeDtypeStruct((b, s, BRANCH_W), BF16),
        scratch_shapes=[pltpu.VMEM((VALUE_SLOT, s), BF16), pltpu.VMEM((VALUE_SLOT, s), BF16),
                        pltpu.VMEM((nsel, lanes), F32), pltpu.VMEM((KEY_TILE, lanes), F32),
                        pltpu.VMEM((ncp, lanes), F32), pltpu.VMEM((2, KEY_TILE, lanes), F32),
                        pltpu.SMEM((s // KEY_TILE + 1,), jnp.int32)],
        compiler_params=_params(("parallel", "arbitrary")),
        name="nsa_attention",
    )(a, f, slopes, kc, vct, jnp.asarray(ov.T), a, a)


def _layer_norm(r, g, b):
    mu = jnp.mean(r, axis=-1, keepdims=True)
    c = r - mu
    var = jnp.mean(c * c, axis=-1, keepdims=True)
    return c * lax.rsqrt(var + LN_EPS) * g + b


def _merge_kernel(h_ref, o0_ref, o1_ref, o2_ref, o3_ref, o4_ref, wg_ref, bg_ref, wbr_ref, wout_ref, g_ref, b_ref,
                  out_ref):
    h = h_ref[...]
    hb = h.astype(BF16)
    merged = jnp.zeros(h.shape, F32)
    for i, o_ref in enumerate((o0_ref, o1_ref, o2_ref, o3_ref, o4_ref)):
        gate = _sigmoid(_dot(hb, wg_ref[i]) + bg_ref[i])
        merged = merged + gate * _dot(o_ref[...], wbr_ref[i])
    y = _dot(merged.astype(BF16), wout_ref[...])
    out_ref[...] = _layer_norm(DEEPNORM_ALPHA * h + y, g_ref[...], b_ref[...])


def gated_merge(h, branches, w_gate, b_gate, w_br, w_out, ln_g, ln_b, bm):
    t, d = h.shape
    nb, bw = len(branches), branches[0].shape[1]
    row = lambda i: (i, 0)
    return pl.pallas_call(
        _merge_kernel,
        grid=(t // bm,),
        in_specs=[pl.BlockSpec((bm, d), row)] + [pl.BlockSpec((bm, bw), row)] * nb
                 + [_resident((nb, d, d)), _resident((nb, 1, d)), _resident((nb, bw, d)), _resident((d, d)),
                    _resident((1, d)), _resident((1, d))],
        out_specs=pl.BlockSpec((bm, d), row),
        out_shape=jax.ShapeDtypeStruct((t, d), F32),
        compiler_params=_params(("parallel",)),
        name="gated_merge",
    )(h, *branches, w_gate.astype(BF16), b_gate.reshape(nb, 1, d), w_br.astype(BF16), w_out.astype(BF16),
      ln_g.reshape(1, d), ln_b.reshape(1, d))


ROUTER_LANES = 128


def _moe_kernel(h_ref, wr_ref, br_ref, wup_ref, wdn_ref, g_ref, b_ref, out_ref, hid_ref):
    h = h_ref[...]
    hb = h.astype(BF16)
    bm = h.shape[0]
    logits = _dot_exact(h, wr_ref[...]) + br_ref[...]
    lane = lax.broadcasted_iota(jnp.int32, (bm, ROUTER_LANES), 1)
    lane_f = lane.astype(F32)
    is_g = lane < N_GROUPS
    glog = jnp.where(is_g, logits, NEG)
    gmax = jnp.max(glog, axis=-1, keepdims=True)
    g_sel = jnp.min(jnp.where(glog == gmax, lane_f, NO_INDEX), axis=-1, keepdims=True)
    pg_sel = 1.0 / jnp.sum(jnp.where(is_g, jnp.exp(glog - gmax), 0.0), axis=-1, keepdims=True)
    lo = N_GROUPS + g_sel * EXPERTS_PER_GROUP
    in_grp = jnp.logical_and(lane_f >= lo, lane_f < lo + EXPERTS_PER_GROUP)
    elog = jnp.where(in_grp, logits, NEG)
    emax = jnp.max(elog, axis=-1, keepdims=True)
    ee = jnp.where(in_grp, jnp.exp(elog - emax), 0.0)
    pe = ee / jnp.sum(ee, axis=-1, keepdims=True)
    pe_m = jnp.where(in_grp, pe, -1.0)
    v1 = jnp.max(pe_m, axis=-1, keepdims=True)
    i1 = jnp.min(jnp.where(pe_m == v1, lane_f, NO_INDEX), axis=-1, keepdims=True)
    pe_m2 = jnp.where(lane_f == i1, -1.0, pe_m)
    v2 = jnp.max(pe_m2, axis=-1, keepdims=True)
    i2 = jnp.min(jnp.where(pe_m2 == v2, lane_f, NO_INDEX), axis=-1, keepdims=True)
    norm = pg_sel / (v1 + v2)
    gate = jnp.where(lane_f == i1, v1 * norm, jnp.where(lane_f == i2, v2 * norm, 0.0))
    for e in range(N_EXPERTS):
        au = _dot(hb, wup_ref[e])
        a, u = au[:, :D_EXPERT], au[:, D_EXPERT:]
        w_e = jnp.sum(jnp.where(lane == N_GROUPS + e, gate, 0.0), axis=-1, keepdims=True)
        hid_ref[:, e * D_EXPERT:(e + 1) * D_EXPERT] = (w_e * (a * _sigmoid(a) * u)).astype(BF16)
    y = _dot(hid_ref[...], wdn_ref[...])
    out_ref[...] = _layer_norm(DEEPNORM_ALPHA * h + y, g_ref[...], b_ref[...])


def hierarchical_moe(h, w_rg, b_rg, w_re, b_re, w_up, w_down, ln_g, ln_b, bm):
    t, d = h.shape
    ne = N_EXPERTS
    w_r = jnp.concatenate([w_rg, w_re.transpose(1, 0, 2).reshape(d, ne)], axis=1)
    w_r = jnp.pad(w_r, ((0, 0), (0, ROUTER_LANES - w_r.shape[1])))
    b_r = jnp.pad(jnp.concatenate([b_rg, b_re.reshape(ne)]), (0, ROUTER_LANES - N_GROUPS - ne)).reshape(1, -1)
    row = lambda i: (i, 0)
    return pl.pallas_call(
        _moe_kernel,
        grid=(t // bm,),
        in_specs=[pl.BlockSpec((bm, d), row), _resident((d, ROUTER_LANES)), _resident((1, ROUTER_LANES)),
                  _resident((ne, d, 2 * D_EXPERT)), _resident((ne * D_EXPERT, d)), _resident((1, d)),
                  _resident((1, d))],
        out_specs=pl.BlockSpec((bm, d), row),
        out_shape=jax.ShapeDtypeStruct((t, d), F32),
        scratch_shapes=[pltpu.VMEM((bm, ne * D_EXPERT), BF16)],
        compiler_params=_params(("parallel",)),
        name="hierarchical_moe",
    )(h, w_r, b_r, w_up.astype(BF16), w_down.reshape(ne * D_EXPERT, d).astype(BF16),
      ln_g.reshape(1, d), ln_b.reshape(1, d))


def _pad_in_weight(w_in):
    d = w_in.shape[0]
    sizes = (768, 768, MLA_Q_RANK, MLA_KV_RANK, MLA_ROPE, 256, 384, 12, 256)
    offs = np.concatenate([[0], np.cumsum(sizes)])
    sb, moba, c_q, c_kv, k_rope, nsa_q, nsa_kv, nsa_g, mem_q = (w_in[:, offs[i]:offs[i + 1]] for i in range(len(sizes)))
    z = lambda n: jnp.zeros((d, n), w_in.dtype)
    q_scale = HEAD_DIM ** -0.5 * LOG2E
    scale_q = lambda qkv: jnp.concatenate([qkv[:, :BRANCH_W] * q_scale, qkv[:, BRANCH_W:]], axis=1)
    return jnp.concatenate([scale_q(sb), scale_q(moba), c_q, c_kv, k_rope, _rotate_half_cols(k_rope), z(64),
                            nsa_kv[:, :128], nsa_g, z(116), nsa_kv[:, 128:], nsa_q * q_scale, mem_q * q_scale], axis=1)


def kernel(x, mem, w_in, g_cq, g_ckv, w_uq, w_ukv, nsa_pe, w_phi_k1, w_phi_k2, w_phi_v1, w_phi_v2, w_mem_kv, w_br,
           w_gate, b_gate, w_out, ln1_g, ln1_b, w_rg, b_rg, w_re, b_re, w_up, w_down, ln2_g, ln2_b):
    b, s_len, d = x.shape
    s = -(-s_len // MOBA_BLOCK) * MOBA_BLOCK
    t = b * s
    n_mem = mem.shape[1]
    h = jnp.pad(x, ((0, 0), (0, s - s_len), (0, 0))).reshape(t, d)
    for l in range(w_in.shape[0]):
        a, f = in_projection(h, _pad_in_weight(w_in[l]).astype(BF16), bm=512)
        a3, f3 = a.reshape(b, s, IN_PAD), f.reshape(b, s, F32_W)
        q, kv, kr = mla_project(f, g_cq[l], g_ckv[l], w_uq[l], w_ukv[l], s, bm=512)
        kc, vct = nsa_compress(f3, nsa_pe[l], w_phi_k1[l], w_phi_k2[l], w_phi_v1[l], w_phi_v2[l])
        mkv = matmul(mem.reshape(b * n_mem, d), w_mem_kv[l].astype(BF16), bm=n_mem, out_dtype=BF16)
        branches = [sb_attention(a3),
                    moba_attention(a3),
                    mla_attention(q.reshape(b, s, -1), kv.reshape(b, s, -1), kr.reshape(b, s, -1)),
                    nsa_attention(a3, f3, kc, vct),
                    cross_attention(a3, mkv.reshape(b, n_mem, -1), qb=512)]
        h = gated_merge(h, [o.reshape(t, BRANCH_W) for o in branches], w_gate[l], b_gate[l], w_br[l], w_out[l],
                        ln1_g[l], ln1_b[l], bm=256)
        h = hierarchical_moe(h, w_rg[l], b_rg[l], w_re[l], b_re[l], w_up[l], w_down[l], ln2_g[l], ln2_b[l], bm=256)
    return h.reshape(b, s, d)[:, :s_len]
```

```python
import functools

import jax
import jax.numpy as jnp
import numpy as np
from jax import lax
from jax.experimental import pallas as pl
from jax.experimental.pallas import tpu as pltpu

DEPTH = 4
HEAD_DIM = 64
N_HEADS = 4
BRANCH_W = N_HEADS * HEAD_DIM
N_BRANCHES = 5
MOBA_BLOCK = 256
MOBA_TOPK = 3
MLA_Q_RANK = 256
MLA_KV_RANK = 128
MLA_NOPE = 64
MLA_ROPE = 32
MLA_V = 64
ROPE_THETA = 10000.0
NSA_CMP_LEN = 32
NSA_CMP_STRIDE = 16
NSA_SEL_LEN = 64
NSA_TOPN = 16
NSA_WINDOW = 512
NSA_PHI_HIDDEN = 128
N_GROUPS = 4
EXPERTS_PER_GROUP = 4
N_EXPERTS = N_GROUPS * EXPERTS_PER_GROUP
D_EXPERT = 256
DEEPNORM_ALPHA = (2.0 * DEPTH) ** 0.25
LN_EPS = 1e-5
RMS_EPS = 1e-6
NEG = -1e30
BIG = 1e30
NO_INDEX = 1e9

LANES = 128
BF16_SUBLANES = 16
VALUE_SLOT = HEAD_DIM + BF16_SUBLANES
QUERY_BLOCK = 256
KEY_TILE = 256
SOFTMAX_KEY_TILE = 256
SB_KEY_TILE = 512
LOG2E = 1.4426950408889634
VMEM_LIMIT_BYTES = 56 * 1024 * 1024

F32 = jnp.float32
BF16 = jnp.bfloat16
HIGHEST = lax.Precision.HIGHEST

COL_SB = 0
COL_MOBA = 768
COL_MLA = 1536
COL_NSA_CMP = 2048
COL_NSA_G = 2176
COL_NSA_SLC = 2304
COL_NSA_WIN = 2432
COL_NSA_Q = 2560
COL_MEM_Q = 2816
IN_PAD = 3072
COL_F32 = COL_MLA
F32_W = COL_NSA_SLC - COL_MLA


def _params(semantics):
    return pltpu.CompilerParams(dimension_semantics=semantics, vmem_limit_bytes=VMEM_LIMIT_BYTES)


def _dot(a, b):
    return jnp.dot(a, b, preferred_element_type=F32)


def _dot_exact(a, b):
    return jnp.dot(a, b, preferred_element_type=F32, precision=HIGHEST)


def _sigmoid(x):
    return 1.0 / (1.0 + jnp.exp(-x))


def _resident(shape):
    zeros = (0,) * len(shape)
    return pl.BlockSpec(shape, lambda *_: zeros, pipeline_mode=pl.Buffered(1))


def _in_proj_kernel(x_ref, w_ref, o_ref, f_ref):
    y = _dot(x_ref[...].astype(BF16), w_ref[...])
    o_ref[...] = y.astype(BF16)
    f_ref[...] = y[:, COL_F32:COL_F32 + F32_W]


def in_projection(h, w, bm):
    t, d = h.shape
    n = w.shape[1]
    return pl.pallas_call(
        _in_proj_kernel,
        grid=(t // bm,),
        in_specs=[pl.BlockSpec((bm, d), lambda i: (i, 0)), _resident((d, n))],
        out_specs=[pl.BlockSpec((bm, n), lambda i: (i, 0)), pl.BlockSpec((bm, F32_W), lambda i: (i, 0))],
        out_shape=[jax.ShapeDtypeStruct((t, n), BF16), jax.ShapeDtypeStruct((t, F32_W), F32)],
        compiler_params=_params(("parallel",)),
        name="in_projection",
    )(h, w)


def _mm_kernel(x_ref, w_ref, o_ref):
    o_ref[...] = _dot(x_ref[...].astype(BF16), w_ref[...]).astype(o_ref.dtype)


def matmul(x, w, *, bm, out_dtype=F32):
    m, k = x.shape
    n = w.shape[1]
    return pl.pallas_call(
        _mm_kernel,
        grid=(m // bm,),
        in_specs=[pl.BlockSpec((bm, k), lambda i: (i, 0)), _resident((k, n))],
        out_specs=pl.BlockSpec((bm, n), lambda i: (i, 0)),
        out_shape=jax.ShapeDtypeStruct((m, n), out_dtype),
        compiler_params=_params(("parallel",)),
        name="matmul",
    )(x, w)


def _softmax_tile(s, v_slot, m, acc, shift=None):
    shifts = shift if isinstance(shift, (list, tuple)) else [shift]
    rows = s.shape[0] // len(shifts)
    blocks = [s[i * rows:(i + 1) * rows] for i in range(len(shifts))]
    tops = [jnp.max(b, axis=0, keepdims=True) for b in blocks]
    m_new = jnp.maximum(m, functools.reduce(jnp.maximum, [t if sh is None else t - sh for t, sh in zip(tops, shifts)]))
    p = jnp.concatenate([jnp.exp2(b - (m_new if sh is None else m_new + sh)) for b, sh in zip(blocks, shifts)], axis=0)
    return m_new, jnp.exp2(m - m_new) * acc + _dot(v_slot, p.astype(BF16))


def _softmax_init(q):
    return (jnp.full((1, q), NEG, F32), jnp.zeros((VALUE_SLOT, q), F32))


def _softmax_result(acc):
    return acc[:HEAD_DIM] / acc[HEAD_DIM:HEAD_DIM + 1]


def _causal_softmax(qi, qb, kt, heads, scores, values, finish, finish_last, raw_ref):
    def tile(j, state, fin):
        return tuple(_softmax_tile(fin[h][0], values(j, h), *state[h], shift=fin[h][1]) for h in range(heads))

    def prefetch(j, slot):
        for h, raw in enumerate(scores(j)):
            raw_ref[slot, h] = raw

    def held(slot):
        return tuple(raw_ref[slot, h] for h in range(heads))

    def pair(i, state):
        prefetch(2 * i + 1, 0)
        state = tile(2 * i, state, finish(2 * i, held(1)))
        prefetch(jnp.minimum(2 * i + 2, last), 1)
        return tile(2 * i + 1, state, finish(2 * i + 1, held(0)))

    last = lax.div(qi * qb, kt)
    prefetch(last, 0)
    prefetch(0, 1)
    state = tile(last, tuple(_softmax_init(qb) for _ in range(heads)), finish_last(last, held(0)))
    state = lax.fori_loop(0, lax.div(last, 2), pair, state)
    state = lax.cond(lax.rem(last, 2) == 1, lambda st: tile(last - 1, st, finish(last - 1, held(1))),
                     lambda st: st, state)
    return [_softmax_result(acc) for _, acc in state]


def _transposed(x):
    return x.astype(F32).T


def _pair_rows(qt_all, h):
    g = qt_all[(h // 2) * LANES:(h // 2 + 1) * LANES]
    row = lax.broadcasted_iota(jnp.int32, g.shape, 0)
    keep = (row >= HEAD_DIM) if h % 2 else (row < HEAD_DIM)
    return jnp.where(keep, g, 0.0).astype(BF16)


def _pair_cols(h):
    return slice((h // 2) * LANES, (h // 2 + 1) * LANES)


def _head_rows(h):
    return slice(h * HEAD_DIM, (h + 1) * HEAD_DIM)


def _fill_transposed(src_ref, dst_ref, kb):
    def chunk(c, _):
        r0 = pl.multiple_of(c * kb, kb)
        dst_ref[:, pl.ds(r0, kb)] = src_ref[0, pl.ds(r0, kb), :].astype(F32).T.astype(BF16)
        return 0
    lax.fori_loop(0, src_ref.shape[1] // kb, chunk, 0)


def _fill_value_slots(src_ref, dst_ref, kb, first_rows):
    ones = jnp.ones((BF16_SUBLANES, kb), BF16)

    def chunk(c, _):
        r0 = pl.multiple_of(c * kb, kb)
        xt = src_ref[0, pl.ds(r0, kb), :].astype(F32).T
        for h, r in enumerate(first_rows):
            dst_ref[h * VALUE_SLOT:h * VALUE_SLOT + HEAD_DIM, pl.ds(r0, kb)] = xt[r:r + HEAD_DIM].astype(BF16)
            dst_ref[h * VALUE_SLOT + HEAD_DIM:(h + 1) * VALUE_SLOT, pl.ds(r0, kb)] = ones
        return 0
    lax.fori_loop(0, src_ref.shape[1] // kb, chunk, 0)


def _value_slot(h):
    return slice(h * VALUE_SLOT, (h + 1) * VALUE_SLOT)


def _store_heads(o_ref, outs):
    o_ref[0] = jnp.concatenate(outs, axis=0).T.astype(o_ref.dtype)


def _qkv_specs(s, kb, col):
    c = col // BRANCH_W
    return [pl.BlockSpec((1, kb, BRANCH_W), lambda bi, qi: (bi, qi, c)),
            pl.BlockSpec((1, s, BRANCH_W), lambda bi, qi: (bi, 0, c + 1)),
            pl.BlockSpec((1, s, BRANCH_W), lambda bi, qi: (bi, 0, c + 2))]


def _sb_kernel(q_ref, k_ref, v_ref, tri_ref, o_ref, vt_ref, raw_ref, *, kt):
    qi = pl.program_id(1)
    qb = q_ref.shape[1]
    kb = KEY_TILE
    pieces = kt // kb

    @pl.when(qi == 0)
    def _():
        _fill_transposed(v_ref, vt_ref, kb)

    qt_all = _transposed(q_ref[0])
    qts = [_pair_rows(qt_all, h) for h in range(N_HEADS)]
    tri = tri_ref[...]
    rel = lax.broadcasted_iota(jnp.int32, (kb, qb), 0) - lax.broadcasted_iota(jnp.int32, (kb, qb), 1)
    order = [(u, h) for u in reversed(range(pieces)) for h in range(N_HEADS)]

    def prefetch(j, slot):
        for u, h in order:
            raw_ref[slot, h, u * kb:(u + 1) * kb] = _dot(
                k_ref[0, pl.ds(pl.multiple_of(j * kt + u * kb, kb), kb), _pair_cols(h)], qts[h])

    def tile(j, state, slot, last):
        k0 = {u: pl.multiple_of(j * kt + u * kb, kb) for u in range(pieces)}
        log_beta, later, col_sum, past = {}, {}, {}, {}
        for u, h in order:
            z = raw_ref[slot, h, u * kb:(u + 1) * kb]
            nz = -z
            soft = jnp.log(1.0 + jnp.exp2(jnp.minimum(z, nz))) * LOG2E
            log_keep = jnp.minimum(nz, 0.0) - soft
            log_beta[u, h] = log_keep + z
            if last:
                past[u] = rel < qi * qb - k0[u]
                log_keep = jnp.where(past[u], log_keep, 0.0)
            sums = _dot(tri, log_keep.astype(BF16))
            later[u, h] = sums[:kb]
            col_sum[u, h] = sums[kb:kb + 1]
        state = list(state)
        for u, h in order:
            carry, acc = state[h]
            a = jnp.exp2(log_beta[u, h] + later[u, h])
            if last:
                a = jnp.where(past[u], a, 0.0)
            part = _dot(vt_ref[_head_rows(h), pl.ds(k0[u], kb)], a.astype(BF16))
            state[h] = (carry + col_sum[u, h], acc + jnp.exp2(carry) * part)
        return tuple(state)

    def pair(i, state):
        first = last - 1 - 2 * i
        prefetch(first - 1, 0)
        state = tile(first, state, 1, False)
        prefetch(jnp.maximum(first - 2, 0), 1)
        return tile(first - 1, state, 0, False)

    last = lax.div(qi * qb, kt)
    prefetch(last, 0)
    prefetch(jnp.maximum(last - 1, 0), 1)
    state = tuple((jnp.zeros((1, qb), F32), jnp.zeros((HEAD_DIM, qb), F32)) for _ in range(N_HEADS))
    state = tile(last, state, 0, True)
    state = lax.fori_loop(0, lax.div(last, 2), pair, state)
    state = lax.cond(lax.rem(last, 2) == 1, lambda st: tile(0, st, 1, False), lambda st: st, state)
    _store_heads(o_ref, [st[1] for st in state])


def sb_attention(a):
    b, s, _ = a.shape
    kb, qb = KEY_TILE, QUERY_BLOCK
    tri = jnp.asarray(np.concatenate([np.triu(np.ones((kb, kb), np.float32), 1), np.ones((BF16_SUBLANES, kb), np.float32)]),
                      BF16)
    return pl.pallas_call(
        functools.partial(_sb_kernel, kt=min(SB_KEY_TILE, s)),
        grid=(b, s // qb),
        in_specs=_qkv_specs(s, qb, COL_SB) + [_resident(tri.shape)],
        out_specs=pl.BlockSpec((1, qb, BRANCH_W), lambda bi, qi: (bi, qi, 0)),
        out_shape=jax.ShapeDtypeStruct((b, s, BRANCH_W), BF16),
        scratch_shapes=[pltpu.VMEM((BRANCH_W, s), BF16), pltpu.VMEM((2, N_HEADS, min(SB_KEY_TILE, s), qb), F32)],
        compiler_params=_params(("parallel", "arbitrary")),
        name="sb_attention",
    )(a, a, a, tri)


def _alibi_slopes(n):
    return np.power(2.0, -8.0 * np.arange(1, n + 1, dtype=np.float64) / n).astype(np.float32)


def _select_top(scores, ids, count, floor):
    def step(_, c):
        out = []
        for score, sel in c:
            mx = jnp.max(score, axis=0, keepdims=True)
            idx = jnp.min(jnp.where(score == mx, ids, NO_INDEX), axis=0, keepdims=True)
            pick = ids == idx
            out.append((jnp.where(pick, -jnp.inf, score), jnp.where(jnp.logical_and(pick, mx > floor), 1.0, sel)))
        return tuple(out)
    done = lax.fori_loop(0, count, step, tuple((s, jnp.zeros_like(s)) for s in scores))
    return [sel for _, sel in done]


def _moba_kernel(slope_ref, q_ref, k_ref, v_ref, o_ref, vt_ref, km_ref, sel_ref, raw_ref, bias_ref, *, kt, topk):
    qi = pl.program_id(1)
    kb = q_ref.shape[1]
    nblk = km_ref.shape[0]
    per_tile = kt // kb
    rk = lax.broadcasted_iota(jnp.int32, (kt, kb), 0)
    rel = rk - lax.broadcasted_iota(jnp.int32, (kt, kb), 1)

    @pl.when(qi == 0)
    def _():
        def key_mean(c, _):
            km_ref[pl.ds(c, 1), :] = jnp.mean(k_ref[0, pl.ds(pl.multiple_of(c * kb, kb), kb), :].astype(F32),
                                              axis=0, keepdims=True)
            return 0
        _fill_value_slots(v_ref, vt_ref, kb, [h * HEAD_DIM for h in range(N_HEADS)])
        lax.fori_loop(0, nblk, key_mean, 0)
        for h in range(N_HEADS):
            bias_ref[h] = slope_ref[h] * rel.astype(F32)

    qt_all = _transposed(q_ref[0])
    qts = [_pair_rows(qt_all, h) for h in range(N_HEADS)]
    ids = lax.broadcasted_iota(jnp.int32, (nblk, kb), 0)
    row = lax.broadcasted_iota(jnp.int32, qt_all.shape, 0)
    km = km_ref[...]
    gscores = []
    for h in range(N_HEADS):
        in_head = jnp.logical_and(row >= h * HEAD_DIM, row < (h + 1) * HEAD_DIM)
        gscore = _dot_exact(km, jnp.where(in_head, qt_all, 0.0))
        gscores.append(jnp.where(ids < qi, gscore, NEG))
    for h, sel in enumerate(_select_top(gscores, ids.astype(F32), topk, 0.5 * NEG)):
        sel_ref[h] = jnp.where(sel > 0.5, 0.0, NEG)


    def scores(j):
        k0 = pl.multiple_of(j * kt, kt)
        return tuple(_dot(k_ref[0, pl.ds(k0, kt), _pair_cols(h)], qts[h]) for h in range(N_HEADS))

    def values(j, h):
        return vt_ref[_value_slot(h), pl.ds(pl.multiple_of(j * kt, kt), kt)]

    def unselected(j, h):
        rows = [jnp.broadcast_to(sel_ref[h, pl.ds(j * per_tile + r, 1), :], (kb, kb)) for r in range(per_tile)]
        return jnp.concatenate(rows, axis=0)

    def finish(j, raw):
        off = (qi * kb - j * kt).astype(F32)
        return tuple((raw[h] + bias_ref[h],
                      [slope_ref[h] * off - sel_ref[h, pl.ds(j * per_tile + r, 1), :] for r in range(per_tile)])
                     for h in range(N_HEADS))

    def finish_last(j, raw):
        d0 = qi * kb - j * kt
        own_causal = jnp.logical_and(rk >= d0, rel <= d0)
        out = []
        for h in range(N_HEADS):
            s = raw[h] + bias_ref[h]
            out.append((jnp.where(own_causal, s, s + unselected(j, h)), slope_ref[h] * d0.astype(F32)))
        return tuple(out)

    _store_heads(o_ref, _causal_softmax(qi, kb, kt, N_HEADS, scores, values, finish, finish_last, raw_ref))


def moba_attention(a):
    b, s, _ = a.shape
    kb = MOBA_BLOCK
    nblk = s // kb
    topk = min(MOBA_TOPK, nblk - 1)
    slopes = jnp.asarray(_alibi_slopes(N_HEADS) * np.float32(LOG2E))
    c = COL_MOBA // BRANCH_W
    return pl.pallas_call(
        functools.partial(_moba_kernel, kt=min(SOFTMAX_KEY_TILE, s), topk=topk),
        grid_spec=pltpu.PrefetchScalarGridSpec(
            num_scalar_prefetch=1,
            grid=(b, nblk),
            in_specs=[pl.BlockSpec((1, kb, BRANCH_W), lambda bi, qi, sl: (bi, qi, c)),
                      pl.BlockSpec((1, s, BRANCH_W), lambda bi, qi, sl: (bi, 0, c + 1)),
                      pl.BlockSpec((1, s, BRANCH_W), lambda bi, qi, sl: (bi, 0, c + 2))],
            out_specs=pl.BlockSpec((1, kb, BRANCH_W), lambda bi, qi, sl: (bi, qi, 0)),
            scratch_shapes=[pltpu.VMEM((N_HEADS * VALUE_SLOT, s), BF16),
                            pltpu.VMEM((nblk, BRANCH_W), F32),
                            pltpu.VMEM((N_HEADS, nblk, kb), F32),
                            pltpu.VMEM((2, N_HEADS, min(SOFTMAX_KEY_TILE, s), kb), F32),
                            pltpu.VMEM((N_HEADS, min(SOFTMAX_KEY_TILE, s), kb), F32)]),
        out_shape=jax.ShapeDtypeStruct((b, s, BRANCH_W), BF16),
        compiler_params=_params(("parallel", "arbitrary")),
        name="moba_attention",
    )(slopes, a, a, a)


MLA_SLOT = 128


def _rms(x, g):
    return x * lax.rsqrt(jnp.mean(x * x, axis=-1, keepdims=True) + RMS_EPS) * g


def _mla_proj_kernel(x_ref, gq_ref, gkv_ref, wq_ref, wqr_ref, wkv_ref, cq_ref, sq_ref, ck_ref, sk_ref,
                     q_ref, kv_ref, kr_ref):
    x = x_ref[...]
    c_q = _rms(x[:, :MLA_Q_RANK], gq_ref[...]).astype(BF16)
    c_kv = _rms(x[:, MLA_Q_RANK:MLA_Q_RANK + MLA_KV_RANK], gkv_ref[...]).astype(BF16)
    q_ref[...] = (_dot(c_q, wq_ref[...]) * cq_ref[...] + _dot(c_q, wqr_ref[...]) * sq_ref[...]).astype(q_ref.dtype)
    kv_ref[...] = _dot(c_kv, wkv_ref[...]).astype(kv_ref.dtype)
    tail = x[:, MLA_Q_RANK + MLA_KV_RANK:]
    rot = pltpu.roll(tail, LANES - MLA_ROPE, axis=1)
    kr_ref[...] = (tail * ck_ref[...] + rot * sk_ref[...]).astype(kr_ref.dtype)


def _rope_tables(s_len):
    half = MLA_ROPE // 2
    freqs = jnp.power(ROPE_THETA, -jnp.arange(half, dtype=F32) / half)
    ang = jnp.arange(s_len).astype(F32)[:, None] * freqs
    cos = jnp.concatenate([jnp.cos(ang)] * 2, axis=-1)
    sin = jnp.concatenate([jnp.sin(ang)] * 2, axis=-1)
    zk = jnp.zeros((s_len, LANES - MLA_ROPE), F32)
    zq = jnp.zeros((s_len, MLA_SLOT - MLA_NOPE - MLA_ROPE), F32)
    cq = jnp.tile(jnp.concatenate([jnp.ones((s_len, MLA_NOPE), F32), cos, zq], axis=-1), (1, N_HEADS))
    sq = jnp.tile(jnp.concatenate([jnp.zeros((s_len, MLA_NOPE), F32), sin, zq], axis=-1), (1, N_HEADS))
    return cq, sq, jnp.concatenate([cos, zk], axis=-1), jnp.concatenate([sin, zk], axis=-1)


def _rotate_half_cols(w):
    half = w.shape[-1] // 2
    return jnp.concatenate([-w[..., half:], w[..., :half]], axis=-1)


def mla_project(f, g_cq, g_ckv, w_uq, w_ukv, s_len, bm):
    t = f.shape[0]
    wq = w_uq.reshape(MLA_Q_RANK, N_HEADS, MLA_NOPE + MLA_ROPE)
    pad = jnp.zeros((MLA_Q_RANK, N_HEADS, MLA_SLOT - MLA_NOPE - MLA_ROPE), wq.dtype)
    wq_s = jnp.concatenate([wq, pad], axis=-1).reshape(MLA_Q_RANK, -1)
    wqr_s = jnp.concatenate([jnp.zeros_like(wq[..., :MLA_NOPE]), _rotate_half_cols(wq[..., MLA_NOPE:]), pad],
                            axis=-1).reshape(MLA_Q_RANK, -1)
    cq, sq, ck, sk = _rope_tables(s_len)
    q_scale = (MLA_NOPE + MLA_ROPE) ** -0.5 * LOG2E
    nrow = s_len // bm
    row = lambda i: (i, 0)
    pos = lambda i: (i % nrow, 0)
    qw = N_HEADS * MLA_SLOT
    kvw = N_HEADS * (MLA_NOPE + MLA_V)
    return pl.pallas_call(
        _mla_proj_kernel,
        grid=(t // bm,),
        in_specs=[pl.BlockSpec((bm, 512), row),
                  _resident((1, MLA_Q_RANK)), _resident((1, MLA_KV_RANK)),
                  _resident((MLA_Q_RANK, qw)), _resident((MLA_Q_RANK, qw)), _resident((MLA_KV_RANK, kvw)),
                  pl.BlockSpec((bm, qw), pos), pl.BlockSpec((bm, qw), pos),
                  pl.BlockSpec((bm, LANES), pos), pl.BlockSpec((bm, LANES), pos)],
        out_specs=[pl.BlockSpec((bm, qw), row), pl.BlockSpec((bm, kvw), row), pl.BlockSpec((bm, LANES), row)],
        out_shape=[jax.ShapeDtypeStruct((t, qw), BF16), jax.ShapeDtypeStruct((t, kvw), BF16),
                   jax.ShapeDtypeStruct((t, LANES), BF16)],
        compiler_params=_params(("parallel",)),
        name="mla_project",
    )(f, g_cq.reshape(1, -1), g_ckv.reshape(1, -1), (wq_s * q_scale).astype(BF16), (wqr_s * q_scale).astype(BF16),
      w_ukv.astype(BF16), cq, sq, ck, sk)


def _mla_kernel(q_ref, kv_ref, kr_ref, o_ref, vt_ref, raw_ref, *, kt):
    qi = pl.program_id(1)
    qb = q_ref.shape[1]

    @pl.when(qi == 0)
    def _():
        _fill_value_slots(kv_ref, vt_ref, KEY_TILE, [h * LANES + MLA_NOPE for h in range(N_HEADS)])

    qt_all = _transposed(q_ref[0])
    qts = []
    for h in range(N_HEADS):
        g = qt_all[h * MLA_SLOT:(h + 1) * MLA_SLOT]
        row = lax.broadcasted_iota(jnp.int32, g.shape, 0)
        qts.append(jnp.concatenate([jnp.where(row < MLA_NOPE, g, 0.0), g[MLA_NOPE:], jnp.zeros_like(g[MLA_NOPE:])],
                                   axis=0).astype(BF16))
    rel = lax.broadcasted_iota(jnp.int32, (kt, qb), 0) - lax.broadcasted_iota(jnp.int32, (kt, qb), 1)

    def scores(j):
        k0 = pl.multiple_of(j * kt, kt)
        kr = kr_ref[0, pl.ds(k0, kt), :]
        return tuple(_dot(jnp.concatenate([kv_ref[0, pl.ds(k0, kt), h * LANES:(h + 1) * LANES], kr], axis=1), qts[h])
                     for h in range(N_HEADS))

    def values(j, h):
        return vt_ref[_value_slot(h), pl.ds(pl.multiple_of(j * kt, kt), kt)]

    def finish_last(j, raw):
        causal = rel <= qi * qb - j * kt
        return tuple((jnp.where(causal, s, NEG), None) for s in raw)

    outs = _causal_softmax(qi, qb, kt, N_HEADS, scores, values,
                           lambda j, raw: tuple((s, None) for s in raw), finish_last, raw_ref)
    _store_heads(o_ref, outs)


def mla_attention(q, kv, kr):
    b, s, qw = q.shape
    kb = QUERY_BLOCK
    return pl.pallas_call(
        functools.partial(_mla_kernel, kt=min(SOFTMAX_KEY_TILE, s)),
        grid=(b, s // kb),
        in_specs=[pl.BlockSpec((1, kb, qw), lambda bi, qi: (bi, qi, 0)),
                  pl.BlockSpec((1, s, kv.shape[2]), lambda bi, qi: (bi, 0, 0)),
                  pl.BlockSpec((1, s, LANES), lambda bi, qi: (bi, 0, 0))],
        out_specs=pl.BlockSpec((1, kb, BRANCH_W), lambda bi, qi: (bi, qi, 0)),
        out_shape=jax.ShapeDtypeStruct((b, s, BRANCH_W), BF16),
        scratch_shapes=[pltpu.VMEM((N_HEADS * VALUE_SLOT, s), BF16),
                        pltpu.VMEM((2, N_HEADS, min(SOFTMAX_KEY_TILE, s), kb), F32)],
        compiler_params=_params(("parallel", "arbitrary")),
        name="mla_attention",
    )(q, kv, kr)


def _cross_kernel(q_ref, kv_ref, o_ref):
    qt_all = _transposed(q_ref[0])
    kv = kv_ref[0]
    vt = _transposed(kv[:, BRANCH_W:]).astype(BF16)
    outs = []
    for h in range(N_HEADS):
        s = _dot(kv[:, _pair_cols(h)], _pair_rows(qt_all, h))
        p = jnp.exp2(s - jnp.max(s, axis=0, keepdims=True))
        outs.append(_dot(vt[_head_rows(h)], p.astype(BF16)) / jnp.sum(p, axis=0, keepdims=True))
    _store_heads(o_ref, outs)


def cross_attention(a, mkv, qb):
    b, s, _ = a.shape
    n = mkv.shape[1]
    c = COL_MEM_Q // BRANCH_W
    return pl.pallas_call(
        _cross_kernel,
        grid=(b, s // qb),
        in_specs=[pl.BlockSpec((1, qb, BRANCH_W), lambda bi, qi: (bi, qi, c)),
                  pl.BlockSpec((1, n, 2 * BRANCH_W), lambda bi, qi: (bi, 0, 0))],
        out_specs=pl.BlockSpec((1, qb, BRANCH_W), lambda bi, qi: (bi, qi, 0)),
        out_shape=jax.ShapeDtypeStruct((b, s, BRANCH_W), BF16),
        compiler_params=_params(("parallel", "parallel")),
        name="cross_attention",
    )(a, mkv)


def _compress_kernel(x_ref, pe_ref, w1_ref, w2_ref, kc_ref, vct_ref):
    n = kc_ref.shape[1]
    first = jnp.zeros((n, 2 * NSA_PHI_HIDDEN), F32)
    second = jnp.zeros((n, 2 * NSA_PHI_HIDDEN), F32)
    for r in range(NSA_CMP_STRIDE):
        x = x_ref[0, pl.ds(r, n, stride=NSA_CMP_STRIDE), :]
        first = first + _dot((x + pe_ref[r:r + 1]).astype(BF16), w1_ref[r])
        second = second + _dot((x + pe_ref[NSA_CMP_STRIDE + r:NSA_CMP_STRIDE + r + 1]).astype(BF16),
                               w1_ref[NSA_CMP_STRIDE + r])
    hidden = jax.nn.gelu(first + pltpu.roll(second, n - 1, axis=0))
    out = _dot(hidden.astype(BF16), w2_ref[...])
    kc_ref[0] = out.astype(BF16)
    vct_ref[0] = out.T[HEAD_DIM:].astype(BF16)


def _pair_diag(wk, wv):
    z = jnp.zeros_like(wk)
    return jnp.concatenate([jnp.concatenate([wk, z], axis=-1), jnp.concatenate([z, wv], axis=-1)], axis=-2)


def nsa_compress(f, nsa_pe, w_k1, w_k2, w_v1, w_v2):
    b, s, _ = f.shape
    n = s // NSA_CMP_STRIDE
    hd = HEAD_DIM
    w1 = _pair_diag(w_k1.reshape(NSA_CMP_LEN, hd, -1), w_v1.reshape(NSA_CMP_LEN, hd, -1)).astype(BF16)
    w2 = _pair_diag(w_k2, w_v2).astype(BF16)
    pe = jnp.concatenate([nsa_pe, nsa_pe], axis=-1)
    c = (COL_NSA_CMP - COL_F32) // LANES
    return pl.pallas_call(
        _compress_kernel,
        grid=(b,),
        in_specs=[pl.BlockSpec((1, s, LANES), lambda bi: (bi, 0, c)),
                  _resident(pe.shape), _resident(w1.shape), _resident(w2.shape)],
        out_specs=[pl.BlockSpec((1, n, LANES), lambda bi: (bi, 0, 0)),
                   pl.BlockSpec((1, hd, n), lambda bi: (bi, 0, 0))],
        out_shape=[jax.ShapeDtypeStruct((b, n, LANES), BF16), jax.ShapeDtypeStruct((b, hd, n), BF16)],
        compiler_params=_params(("parallel",)),
        name="nsa_compress",
    )(f, pe, w1, w2)


def _nsa_kernel(q_ref, g_ref, slope_ref, kc_ref, vct_ref, ovt_ref, slc_ref, win_ref, o_ref,
                vst_ref, vwt_ref, sel_ref, bias_ref, cbias_ref, raw_ref, order_ref, *, topn, n_cmp):
    qi = pl.program_id(1)
    qn = q_ref.shape[1]
    kb = KEY_TILE
    lanes = N_HEADS * qn
    dv = HEAD_DIM
    q0 = qi * qn

    slope = slope_ref[...]
    ql = jnp.bitwise_and(lax.broadcasted_iota(jnp.int32, (1, lanes), 1), qn - 1)
    qpos = q0 + ql
    kid = lax.broadcasted_iota(jnp.int32, (kb, 1), 0)
    ncp = kc_ref.shape[1]
    cid = lax.broadcasted_iota(jnp.int32, (ncp, 1), 0)
    cmp_last = cid * NSA_CMP_STRIDE + (NSA_CMP_LEN - 1)
    cmp_end = jnp.where(cid < n_cmp, cmp_last, 1 << 30)

    @pl.when(qi == 0)
    def _():
        _fill_value_slots(slc_ref, vst_ref, kb, [HEAD_DIM])
        _fill_value_slots(win_ref, vwt_ref, kb, [HEAD_DIM])
        bias_ref[...] = slope * (ql - kid).astype(F32)
        cbias_ref[...] = slope * (ql - cmp_last).astype(F32)

    qt_all = _transposed(q_ref[0])
    qt = jnp.concatenate([qt_all[_head_rows(h)] for h in range(N_HEADS)], axis=1)
    qt = jnp.concatenate([qt, jnp.zeros_like(qt)], axis=0).astype(BF16)
    jd = lax.div(q0, kb)
    win_tiles = [jnp.maximum(jd - back, 0) for back in range(NSA_WINDOW // kb + 1)]

    def raw_scores(kv_ref, j):
        return _dot(kv_ref[0, pl.ds(pl.multiple_of(j * kb, kb), kb), :], qt)

    raw_c = _dot(kc_ref[0], qt)
    raw_d = raw_scores(slc_ref, jd)
    raw_w = [raw_scores(win_ref, j) for j in win_tiles]
    bias = bias_ref[...]
    causal = ql - kid >= 0

    half = lanes // 2
    o_halves, p_sum = [], None
    for part in range(2):
        sl = slice(part * half, (part + 1) * half)
        valid = cmp_end - ql[:, sl] <= q0
        s = jnp.where(valid, raw_c[:, sl] - cbias_ref[:, sl], NEG)
        top = jnp.max(s, axis=0, keepdims=True)
        e = jnp.exp2(s - top)
        p_c = e * jnp.where(top > 0.5 * NEG, 1.0 / jnp.sum(e, axis=0, keepdims=True), 0.0)
        o_halves.append(_dot(vct_ref[0], p_c.astype(BF16)))
        for hh in range(half // qn):
            p_h = p_c[:, hh * qn:(hh + 1) * qn]
            p_sum = p_h if p_sum is None else p_sum + p_h
    o_c = jnp.concatenate(o_halves, axis=1)

    imp = _dot_exact(ovt_ref[...], p_sum)
    nsel = imp.shape[0]
    sid = lax.broadcasted_iota(jnp.int32, (nsel, qn), 0)
    cur = jnp.right_shift(qpos[:, 0:qn], NSA_SEL_LEN.bit_length() - 1)
    forced = jnp.logical_or(sid == 0, sid == cur)
    score = jnp.where(forced, BIG, jnp.where(sid < cur, imp, NEG))
    sel, = _select_top([score], sid.astype(F32), topn, 0.5 * NEG)
    sel_ref[...] = jnp.concatenate([jnp.where(sel > 0.5, 0.0, NEG)] * N_HEADS, axis=1)
    per_tile = kb // NSA_SEL_LEN
    n = jnp.int32(0)
    for j in range(nsel // per_tile):
        order_ref[n] = j
        wanted = jnp.max(sel[j * per_tile:(j + 1) * per_tile]) > 0.5
        n = n + jnp.logical_and(wanted, j < jd).astype(jnp.int32)

    def sel_tile(j, c, raw, own):
        s = raw - bias
        if own:
            s = jnp.where(causal, s, NEG)
        off = slope * ((jd - j) * kb).astype(F32)
        shifts = [off - sel_ref[pl.ds(j * per_tile + r, 1), :] for r in range(per_tile)]
        return _softmax_tile(s, vst_ref[:, pl.ds(pl.multiple_of(j * kb, kb), kb)], *c, shift=shifts)

    def fetch(k, slot):
        raw_ref[slot] = raw_scores(slc_ref, order_ref[k])

    def work(k, slot, c):
        return sel_tile(order_ref[k], c, raw_ref[slot], False)

    def pair(i, c):
        fetch(2 * i + 1, 1)
        c = work(2 * i, 0, c)
        fetch(jnp.minimum(2 * i + 2, n), 0)
        return work(2 * i + 1, 1, c)

    fetch(0, 0)
    c = sel_tile(jd, _softmax_init(lanes), raw_d, True)
    c = lax.fori_loop(0, lax.div(n, 2), pair, c)
    c = lax.cond(lax.rem(n, 2) == 1, lambda c: work(n - 1, 0, c), lambda c: c, c)
    o_s = _softmax_result(c[1])

    tops, masked = [], []
    for back, raw in enumerate(raw_w):
        s = raw - bias
        if back == 0:
            s = jnp.where(causal, s, NEG)
        else:
            inside = NSA_WINDOW - back * kb if back * kb + kb > NSA_WINDOW else 2 * kb
            s = jnp.where(ql - kid < jnp.where(jd - back >= 0, inside, -2 * kb), s, NEG)
        masked.append(s)
        tops.append(jnp.max(s, axis=0, keepdims=True) - slope * float(back * kb))
    m_w = functools.reduce(jnp.maximum, tops)
    acc_w = jnp.zeros((VALUE_SLOT, lanes), F32)
    for back, s in enumerate(masked):
        p = jnp.exp2(s - (m_w + slope * float(back * kb)))
        acc_w = acc_w + _dot(vwt_ref[:, pl.ds(pl.multiple_of(win_tiles[back] * kb, kb), kb)], p.astype(BF16))
    o_w = _softmax_result(acc_w)

    gt = _sigmoid(g_ref[0]).T
    def gate(ci):
        return jnp.concatenate([gt[h * 3 + ci:h * 3 + ci + 1] for h in range(N_HEADS)], axis=1)
    out = gate(0) * o_c + gate(1) * o_s + gate(2) * o_w
    _store_heads(o_ref, [out[:, h * qn:(h + 1) * qn] for h in range(N_HEADS)])


def nsa_attention(a, f, kc, vct):
    b, s, _ = a.shape
    ncp = kc.shape[1]
    n_cmp = ncp - NSA_CMP_LEN // NSA_CMP_STRIDE + 1
    nsel = s // NSA_SEL_LEN
    qn = QUERY_BLOCK
    lanes = N_HEADS * qn
    cs = np.arange(ncp) * NSA_CMP_STRIDE
    ss = np.arange(nsel) * NSA_SEL_LEN
    ov = ((cs[:, None] < ss[None, :] + NSA_SEL_LEN) & (cs[:, None] + NSA_CMP_LEN > ss[None, :])
          & (np.arange(ncp)[:, None] < n_cmp)).astype(np.float32)
    slopes = jnp.asarray(np.repeat(_alibi_slopes(N_HEADS) * np.float32(LOG2E), qn)[None, :])
    per_b = lambda bi, qi: (bi, 0, 0)
    return pl.pallas_call(
        functools.partial(_nsa_kernel, topn=min(NSA_TOPN, nsel), n_cmp=n_cmp),
        grid=(b, s // qn),
        in_specs=[pl.BlockSpec((1, qn, BRANCH_W), lambda bi, qi: (bi, qi, COL_NSA_Q // BRANCH_W)),
                  pl.BlockSpec((1, qn, LANES), lambda bi, qi: (bi, qi, (COL_NSA_G - COL_F32) // LANES)),
                  _resident((1, lanes)),
                  pl.BlockSpec((1, ncp, LANES), per_b),
                  pl.BlockSpec((1, HEAD_DIM, ncp), per_b),
                  _resident((nsel, ncp)),
                  pl.BlockSpec((1, s, LANES), lambda bi, qi: (bi, 0, COL_NSA_SLC // LANES)),
                  pl.BlockSpec((1, s, LANES), lambda bi, qi: (bi, 0, COL_NSA_WIN // LANES))],
        out_specs=pl.BlockSpec((1, qn, BRANCH_W), lambda bi, qi: (bi, qi, 0)),
        out_shape=jax.ShapeDtypeStruct((b, s, BRANCH_W), BF16),
        scratch_shapes=[pltpu.VMEM((VALUE_SLOT, s), BF16), pltpu.VMEM((VALUE_SLOT, s), BF16),
                        pltpu.VMEM((nsel, lanes), F32), pltpu.VMEM((KEY_TILE, lanes), F32),
                        pltpu.VMEM((ncp, lanes), F32), pltpu.VMEM((2, KEY_TILE, lanes), F32),
                        pltpu.SMEM((s // KEY_TILE + 1,), jnp.int32)],
        compiler_params=_params(("parallel", "arbitrary")),
        name="nsa_attention",
    )(a, f, slopes, kc, vct, jnp.asarray(ov.T), a, a)


def _layer_norm(r, g, b):
    mu = jnp.mean(r, axis=-1, keepdims=True)
    c = r - mu
    var = jnp.mean(c * c, axis=-1, keepdims=True)
    return c * lax.rsqrt(var + LN_EPS) * g + b


def _merge_kernel(h_ref, o0_ref, o1_ref, o2_ref, o3_ref, o4_ref, wg_ref, bg_ref, wbr_ref, wout_ref, g_ref, b_ref,
                  out_ref):
    h = h_ref[...]
    hb = h.astype(BF16)
    merged = jnp.zeros(h.shape, F32)
    for i, o_ref in enumerate((o0_ref, o1_ref, o2_ref, o3_ref, o4_ref)):
        gate = _sigmoid(_dot(hb, wg_ref[i]) + bg_ref[i])
        merged = merged + gate * _dot(o_ref[...], wbr_ref[i])
    y = _dot(merged.astype(BF16), wout_ref[...])
    out_ref[...] = _layer_norm(DEEPNORM_ALPHA * h + y, g_ref[...], b_ref[...])


def gated_merge(h, branches, w_gate, b_gate, w_br, w_out, ln_g, ln_b, bm):
    t, d = h.shape
    nb, bw = len(branches), branches[0].shape[1]
    row = lambda i: (i, 0)
    return pl.pallas_call(
        _merge_kernel,
        grid=(t // bm,),
        in_specs=[pl.BlockSpec((bm, d), row)] + [pl.BlockSpec((bm, bw), row)] * nb
                 + [_resident((nb, d, d)), _resident((nb, 1, d)), _resident((nb, bw, d)), _resident((d, d)),
                    _resident((1, d)), _resident((1, d))],
        out_specs=pl.BlockSpec((bm, d), row),
        out_shape=jax.ShapeDtypeStruct((t, d), F32),
        compiler_params=_params(("parallel",)),
        name="gated_merge",
    )(h, *branches, w_gate.astype(BF16), b_gate.reshape(nb, 1, d), w_br.astype(BF16), w_out.astype(BF16),
      ln_g.reshape(1, d), ln_b.reshape(1, d))


ROUTER_LANES = 128


def _moe_kernel(h_ref, wr_ref, br_ref, wup_ref, wdn_ref, g_ref, b_ref, out_ref, hid_ref):
    h = h_ref[...]
    hb = h.astype(BF16)
    bm = h.shape[0]
    logits = _dot_exact(h, wr_ref[...]) + br_ref[...]
    lane = lax.broadcasted_iota(jnp.int32, (bm, ROUTER_LANES), 1)
    lane_f = lane.astype(F32)
    is_g = lane < N_GROUPS
    glog = jnp.where(is_g, logits, NEG)
    gmax = jnp.max(glog, axis=-1, keepdims=True)
    g_sel = jnp.min(jnp.where(glog == gmax, lane_f, NO_INDEX), axis=-1, keepdims=True)
    pg_sel = 1.0 / jnp.sum(jnp.where(is_g, jnp.exp(glog - gmax), 0.0), axis=-1, keepdims=True)
    lo = N_GROUPS + g_sel * EXPERTS_PER_GROUP
    in_grp = jnp.logical_and(lane_f >= lo, lane_f < lo + EXPERTS_PER_GROUP)
    elog = jnp.where(in_grp, logits, NEG)
    emax = jnp.max(elog, axis=-1, keepdims=True)
    ee = jnp.where(in_grp, jnp.exp(elog - emax), 0.0)
    pe = ee / jnp.sum(ee, axis=-1, keepdims=True)
    pe_m = jnp.where(in_grp, pe, -1.0)
    v1 = jnp.max(pe_m, axis=-1, keepdims=True)
    i1 = jnp.min(jnp.where(pe_m == v1, lane_f, NO_INDEX), axis=-1, keepdims=True)
    pe_m2 = jnp.where(lane_f == i1, -1.0, pe_m)
    v2 = jnp.max(pe_m2, axis=-1, keepdims=True)
    i2 = jnp.min(jnp.where(pe_m2 == v2, lane_f, NO_INDEX), axis=-1, keepdims=True)
    norm = pg_sel / (v1 + v2)
    gate = jnp.where(lane_f == i1, v1 * norm, jnp.where(lane_f == i2, v2 * norm, 0.0))
    for e in range(N_EXPERTS):
        au = _dot(hb, wup_ref[e])
        a, u = au[:, :D_EXPERT], au[:, D_EXPERT:]
        w_e = jnp.sum(jnp.where(lane == N_GROUPS + e, gate, 0.0), axis=-1, keepdims=True)
        hid_ref[:, e * D_EXPERT:(e + 1) * D_EXPERT] = (w_e * (a * _sigmoid(a) * u)).astype(BF16)
    y = _dot(hid_ref[...], wdn_ref[...])
    out_ref[...] = _layer_norm(DEEPNORM_ALPHA * h + y, g_ref[...], b_ref[...])


def hierarchical_moe(h, w_rg, b_rg, w_re, b_re, w_up, w_down, ln_g, ln_b, bm):
    t, d = h.shape
    ne = N_EXPERTS
    w_r = jnp.concatenate([w_rg, w_re.transpose(1, 0, 2).reshape(d, ne)], axis=1)
    w_r = jnp.pad(w_r, ((0, 0), (0, ROUTER_LANES - w_r.shape[1])))
    b_r = jnp.pad(jnp.concatenate([b_rg, b_re.reshape(ne)]), (0, ROUTER_LANES - N_GROUPS - ne)).reshape(1, -1)
    row = lambda i: (i, 0)
    return pl.pallas_call(
        _moe_kernel,
        grid=(t // bm,),
        in_specs=[pl.BlockSpec((bm, d), row), _resident((d, ROUTER_LANES)), _resident((1, ROUTER_LANES)),
                  _resident((ne, d, 2 * D_EXPERT)), _resident((ne * D_EXPERT, d)), _resident((1, d)),
                  _resident((1, d))],
        out_specs=pl.BlockSpec((bm, d), row),
        out_shape=jax.ShapeDtypeStruct((t, d), F32),
        scratch_shapes=[pltpu.VMEM((bm, ne * D_EXPERT), BF16)],
        compiler_params=_params(("parallel",)),
        name="hierarchical_moe",
    )(h, w_r, b_r, w_up.astype(BF16), w_down.reshape(ne * D_EXPERT, d).astype(BF16),
      ln_g.reshape(1, d), ln_b.reshape(1, d))


def _pad_in_weight(w_in):
    d = w_in.shape[0]
    sizes = (768, 768, MLA_Q_RANK, MLA_KV_RANK, MLA_ROPE, 256, 384, 12, 256)
    offs = np.concatenate([[0], np.cumsum(sizes)])
    sb, moba, c_q, c_kv, k_rope, nsa_q, nsa_kv, nsa_g, mem_q = (w_in[:, offs[i]:offs[i + 1]] for i in range(len(sizes)))
    z = lambda n: jnp.zeros((d, n), w_in.dtype)
    q_scale = HEAD_DIM ** -0.5 * LOG2E
    scale_q = lambda qkv: jnp.concatenate([qkv[:, :BRANCH_W] * q_scale, qkv[:, BRANCH_W:]], axis=1)
    return jnp.concatenate([scale_q(sb), scale_q(moba), c_q, c_kv, k_rope, _rotate_half_cols(k_rope), z(64),
                            nsa_kv[:, :128], nsa_g, z(116), nsa_kv[:, 128:], nsa_q * q_scale, mem_q * q_scale], axis=1)


def kernel(x, mem, w_in, g_cq, g_ckv, w_uq, w_ukv, nsa_pe, w_phi_k1, w_phi_k2, w_phi_v1, w_phi_v2, w_mem_kv, w_br,
           w_gate, b_gate, w_out, ln1_g, ln1_b, w_rg, b_rg, w_re, b_re, w_up, w_down, ln2_g, ln2_b):
    b, s_len, d = x.shape
    s = -(-s_len // MOBA_BLOCK) * MOBA_BLOCK
    t = b * s
    n_mem = mem.shape[1]
    h = jnp.pad(x, ((0, 0), (0, s - s_len), (0, 0))).reshape(t, d)
    for l in range(w_in.shape[0]):
        a, f = in_projection(h, _pad_in_weight(w_in[l]).astype(BF16), bm=512)
        a3, f3 = a.reshape(b, s, IN_PAD), f.reshape(b, s, F32_W)
        q, kv, kr = mla_project(f, g_cq[l], g_ckv[l], w_uq[l], w_ukv[l], s, bm=512)
        kc, vct = nsa_compress(f3, nsa_pe[l], w_phi_k1[l], w_phi_k2[l], w_phi_v1[l], w_phi_v2[l])
        mkv = matmul(mem.reshape(b * n_mem, d), w_mem_kv[l].astype(BF16), bm=n_mem, out_dtype=BF16)
        branches = [sb_attention(a3),
                    moba_attention(a3),
                    mla_attention(q.reshape(b, s, -1), kv.reshape(b, s, -1), kr.reshape(b, s, -1)),
                    nsa_attention(a3, f3, kc, vct),
                    cross_attention(a3, mkv.reshape(b, n_mem, -1), qb=512)]
        h = gated_merge(h, [o.reshape(t, BRANCH_W) for o in branches], w_gate[l], b_gate[l], w_br[l], w_out[l],
                        ln1_g[l], ln1_b[l], bm=256)
        h = hierarchical_moe(h, w_rg[l], b_rg[l], w_re[l], b_re[l], w_up[l], w_down[l], ln2_g[l], ln2_b[l], bm=256)
    return h.reshape(b, s, d)[:, :s_len]
```
